```python
import math
import jax
import jax.numpy as jnp
from jax import lax
import numpy as np

D_MODEL = 1024
BATCH = 1
SEQ = 16384
DEPTH = 2
DEC_BATCH = 32
DEC_SEQ = 8
PAST_LEN = 16384
PAGE_SIZE = 128

HEAD_DIM = 64
H_A = 8
H_B = 8
D_A = H_A * HEAD_DIM
D_B = H_B * HEAD_DIM
MOBA_BLOCK = 256
MOBA_TOPK = 3
Q_BLOCK = 128
SCALE = HEAD_DIM ** -0.5
D_DECAY_LORA = 64
D_AAA_LORA = 64
D_GATE_LORA = 128
D_B_IN = 3 * D_B + D_DECAY_LORA + D_AAA_LORA + D_GATE_LORA
D_IN0 = 3 * D_A + D_B_IN
D_C = 2 * D_MODEL
G_C = 8
CHUNK = 128
D_FF = 2816
CONV_W = 3
NORM_EPS = 1e-6
LN_EPS = 1e-5
GN_EPS = 64e-5

kernel_name = 'moba_rwkv7_gmlp_convffn_step'


def rms_norm(x, g):
    xf = x.astype(jnp.float32)
    y = xf * lax.rsqrt(jnp.mean(xf * xf, axis=-1, keepdims=True) + NORM_EPS)
    return (y * g.astype(jnp.float32)).astype(x.dtype)


def layer_norm_f32(x, w, b, eps):
    xf = x.astype(jnp.float32)
    xc = xf - jnp.mean(xf, axis=-1, keepdims=True)
    y = xc * lax.rsqrt(jnp.mean(xc * xc, axis=-1, keepdims=True) + eps)
    return y * w.astype(jnp.float32) + b.astype(jnp.float32)


def alibi_slopes():
    return jnp.exp2(-8.0 * jnp.arange(1, H_A + 1, dtype=jnp.float32) / H_A)


def to_blocks(k):
    b, l, h, d = k.shape
    pad = (-l) % MOBA_BLOCK
    k = jnp.pad(k, ((0, 0), (0, pad), (0, 0), (0, 0)))
    return k.reshape(b, (l + pad) // MOBA_BLOCK, MOBA_BLOCK, h, d)


def moba_core(q, q_pos, k_blocks, v_blocks, k_means):
    bsz, nq, nh, _ = q.shape
    nb = k_blocks.shape[1]
    qf = q.astype(jnp.float32)
    own = q_pos // MOBA_BLOCK
    gate = jnp.einsum('bqhd,bnhd->bqhn', qf, k_means)
    is_past = jnp.arange(nb)[None, :] < own[:, None]
    gate = jnp.where(is_past[None, :, None, :], gate, -jnp.inf)
    if nb < MOBA_TOPK:
        gate = jnp.pad(gate, ((0, 0), (0, 0), (0, 0), (0, MOBA_TOPK - nb)), constant_values=-jnp.inf)
    _, top_idx = lax.top_k(gate, MOBA_TOPK)
    top_idx = jnp.minimum(top_idx, nb - 1)
    own_idx = jnp.broadcast_to(own[None, :, None, None], (bsz, nq, nh, 1)).astype(top_idx.dtype)
    idx = jnp.concatenate([top_idx, own_idx], axis=-1)
    b_ix = jnp.arange(bsz)[:, None, None, None]
    h_ix = jnp.arange(nh)[None, None, :, None]
    kg = k_blocks[b_ix, idx, :, h_ix, :]
    vg = v_blocks[b_ix, idx, :, h_ix, :]
    kpos = idx[..., None] * MOBA_BLOCK + jnp.arange(MOBA_BLOCK, dtype=jnp.int32)
    valid_sel = jnp.concatenate([jnp.arange(MOBA_TOPK)[None, :] < own[:, None],
                                 jnp.ones((nq, 1), dtype=bool)], axis=-1)
    qp = q_pos[None, :, None, None, None]
    mask = valid_sel[None, :, None, :, None] & (kpos <= qp)
    dist = (qp - kpos).astype(jnp.float32)
    slopes = alibi_slopes()
    s = jnp.einsum('bqhd,bqhnkd->bqhnk', qf, kg.astype(jnp.float32)) * SCALE \
        - slopes[None, None, :, None, None] * dist
    s = jnp.where(mask, s, -jnp.inf)
    p = jax.nn.softmax(s.reshape(bsz, nq, nh, -1), axis=-1).reshape(s.shape)
    out = jnp.einsum('bqhnk,bqhnkd->bqhd', p, vg.astype(jnp.float32))
    return out.astype(q.dtype)


def moba_prompt(q, k, v):
    b, t, h, d = q.shape
    kb, vb = to_blocks(k), to_blocks(v)
    km = jnp.mean(kb.astype(jnp.float32), axis=2)
    nqb = t // Q_BLOCK
    qb = jnp.moveaxis(q.reshape(b, nqb, Q_BLOCK, h, d), 1, 0)
    pos = jnp.arange(t, dtype=jnp.int32).reshape(nqb, Q_BLOCK)
    out = lax.map(lambda a: moba_core(a[0], a[1], kb, vb, km), (qb, pos))
    return jnp.moveaxis(out, 0, 1).reshape(b, t, h, d)


def moba_sample(q, k, v, cache_k, cache_v, page_table):
    _, s, h, d = q.shape
    past = page_table.shape[1] * cache_k.shape[1]
    q_pos = past + jnp.arange(s, dtype=jnp.int32)

    def one_seq(a):
        q_i, k_i, v_i, pt_i = a
        k_full = jnp.concatenate([cache_k[pt_i].reshape(past, h, d).astype(k_i.dtype), k_i], axis=0)
        v_full = jnp.concatenate([cache_v[pt_i].reshape(past, h, d).astype(v_i.dtype), v_i], axis=0)
        kb, vb = to_blocks(k_full[None]), to_blocks(v_full[None])
        km = jnp.mean(kb.astype(jnp.float32), axis=2)
        return moba_core(q_i[None], q_pos, kb, vb, km)[0]

    return lax.map(one_seq, (q, k, v, page_table))


def wkv7_scan(r, decay, k, v, kk, a, s0):
    def step(s, inp):
        r_t, w_t, k_t, v_t, kk_t, a_t = inp
        sa = jnp.einsum('bhij,bhj->bhi', s, -kk_t)
        s = s * w_t[:, :, None, :] + sa[..., None] * (kk_t * a_t)[:, :, None, :] \
            + v_t[..., None] * k_t[:, :, None, :]
        return s, jnp.einsum('bhij,bhj->bhi', s, r_t)

    xs = tuple(jnp.moveaxis(z.astype(jnp.float32), 1, 0) for z in (r, decay, k, v, kk, a))
    s_fin, ys = lax.scan(step, s0.astype(jnp.float32), xs)
    return jnp.moveaxis(ys, 0, 1), s_fin


def rwkv7_mix(p_b, p_prev, s0, mu, w0, w2, a0, a2, g2, k_k, k_a, r_k, gn_w, gn_b):
    dt = p_b.dtype
    b, t, _ = p_b.shape
    xm = p_b + mu * (p_prev - p_b)
    o1 = 3 * D_B
    o2 = o1 + D_DECAY_LORA
    o3 = o2 + D_AAA_LORA
    r, k, v = xm[..., :D_B], xm[..., D_B:2 * D_B], xm[..., 2 * D_B:o1]
    xw, xa, xg = xm[..., o1:o2], xm[..., o2:o3], xm[..., o3:]
    logw = -jax.nn.softplus(-(w0 + jnp.tanh(xw) @ w2).astype(jnp.float32)) - 0.5
    decay = jnp.exp(-jnp.exp(logw))
    a = jax.nn.sigmoid(a0 + xa @ a2)
    g = jax.nn.sigmoid(xg) @ g2

    def heads(z):
        return z.reshape(b, t, H_B, HEAD_DIM)

    kk = heads(k * k_k).astype(jnp.float32)
    kk = kk / jnp.maximum(jnp.sqrt(jnp.sum(kk * kk, axis=-1, keepdims=True)), 1e-12)
    k = k * (1.0 + (a - 1.0) * k_a)
    rh, kh, vh = heads(r), heads(k), heads(v)
    y, s_fin = wkv7_scan(rh, heads(decay), kh, vh, kk, heads(a), s0)
    y = layer_norm_f32(y, gn_w.reshape(H_B, HEAD_DIM), gn_b.reshape(H_B, HEAD_DIM), GN_EPS)
    bonus = jnp.sum((rh * kh * r_k).astype(jnp.float32), axis=-1, keepdims=True) * vh.astype(jnp.float32)
    y = (y + bonus).reshape(b, t, D_B).astype(dt)
    return y * g, s_fin.astype(s0.dtype)


def mixer_ab(xn, prev_xn, wkv0, attend, prm):
    b, t, _ = xn.shape
    w_in = prm['l0_w_in']
    p = xn @ w_in
    q = p[..., :D_A].reshape(b, t, H_A, HEAD_DIM)
    k = p[..., D_A:2 * D_A].reshape(b, t, H_A, HEAD_DIM)
    v = p[..., 2 * D_A:3 * D_A].reshape(b, t, H_A, HEAD_DIM)
    p_b = p[..., 3 * D_A:]
    prev_row = (prev_xn @ w_in[:, 3 * D_A:])[:, None, :]
    p_prev = jnp.concatenate([prev_row, p_b[:, :-1]], axis=1)
    o_a = attend(q, k, v).reshape(b, t, D_A)
    o_b, wkv = rwkv7_mix(p_b, p_prev, wkv0, prm['rw_mu'], prm['rw_w0'], prm['rw_w2'], prm['rw_a0'],
                         prm['rw_a2'], prm['rw_g2'], prm['rw_k_k'], prm['rw_k_a'], prm['rw_r_k'],
                         prm['rw_gn_w'], prm['rw_gn_b'])
    out = jnp.concatenate([o_a, o_b], axis=-1) @ prm['l0_w_out']
    return out, k, v, wkv


def mixer_c(xn, prm):
    z = jax.nn.gelu(xn @ prm['l1_w_in'])
    u, v = z[..., :D_C], z[..., D_C:]
    v = layer_norm_f32(v, prm['gm_ln_w'], prm['gm_ln_b'], LN_EPS).astype(z.dtype)
    b, t, _ = v.shape
    pad = (-t) % CHUNK
    vc = jnp.pad(v, ((0, 0), (0, pad), (0, 0))).reshape(b, (t + pad) // CHUNK, CHUNK, G_C, D_C // G_C)
    tri = jnp.tril(jnp.ones((CHUNK, CHUNK), dtype=bool))
    wm = jnp.where(tri[None], prm['gm_ws'], 0.0).astype(vc.dtype)
    mixed = jnp.einsum('gij,bcjgd->bcigd', wm, vc) + prm['gm_bs'].T[None, None, :, :, None]
    mixed = mixed.reshape(b, t + pad, D_C)[:, :t]
    return (u * mixed) @ prm['l1_w_out'], v


def conv_ffn(h, prev_rows, w_up, conv_w, conv_b, w_down):
    up = h @ w_up
    t = up.shape[1]
    ext = jnp.concatenate([prev_rows.astype(up.dtype), up], axis=1)
    c = conv_b + conv_w[0] * ext[:, 0:t]
    for j in range(1, CONV_W):
        c = c + conv_w[j] * ext[:, j:j + t]
    gate, val = c[..., :D_FF], c[..., D_FF:]
    return (jax.nn.gelu(gate) * val) @ w_down, ext[:, t:]


def trunk(x, attend, prev_xn, wkv0, conv0, prm):
    h = x
    conv_rows = []
    for layer in range(DEPTH):
        if layer % 2 == 0:
            xn = rms_norm(h, prm['l0_norm_pre'])
            mix, k_new, v_new, wkv_new = mixer_ab(xn, prev_xn, wkv0, attend, prm)
            shift_new = xn[:, -1]
            h = h + rms_norm(mix, prm['l0_norm_post'])
        else:
            mix, chunk_v = mixer_c(rms_norm(h, prm['l1_norm_pre']), prm)
            h = h + rms_norm(mix, prm['l1_norm_post'])
        f, rows = conv_ffn(rms_norm(h, prm['ffn_norm_pre'][layer]), conv0[layer], prm['ffn_w_up'][layer],
                           prm['ffn_conv_w'][layer], prm['ffn_conv_b'][layer], prm['ffn_w_down'][layer])
        h = h + rms_norm(f, prm['ffn_norm_post'][layer])
        conv_rows.append(rows)
    return h, k_new, v_new, wkv_new, shift_new, chunk_v, jnp.stack(conv_rows)


def setup_inputs(seed: int = 0) -> dict:
    key = jax.random.key(seed)
    ks = iter(jax.random.split(key, 48))

    def nrm(shape, scale):
        return jax.random.normal(next(ks), shape, jnp.float32) * scale

    def uni(shape, lo, hi):
        return jax.random.uniform(next(ks), shape, jnp.float32, minval=lo, maxval=hi)

    n_pages = PAST_LEN // PAGE_SIZE
    n_used = DEC_BATCH * n_pages
    n_pool = n_used + max(1, n_used // 4)
    page_table = jax.random.permutation(next(ks), n_pool)[:n_used].reshape(DEC_BATCH, n_pages).astype(jnp.int32)
    return {
        'x_prompt': nrm((BATCH, SEQ, D_MODEL), 1.0),
        'x_sample': nrm((DEC_BATCH, DEC_SEQ, D_MODEL), 1.0),
        'cache_k': nrm((n_pool, PAGE_SIZE, H_A, HEAD_DIM), 1.0),
        'cache_v': nrm((n_pool, PAGE_SIZE, H_A, HEAD_DIM), 1.0),
        'state_wkv': nrm((DEC_BATCH, H_B, HEAD_DIM, HEAD_DIM), 0.1),
        'state_shift': nrm((DEC_BATCH, D_MODEL), 1.0),
        'state_ffn_conv': nrm((DEPTH, DEC_BATCH, CONV_W - 1, 2 * D_FF), 1.0),
        'page_table': page_table,
        'l0_norm_pre': 1.0 + nrm((D_MODEL,), 0.05),
        'l0_norm_post': 1.0 + nrm((D_MODEL,), 0.05),
        'l0_w_in': nrm((D_MODEL, D_IN0), D_MODEL ** -0.5),
        'rw_mu': uni((D_B_IN,), 0.0, 1.0),
        'rw_w0': uni((D_B,), -7.0, 0.0),
        'rw_w2': nrm((D_DECAY_LORA, D_B), 0.5 * D_DECAY_LORA ** -0.5),
        'rw_a0': nrm((D_B,), 0.1),
        'rw_a2': nrm((D_AAA_LORA, D_B), D_AAA_LORA ** -0.5),
        'rw_g2': nrm((D_GATE_LORA, D_B), D_GATE_LORA ** -0.5),
        'rw_k_k': 0.85 + nrm((D_B,), 0.05),
        'rw_k_a': 1.0 + nrm((D_B,), 0.05),
        'rw_r_k': nrm((H_B, HEAD_DIM), 0.1),
        'rw_gn_w': 1.0 + nrm((D_B,), 0.05),
        'rw_gn_b': nrm((D_B,), 0.02),
        'l0_w_out': nrm((D_A + D_B, D_MODEL), (D_A + D_B) ** -0.5),
        'l1_norm_pre': 1.0 + nrm((D_MODEL,), 0.05),
        'l1_norm_post': 1.0 + nrm((D_MODEL,), 0.05),
        'l1_w_in': nrm((D_MODEL, 2 * D_C), D_MODEL ** -0.5),
        'gm_ln_w': 1.0 + nrm((D_C,), 0.05),
        'gm_ln_b': nrm((D_C,), 0.02),
        'gm_ws': nrm((G_C, CHUNK, CHUNK), 0.5 * CHUNK ** -0.5),
        'gm_bs': 1.0 + nrm((G_C, CHUNK), 0.01),
        'l1_w_out': nrm((D_C, D_MODEL), D_C ** -0.5),
        'ffn_norm_pre': 1.0 + nrm((DEPTH, D_MODEL), 0.05),
        'ffn_norm_post': 1.0 + nrm((DEPTH, D_MODEL), 0.05),
        'ffn_w_up': nrm((DEPTH, D_MODEL, 2 * D_FF), D_MODEL ** -0.5),
        'ffn_conv_w': nrm((DEPTH, CONV_W, 2 * D_FF), CONV_W ** -0.5),
        'ffn_conv_b': nrm((DEPTH, 2 * D_FF), 0.02),
        'ffn_w_down': nrm((DEPTH, D_FF, D_MODEL), D_FF ** -0.5),
    }


def reference(x_prompt, x_sample, cache_k, cache_v, state_wkv, state_shift, state_ffn_conv, page_table,
              l0_norm_pre, l0_norm_post, l0_w_in, rw_mu, rw_w0, rw_w2, rw_a0, rw_a2, rw_g2, rw_k_k, rw_k_a,
              rw_r_k, rw_gn_w, rw_gn_b, l0_w_out, l1_norm_pre, l1_norm_post, l1_w_in, gm_ln_w, gm_ln_b,
              gm_ws, gm_bs, l1_w_out, ffn_norm_pre, ffn_norm_post, ffn_w_up, ffn_conv_w, ffn_conv_b,
              ffn_w_down):
    prm = dict(l0_norm_pre=l0_norm_pre, l0_norm_post=l0_norm_post, l0_w_in=l0_w_in, rw_mu=rw_mu,
               rw_w0=rw_w0, rw_w2=rw_w2, rw_a0=rw_a0, rw_a2=rw_a2, rw_g2=rw_g2, rw_k_k=rw_k_k,
               rw_k_a=rw_k_a, rw_r_k=rw_r_k, rw_gn_w=rw_gn_w, rw_gn_b=rw_gn_b, l0_w_out=l0_w_out,
               l1_norm_pre=l1_norm_pre, l1_norm_post=l1_norm_post, l1_w_in=l1_w_in, gm_ln_w=gm_ln_w,
               gm_ln_b=gm_ln_b, gm_ws=gm_ws, gm_bs=gm_bs, l1_w_out=l1_w_out, ffn_norm_pre=ffn_norm_pre,
               ffn_norm_post=ffn_norm_post, ffn_w_up=ffn_w_up, ffn_conv_w=ffn_conv_w,
               ffn_conv_b=ffn_conv_b, ffn_w_down=ffn_w_down)

    bp = x_prompt.shape[0]
    dt = x_prompt.dtype
    wkv0 = jnp.zeros((bp, H_B, HEAD_DIM, HEAD_DIM), dt)
    shift0 = jnp.zeros((bp, D_MODEL), dt)
    conv0 = jnp.zeros((DEPTH, bp, CONV_W - 1, 2 * D_FF), dt)
    (y_prompt, k_prompt, v_prompt, wkv_prompt, shift_prompt, _prompt_chunk_v,
     ffn_conv_prompt) = trunk(x_prompt, moba_prompt, shift0, wkv0, conv0, prm)

    def attend_sample(q, k, v):
        return moba_sample(q, k, v, cache_k, cache_v, page_table)

    (y_sample, k_sample, v_sample, wkv_sample, shift_sample, chunk_v_sample,
     ffn_conv_sample) = trunk(x_sample, attend_sample, state_shift, state_wkv, state_ffn_conv, prm)

    return (y_prompt, y_sample, k_prompt, v_prompt, k_sample, v_sample, wkv_prompt, wkv_sample,
            shift_prompt, shift_sample, chunk_v_sample, ffn_conv_prompt, ffn_conv_sample)
```

```python
import functools

import jax
import jax.numpy as jnp
from jax import lax
from jax.experimental import pallas as pl
from jax.experimental.pallas import tpu as pltpu

F32 = jnp.float32
BF16 = jnp.bfloat16
HI = lax.Precision.HIGHEST

HEAD_DIM = 64
N_HEADS = 8
D_HEADS = N_HEADS * HEAD_DIM
MOBA_BLOCK = 256
MOBA_TOPK = 3
SCALE = HEAD_DIM ** -0.5
D_DECAY_LORA = 64
D_AAA_LORA = 64
GMLP_GROUPS = 8
GMLP_CHUNK = 128
CONV_W = 3
NORM_EPS = 1e-6
LN_EPS = 1e-5
GN_EPS = 64e-5
NEG = -1e30
WKV_CHUNK = 64
V7X_VMEM_LIMIT = 56 * 1024 * 1024


def _cparams(sem, vmem=V7X_VMEM_LIMIT):
    return pltpu.CompilerParams(dimension_semantics=sem, vmem_limit_bytes=vmem)


def _mm(a, b, prec=None):
    return jnp.dot(a, b, precision=prec, preferred_element_type=F32)


def _mm_nt(a, b, prec=None):
    return lax.dot_general(a, b, (((1,), (1,)), ((), ())), precision=prec, preferred_element_type=F32)


def _rms(x, g):
    return x * lax.rsqrt(jnp.mean(x * x, axis=-1, keepdims=True) + NORM_EPS) * g


def _norm_matmul_kernel(x_ref, g_ref, w_ref, o_ref, xn_ref):
    @pl.when(pl.program_id(1) == 0)
    def _():
        xn_ref[...] = _rms(x_ref[...], g_ref[...]).astype(BF16)

    o_ref[...] = _mm(xn_ref[...], w_ref[...])


def _norm_matmul(x, g, w, tm, tn):
    m, kd = x.shape
    n = w.shape[1]
    return pl.pallas_call(
        _norm_matmul_kernel,
        grid=(m // tm, n // tn),
        in_specs=[pl.BlockSpec((tm, kd), lambda i, j: (i, 0)),
                  pl.BlockSpec((1, kd), lambda i, j: (0, 0)),
                  pl.BlockSpec((kd, tn), lambda i, j: (0, j))],
        out_specs=pl.BlockSpec((tm, tn), lambda i, j: (i, j)),
        out_shape=jax.ShapeDtypeStruct((m, n), F32),
        scratch_shapes=[pltpu.VMEM((tm, kd), BF16)],
        compiler_params=_cparams(("parallel", "arbitrary")),
        name="norm_matmul",
    )(x, g.reshape(1, kd), w)


def _shift_rows_kernel(x_ref, g_ref, prev_ref, w_ref, xn_ref, pp_ref):
    xn_ref[...] = _rms(x_ref[...], g_ref[...])
    pp_ref[...] = _mm(prev_ref[...].astype(BF16), w_ref[...])


def _shift_rows(x_last, g, prev_xn, w_b):
    b, kd = x_last.shape
    n = w_b.shape[1]
    bp = -(-b // 8) * 8
    pad = ((0, bp - b), (0, 0))
    xn, pp = pl.pallas_call(
        _shift_rows_kernel,
        out_shape=(jax.ShapeDtypeStruct((bp, kd), F32), jax.ShapeDtypeStruct((bp, n), F32)),
        compiler_params=_cparams(None),
        name="shift_rows",
    )(jnp.pad(x_last, pad), g.reshape(1, kd), jnp.pad(prev_xn, pad), w_b)
    return xn[:b], pp[:b]


def _kmeans_kernel(k_ref, o_ref):
    k = k_ref[0]
    nb = k.shape[0] // MOBA_BLOCK
    o_ref[0] = jnp.mean(k.reshape(nb, MOBA_BLOCK, HEAD_DIM), axis=1)


def _moba_prompt_kernel(slope_ref, qT_ref, km_ref, k_ref, vT_ref, oT_ref, sel_scr):
    i = pl.program_id(1)
    nb = km_ref.shape[1]
    blk = MOBA_BLOCK
    qf = qT_ref[0]
    gate = _mm(km_ref[0], qf, HI)
    nidx = lax.broadcasted_iota(jnp.int32, gate.shape, 0)
    g = jnp.where(nidx < i, gate, -jnp.inf)
    sel = jnp.zeros(gate.shape, F32)
    for _ in range(MOBA_TOPK):
        top = jnp.max(g, axis=0, keepdims=True)
        idx = jnp.min(jnp.where(g == top, nidx, nb), axis=0, keepdims=True)
        pick = (nidx == idx) & (top > -jnp.inf)
        sel = jnp.where(pick, 1.0, sel)
        g = jnp.where(pick, -jnp.inf, g)
    sel_scr[...] = sel

    qs = (qf * SCALE).astype(BF16)
    slope = slope_ref[0]
    ki = lax.broadcasted_iota(jnp.int32, (blk, blk), 0)
    qi = lax.broadcasted_iota(jnp.int32, (blk, blk), 1)
    dm = (ki - qi).astype(F32)

    def scores(n):
        start = pl.multiple_of(n * blk, blk)
        kb = k_ref[0, pl.ds(start, blk), :]
        return _mm(kb, qs) + slope * (dm + ((n - i) * blk).astype(F32))

    def pv(n, p):
        start = pl.multiple_of(n * blk, blk)
        return _mm(vT_ref[0, :, pl.ds(start, blk)], p.astype(BF16))

    s = jnp.where(dm <= 0.0, scores(i), NEG)
    m0 = jnp.max(s, axis=0, keepdims=True)
    p = jnp.exp(s - m0)
    l0 = jnp.sum(p, axis=0, keepdims=True)
    acc0 = pv(i, p)

    def body(n, carry):
        m, l, acc = carry
        s = jnp.where(sel_scr[pl.ds(n, 1), :] > 0.0, scores(n), NEG)
        m_new = jnp.maximum(m, jnp.max(s, axis=0, keepdims=True))
        alpha = jnp.exp(m - m_new)
        p = jnp.exp(s - m_new)
        return m_new, alpha * l + jnp.sum(p, axis=0, keepdims=True), alpha * acc + pv(n, p)

    _, l, acc = lax.fori_loop(0, i, body, (m0, l0, acc0))
    oT_ref[0] = acc / l


def _moba_prompt(q, k, v, slopes):
    t = q.shape[0]
    assert t % MOBA_BLOCK == 0
    nb = t // MOBA_BLOCK

    def heads_t(z):
        return z.reshape(t, N_HEADS, HEAD_DIM).transpose(1, 2, 0)

    k_hm = k.reshape(t, N_HEADS, HEAD_DIM).transpose(1, 0, 2)
    kmean = pl.pallas_call(
        _kmeans_kernel,
        grid=(N_HEADS,),
        in_specs=[pl.BlockSpec((1, t, HEAD_DIM), lambda h: (h, 0, 0))],
        out_specs=pl.BlockSpec((1, nb, HEAD_DIM), lambda h: (h, 0, 0)),
        out_shape=jax.ShapeDtypeStruct((N_HEADS, nb, HEAD_DIM), F32),
        compiler_params=_cparams(("parallel",)),
        name="moba_kmeans",
    )(k_hm)
    oT = pl.pallas_call(
        _moba_prompt_kernel,
        grid=(N_HEADS, nb),
        in_specs=[pl.BlockSpec((1, 1, 1), lambda h, i: (h, 0, 0)),
                  pl.BlockSpec((1, HEAD_DIM, MOBA_BLOCK), lambda h, i: (h, 0, i)),
                  pl.BlockSpec((1, nb, HEAD_DIM), lambda h, i: (h, 0, 0)),
                  pl.BlockSpec((1, t, HEAD_DIM), lambda h, i: (h, 0, 0)),
                  pl.BlockSpec((1, HEAD_DIM, t), lambda h, i: (h, 0, 0))],
        out_specs=pl.BlockSpec((1, HEAD_DIM, MOBA_BLOCK), lambda h, i: (h, 0, i)),
        out_shape=jax.ShapeDtypeStruct((N_HEADS, HEAD_DIM, t), F32),
        scratch_shapes=[pltpu.VMEM((nb, MOBA_BLOCK), F32)],
        compiler_params=_cparams(("parallel", "arbitrary")),
        name="moba_prompt",
    )(slopes.reshape(N_HEADS, 1, 1), heads_t(q), kmean, k_hm.astype(BF16), heads_t(v).astype(BF16))
    return oT.transpose(2, 0, 1).reshape(t, D_HEADS)


def _fold_heads(x, mask):
    x = jnp.where(mask, x, 0.0)
    out = x[:, 0:HEAD_DIM]
    for h in range(1, N_HEADS):
        out = out + x[:, h * HEAD_DIM:(h + 1) * HEAD_DIM]
    return out


def _moba_sample_kernel(pt_ref, slope_ref, q_ref, kn_ref, vn_ref, ka_ref, kb_ref, va_ref, vb_ref, o_ref,
                        qbd_scr, km_scr, m_scr, l_scr, acc_scr, *, past):
    n = pl.program_id(1)
    nb = pl.num_programs(1)
    nq = q_ref.shape[1]
    rows = N_HEADS * nq
    row = lax.broadcasted_iota(jnp.int32, (rows, D_HEADS), 0)
    col = lax.broadcasted_iota(jnp.int32, (rows, D_HEADS), 1)
    head_mask = (row // nq) == (col // HEAD_DIM)

    @pl.when(n == 0)
    def _():
        q = q_ref[0]
        qbd_scr[...] = jnp.where(head_mask, jnp.concatenate([q] * N_HEADS, axis=0), 0.0)

    qbd = qbd_scr[...]
    qs = (qbd * SCALE).astype(BF16)
    slope = slope_ref[...]
    kblk = jnp.concatenate([ka_ref[0], kb_ref[0]], axis=0)
    vblk = jnp.concatenate([va_ref[0], vb_ref[0]], axis=0)
    km_scr[pl.ds(n, 1), :] = jnp.mean(kblk, axis=0, keepdims=True)

    qrow = lax.broadcasted_iota(jnp.int32, (rows, MOBA_BLOCK), 0) % nq
    kcol = lax.broadcasted_iota(jnp.int32, (rows, MOBA_BLOCK), 1)
    dist = (qrow - kcol).astype(F32) + (past - n * MOBA_BLOCK).astype(F32)
    s = _mm_nt(qs, kblk.astype(BF16)) - slope * dist
    m = jnp.max(s, axis=1, keepdims=True)
    p = jnp.exp(s - m)
    m_scr[n] = m
    l_scr[n] = jnp.sum(p, axis=1, keepdims=True)
    acc_scr[n] = _fold_heads(_mm(p.astype(BF16), vblk.astype(BF16)), head_mask)

    @pl.when(n == nb - 1)
    def _():
        nblk = km_scr.shape[0]
        gate = _mm_nt(qbd, km_scr[...], HI)
        bidx = lax.broadcasted_iota(jnp.int32, gate.shape, 1)
        g = gate
        sel = jnp.zeros(gate.shape, F32)
        for _ in range(MOBA_TOPK):
            top = jnp.max(g, axis=1, keepdims=True)
            idx = jnp.min(jnp.where(g == top, bidx, nblk), axis=1, keepdims=True)
            pick = (bidx == idx) & (top > -jnp.inf)
            sel = jnp.where(pick, 1.0, sel)
            g = jnp.where(pick, -jnp.inf, g)

        nk = kn_ref.shape[1]
        qrow_o = lax.broadcasted_iota(jnp.int32, (rows, nk), 0) % nq
        kcol_o = lax.broadcasted_iota(jnp.int32, (rows, nk), 1)
        dist_o = (qrow_o - kcol_o).astype(F32)
        s_o = _mm_nt(qs, kn_ref[0].astype(BF16)) - slope * dist_o
        s_o = jnp.where(dist_o >= 0.0, s_o, NEG)
        m_o = jnp.max(s_o, axis=1, keepdims=True)
        p_o = jnp.exp(s_o - m_o)
        l_o = jnp.sum(p_o, axis=1, keepdims=True)
        acc_o = _fold_heads(_mm(p_o.astype(BF16), vn_ref[0].astype(BF16)), head_mask)

        m_fin = m_o
        for b in range(nblk):
            m_fin = jnp.maximum(m_fin, jnp.where(sel[:, b:b + 1] > 0.0, m_scr[b], NEG))
        w_o = jnp.exp(m_o - m_fin)
        l_fin = w_o * l_o
        acc = w_o * acc_o
        for b in range(nblk):
            w = jnp.where(sel[:, b:b + 1] > 0.0, jnp.exp(m_scr[b] - m_fin), 0.0)
            l_fin = l_fin + w * l_scr[b]
            acc = acc + w * acc_scr[b]
        out = acc / l_fin
        o_ref[0] = jnp.concatenate([out[h * nq:(h + 1) * nq] for h in range(N_HEADS)], axis=1)


def _moba_sample(q, k, v, cache_k, cache_v, page_table, slopes):
    b, s, _ = q.shape
    n_pool, page, _, _ = cache_k.shape
    n_pages = page_table.shape[1]
    past = n_pages * page
    pages_per_blk = MOBA_BLOCK // page
    assert pages_per_blk == 2 and past % MOBA_BLOCK == 0
    nb = past // MOBA_BLOCK
    rows = N_HEADS * s
    nk = 16
    ck = cache_k.reshape(n_pool, page, D_HEADS)
    cv = cache_v.reshape(n_pool, page, D_HEADS)
    kn = jnp.pad(k, ((0, 0), (0, nk - s), (0, 0)))
    vn = jnp.pad(v, ((0, 0), (0, nk - s), (0, 0)))
    slope_col = jnp.repeat(slopes, s).reshape(rows, 1)

    def page_spec(j):
        return pl.BlockSpec((1, page, D_HEADS), lambda i, n, pt: (pt[i * n_pages + 2 * n + j], 0, 0))

    def seq_spec(r):
        return pl.BlockSpec((1, r, D_HEADS), lambda i, n, pt: (i, 0, 0))

    grid_spec = pltpu.PrefetchScalarGridSpec(
        num_scalar_prefetch=1,
        grid=(b, nb),
        in_specs=[pl.BlockSpec((rows, 1), lambda i, n, pt: (0, 0)),
                  seq_spec(s), seq_spec(nk), seq_spec(nk),
                  page_spec(0), page_spec(1), page_spec(0), page_spec(1)],
        out_specs=seq_spec(s),
        scratch_shapes=[pltpu.VMEM((rows, D_HEADS), F32),
                        pltpu.VMEM((nb, D_HEADS), F32),
                        pltpu.VMEM((nb, rows, 1), F32),
                        pltpu.VMEM((nb, rows, 1), F32),
                        pltpu.VMEM((nb, rows, HEAD_DIM), F32)],
    )
    return pl.pallas_call(
        functools.partial(_moba_sample_kernel, past=past),
        grid_spec=grid_spec,
        out_shape=jax.ShapeDtypeStruct((b, s, D_HEADS), F32),
        compiler_params=_cparams(("parallel", "arbitrary")),
        name="moba_sample",
    )(page_table.reshape(-1).astype(jnp.int32), slope_col, q, kn, vn, ck, ck, cv, cv)


def _softplus(z):
    return jnp.maximum(z, 0.0) + jnp.log1p(jnp.exp(-jnp.abs(z)))


def _rwkv_prep_kernel(pb_ref, first_ref, mu_ref, w0_ref, w2_ref, a0_ref, a2_ref, g2_ref, kk_ref, ka_ref,
                      rk_ref, gm_ref, r_o, lw_o, k_o, v_o, kk_o, b_o, g_o, bonus_o, *, seq_len):
    pb = pb_ref[...]
    tm = pb.shape[0]
    row = lax.broadcasted_iota(jnp.int32, (tm, 1), 0)
    if seq_len is None:
        starts, first = row == 0, first_ref[0]
    else:
        starts, first = (row % seq_len) == 0, first_ref[...]
    prev = jnp.where(starts, first, pltpu.roll(pb, 1, axis=0))
    xm = pb + mu_ref[...] * (prev - pb)
    d = D_HEADS
    o1, o2, o3 = 3 * d, 3 * d + D_DECAY_LORA, 3 * d + D_DECAY_LORA + D_AAA_LORA
    r, k, v = xm[:, 0:d], xm[:, d:2 * d], xm[:, 2 * d:o1]
    xw, xa, xg = xm[:, o1:o2], xm[:, o2:o3], xm[:, o3:]
    logw = -_softplus(-(w0_ref[...] + _mm(jnp.tanh(xw), w2_ref[...]))) - 0.5
    a = jax.nn.sigmoid(a0_ref[...] + _mm(xa, a2_ref[...]))
    gmat = gm_ref[...]
    kk = k * kk_ref[...]
    kk = kk / jnp.maximum(jnp.sqrt(_mm(kk * kk, gmat, HI)), 1e-12)
    k = k * (1.0 + (a - 1.0) * ka_ref[...])
    r_o[...] = r
    lw_o[...] = -jnp.exp(logw)
    k_o[...] = k
    v_o[...] = v
    kk_o[...] = kk
    b_o[...] = kk * a
    g_o[...] = _mm(jax.nn.sigmoid(xg), g2_ref[...])
    bonus_o[...] = _mm(r * k * rk_ref[...], gmat, HI) * v


def _rwkv_prep(pb, first, prm, gmat, tm, seq_len):
    m, n = pb.shape
    d = D_HEADS
    if seq_len is None:
        first_spec = pl.BlockSpec((1, 1, n), lambda i: (i, 0, 0))
    else:
        first_spec = pl.BlockSpec((tm, n), lambda i: (i, 0))

    def full(shape):
        return pl.BlockSpec(shape, lambda i: (0,) * len(shape))

    def vec(z):
        return z.reshape(1, -1)

    outs = pl.pallas_call(
        functools.partial(_rwkv_prep_kernel, seq_len=seq_len),
        grid=(m // tm,),
        in_specs=[pl.BlockSpec((tm, n), lambda i: (i, 0)), first_spec,
                  full((1, n)), full((1, d)), full((D_DECAY_LORA, d)), full((1, d)), full((D_AAA_LORA, d)),
                  full((n - 3 * d - D_DECAY_LORA - D_AAA_LORA, d)), full((1, d)), full((1, d)), full((1, d)),
                  full((d, d))],
        out_specs=[pl.BlockSpec((tm, d), lambda i: (i, 0))] * 8,
        out_shape=[jax.ShapeDtypeStruct((m, d), F32)] * 8,
        compiler_params=_cparams(("parallel",)),
        name="rwkv_prep",
    )(pb, first, vec(prm['rw_mu']), vec(prm['rw_w0']), prm['rw_w2'], vec(prm['rw_a0']), prm['rw_a2'],
      prm['rw_g2'], vec(prm['rw_k_k']), vec(prm['rw_k_a']), vec(prm['rw_r_k']), gmat)
    return outs


def _wkv_chunk(r, lw, k, v, kk, b, kkT, lwT, vT, s, consts):
    lincl, uincl, eye, low, upper = consts
    cl = _mm(lincl, lw, HI)
    clT = _mm(lwT, uincl, HI)
    tot = cl[-1:, :]
    e_neg = jnp.exp(-cl)
    rho, beta, kap = r * jnp.exp(cl), b * e_neg, k * e_neg
    e_end = jnp.exp(tot - cl)
    b_end, k_end = b * e_end, k * e_end
    alphaT = kkT * jnp.exp(clT - lwT)
    lt = jnp.where(upper, _mm(beta, alphaT, HI), 0.0)
    akT = jnp.where(upper, _mm(kap, alphaT, HI), 0.0)
    arb = jnp.where(low, _mm_nt(rho, beta, HI), 0.0)
    ark = jnp.where(low, _mm_nt(rho, kap, HI), 0.0)
    x = eye - lt
    pw = _mm(lt, lt, HI)
    n = 2
    while n < lt.shape[0]:
        x = x + _mm(x, pw, HI)
        n *= 2
        if n < lt.shape[0]:
            pw = _mm(pw, pw, HI)
    paT = -_mm(alphaT, x, HI)
    pvT = -_mm(_mm(vT, akT, HI), x, HI)
    rho2 = rho + _mm_nt(arb, paT, HI)
    y = _mm_nt(rho2, s, HI) + _mm_nt(arb, pvT, HI) + _mm(ark, v, HI)
    s_new = s * jnp.exp(tot) + _mm(s, _mm(paT, b_end, HI), HI) + _mm(pvT, b_end, HI) + _mm(vT, k_end, HI)
    return y, s_new


def _rwkv_scan_kernel(r_ref, lw_ref, k_ref, v_ref, kk_ref, b_ref, kkT_ref, lwT_ref, vT_ref, s0_ref,
                      y_ref, sfin_ref, s_scr, *, chunk, chunks_per_step):
    j = pl.program_id(1)

    @pl.when(j == 0)
    def _():
        s_scr[...] = s0_ref[0]

    ii = lax.broadcasted_iota(jnp.int32, (chunk, chunk), 0)
    jj = lax.broadcasted_iota(jnp.int32, (chunk, chunk), 1)
    consts = ((ii >= jj).astype(F32), (ii <= jj).astype(F32), (ii == jj).astype(F32), ii >= jj, ii < jj)
    for c in range(chunks_per_step):
        rows = slice(c * chunk, (c + 1) * chunk)
        for h in range(N_HEADS):
            cols = slice(h * HEAD_DIM, (h + 1) * HEAD_DIM)
            y, s_new = _wkv_chunk(r_ref[rows, cols], lw_ref[rows, cols], k_ref[rows, cols], v_ref[rows, cols],
                                  kk_ref[rows, cols], b_ref[rows, cols], kkT_ref[0, cols, rows],
                                  lwT_ref[0, cols, rows], vT_ref[0, cols, rows], s_scr[h], consts)
            y_ref[rows, cols] = y
            s_scr[h] = s_new

    @pl.when(j == pl.num_programs(1) - 1)
    def _():
        sfin_ref[0] = s_scr[...]


def _rwkv_scan(r, lw, k, v, kk, b, s0, seq_len, chunks_per_step):
    m, d = r.shape
    nseq = m // seq_len
    step = WKV_CHUNK * chunks_per_step
    nsteps = seq_len // step

    def tr(z):
        return z.reshape(nseq, seq_len, d).transpose(0, 2, 1)

    tm_spec = pl.BlockSpec((step, d), lambda i, j: (i * nsteps + j, 0))
    fm_spec = pl.BlockSpec((1, d, step), lambda i, j: (i, 0, j))
    st_spec = pl.BlockSpec((1, N_HEADS, HEAD_DIM, HEAD_DIM), lambda i, j: (i, 0, 0, 0))
    return pl.pallas_call(
        functools.partial(_rwkv_scan_kernel, chunk=WKV_CHUNK, chunks_per_step=chunks_per_step),
        grid=(nseq, nsteps),
        in_specs=[tm_spec] * 6 + [fm_spec] * 3 + [st_spec],
        out_specs=[tm_spec, st_spec],
        out_shape=[jax.ShapeDtypeStruct((m, d), F32),
                   jax.ShapeDtypeStruct((nseq, N_HEADS, HEAD_DIM, HEAD_DIM), F32)],
        scratch_shapes=[pltpu.VMEM((N_HEADS, HEAD_DIM, HEAD_DIM), F32)],
        compiler_params=_cparams(("parallel", "arbitrary")),
        name="rwkv_scan",
    )(r, lw, k, v, kk, b, tr(kk), tr(lw), tr(v), s0)


def _l0_out_kernel(y_ref, bonus_ref, g_ref, oa_ref, h_ref, gnw_ref, gnb_ref, gm_ref, wa_ref, wb_ref,
                   gpost_ref, o_ref):
    y = y_ref[...]
    gmat = gm_ref[...]
    inv = 1.0 / HEAD_DIM
    yc = y - _mm(y, gmat, HI) * inv
    yn = yc * lax.rsqrt(_mm(yc * yc, gmat, HI) * inv + GN_EPS) * gnw_ref[...] + gnb_ref[...]
    ob = (yn + bonus_ref[...]) * g_ref[...]
    mix = _mm(oa_ref[...].astype(BF16), wa_ref[...]) + _mm(ob.astype(BF16), wb_ref[...])
    o_ref[...] = h_ref[...] + _rms(mix, gpost_ref[...])


def _l0_out(y, bonus, g, oa, h, prm, gmat, w_out, tm):
    m, dm = h.shape
    d = D_HEADS

    def rowblk(w):
        return pl.BlockSpec((tm, w), lambda i: (i, 0))

    def full(shape):
        return pl.BlockSpec(shape, lambda i: (0,) * len(shape))

    return pl.pallas_call(
        _l0_out_kernel,
        grid=(m // tm,),
        in_specs=[rowblk(d), rowblk(d), rowblk(d), rowblk(d), rowblk(dm), full((1, d)), full((1, d)),
                  full((d, d)), full((d, dm)), full((d, dm)), full((1, dm))],
        out_specs=rowblk(dm),
        out_shape=jax.ShapeDtypeStruct((m, dm), F32),
        compiler_params=_cparams(("parallel",)),
        name="l0_out",
    )(y, bonus, g, oa, h, prm['rw_gn_w'].reshape(1, d), prm['rw_gn_b'].reshape(1, d), gmat,
      w_out[:d], w_out[d:], prm['l0_norm_post'].reshape(1, dm))


FFN_COLS = 256


def _ffn_down_kernel(up_ref, pa_ref, pb_ref, cw_ref, cb_ref, wd_ref, h_ref, gpost_ref, o_ref, *, seq_len):
    i = pl.program_id(0)
    tm = up_ref.shape[0]
    dff = wd_ref.shape[0]
    row = lax.broadcasted_iota(jnp.int32, (tm, 1), 0)
    pos = row if seq_len is None else row % seq_len

    def conv(cols):
        u = up_ref[:, cols]
        if seq_len is None:
            p0 = jnp.where(i == 0, pb_ref[6:7, cols], pa_ref[6:7, cols])
            p1 = jnp.where(i == 0, pb_ref[7:8, cols], pa_ref[7:8, cols])
        else:
            p0, p1 = pa_ref[:, cols], pb_ref[:, cols]
        m1 = jnp.where(pos == 0, p1, pltpu.roll(u, 1, axis=0))
        m2 = jnp.where(pos == 0, p0, jnp.where(pos == 1, p1, pltpu.roll(u, 2, axis=0)))
        return cb_ref[:, cols] + cw_ref[0:1, cols] * m2 + cw_ref[1:2, cols] * m1 + cw_ref[2:3, cols] * u

    acc = jnp.zeros(o_ref.shape, F32)
    for c in range(dff // FFN_COLS):
        gate = conv(slice(c * FFN_COLS, (c + 1) * FFN_COLS))
        val = conv(slice(dff + c * FFN_COLS, dff + (c + 1) * FFN_COLS))
        act = (jax.nn.gelu(gate) * val).astype(BF16)
        acc = acc + _mm(act, wd_ref[c * FFN_COLS:(c + 1) * FFN_COLS, :])
    o_ref[...] = h_ref[...] + _rms(acc, gpost_ref[...])


def _ffn_down(up, prev_rows, conv_w, conv_b, w_down, h, g_post, tm, seq_len):
    m, n2 = up.shape
    dff, dm = w_down.shape
    assert dff % FFN_COLS == 0
    if seq_len is None:
        pa, pb = up, jnp.pad(prev_rows[0], ((6, 0), (0, 0)))
        pa_spec = pl.BlockSpec((8, n2), lambda i: (jnp.maximum(i * (tm // 8) - 1, 0), 0))
        pb_spec = pl.BlockSpec((8, n2), lambda i: (0, 0))
    else:
        pa = jnp.repeat(prev_rows[:, 0], seq_len, axis=0)
        pb = jnp.repeat(prev_rows[:, 1], seq_len, axis=0)
        pa_spec = pb_spec = pl.BlockSpec((tm, n2), lambda i: (i, 0))

    def full(shape):
        return pl.BlockSpec(shape, lambda i: (0,) * len(shape))

    return pl.pallas_call(
        functools.partial(_ffn_down_kernel, seq_len=seq_len),
        grid=(m // tm,),
        in_specs=[pl.BlockSpec((tm, n2), lambda i: (i, 0)), pa_spec, pb_spec, full((CONV_W, n2)),
                  full((1, n2)), full((dff, dm)), pl.BlockSpec((tm, dm), lambda i: (i, 0)), full((1, dm))],
        out_specs=pl.BlockSpec((tm, dm), lambda i: (i, 0)),
        out_shape=jax.ShapeDtypeStruct((m, dm), F32),
        compiler_params=_cparams(("parallel",)),
        name="ffn_down",
    )(up, pa, pb, conv_w, conv_b.reshape(1, n2), w_down, h, g_post.reshape(1, dm))


def _gmlp_kernel(h_ref, gpre_ref, win_ref, lnw_ref, lnb_ref, wmix_ref, bias_ref, wout_ref, gpost_ref,
                 o_ref, cv_ref, gated_scr):
    h = h_ref[...]
    tm = h.shape[0]
    dc = wout_ref.shape[0]
    rm = wmix_ref.shape[1]
    gw = dc // GMLP_GROUPS
    z = jax.nn.gelu(_mm(_rms(h, gpre_ref[...]).astype(BF16), win_ref[...]))
    v = z[:, dc:]
    vc = v - jnp.mean(v, axis=-1, keepdims=True)
    vn = vc * lax.rsqrt(jnp.mean(vc * vc, axis=-1, keepdims=True) + LN_EPS) * lnw_ref[...] + lnb_ref[...]
    cv_ref[...] = vn
    for s in range(tm // rm):
        rows = slice(s * rm, (s + 1) * rm)
        for g in range(GMLP_GROUPS):
            cols = slice(g * gw, (g + 1) * gw)
            mixed = _mm(wmix_ref[g], vn[rows, cols].astype(BF16)) + bias_ref[:, cols]
            gated_scr[rows, cols] = (z[rows, cols] * mixed).astype(BF16)
    o_ref[...] = h + _rms(_mm(gated_scr[...], wout_ref[...]), gpost_ref[...])


def _gmlp(h, g_pre, w_in, ln_w, ln_b, wmix, bias, w_out, g_post, tm):
    m, dm = h.shape
    dc = w_out.shape[0]
    rm = wmix.shape[1]

    def full(shape):
        return pl.BlockSpec(shape, lambda i: (0,) * len(shape))

    return pl.pallas_call(
        _gmlp_kernel,
        grid=(m // tm,),
        in_specs=[pl.BlockSpec((tm, dm), lambda i: (i, 0)), full((1, dm)), full((dm, 2 * dc)), full((1, dc)),
                  full((1, dc)), full((GMLP_GROUPS, rm, rm)), full((rm, dc)), full((dc, dm)), full((1, dm))],
        out_specs=[pl.BlockSpec((tm, dm), lambda i: (i, 0)), pl.BlockSpec((tm, dc), lambda i: (i, 0))],
        out_shape=[jax.ShapeDtypeStruct((m, dm), F32), jax.ShapeDtypeStruct((m, dc), F32)],
        scratch_shapes=[pltpu.VMEM((tm, dc), BF16)],
        compiler_params=_cparams(("parallel",)),
        name="gmlp",
    )(h, g_pre.reshape(1, dm), w_in, ln_w.reshape(1, dc), ln_b.reshape(1, dc), wmix, bias, w_out,
      g_post.reshape(1, dm))


def _trunk(x, attend, prev_xn, wkv0, conv0, prm, w, *, tile, long_seq):
    nseq, t, dm = x.shape
    m = nseq * t
    h = x.reshape(m, dm)
    seq_len = None if long_seq else t

    g0 = prm['l0_norm_pre']
    qkv = _norm_matmul(h, g0, w['w_qkv'], tile, 512)
    pb = _norm_matmul(h, g0, w['w_b'], tile, 256)
    shift_new, prev_proj = _shift_rows(x[:, -1], g0, prev_xn, w['w_b'])
    q, k_new, v_new = qkv[:, :D_HEADS], qkv[:, D_HEADS:2 * D_HEADS], qkv[:, 2 * D_HEADS:]
    if long_seq:
        first = jnp.concatenate([prev_proj, pb[tile - 1:m - 1:tile]], axis=0).reshape(m // tile, 1, -1)
    else:
        first = jnp.repeat(prev_proj, t, axis=0)
    r, lw, kmod, vv, kk, bb, gate, bonus = _rwkv_prep(pb, first, prm, w['gmat'], tile, seq_len)
    o_a = attend(q, k_new, v_new)

    t_pad = -(-t // WKV_CHUNK) * WKV_CHUNK

    def padt(z):
        if t_pad == t:
            return z
        return jnp.pad(z.reshape(nseq, t, -1), ((0, 0), (0, t_pad - t), (0, 0))).reshape(nseq * t_pad, -1)

    y, wkv_new = _rwkv_scan(padt(r), padt(lw), padt(kmod), padt(vv), padt(kk), padt(bb), wkv0, t_pad,
                            2 if t_pad % (2 * WKV_CHUNK) == 0 else 1)
    if t_pad != t:
        y = y.reshape(nseq, t_pad, -1)[:, :t].reshape(m, -1)
    h = _l0_out(y, bonus, gate, o_a, h, prm, w['gmat'], w['w_out0'], tile)

    conv_rows = []

    def ffn(h, layer):
        up = _norm_matmul(h, prm['ffn_norm_pre'][layer], w['w_up'][layer], tile, 512)
        conv_rows.append(up.reshape(nseq, t, -1)[:, t - (CONV_W - 1):])
        return _ffn_down(up, conv0[layer], prm['ffn_conv_w'][layer], prm['ffn_conv_b'][layer],
                         w['w_down'][layer], h, prm['ffn_norm_post'][layer], min(tile, 256), seq_len)

    h = ffn(h, 0)

    if long_seq:
        wmix, bias = w['wmix'], w['gbias']
    else:
        eye = jnp.eye(tile // t, dtype=F32)
        wmix = jnp.einsum('ab,gij->gaibj', eye, w['wmix'][:, :t, :t].astype(F32))
        wmix = wmix.reshape(GMLP_GROUPS, tile, tile).astype(BF16)
        bias = jnp.tile(w['gbias'][:t], (tile // t, 1))
    h, chunk_v = _gmlp(h, prm['l1_norm_pre'], w['w_in1'], prm['gm_ln_w'], prm['gm_ln_b'], wmix, bias,
                       w['w_out1'], prm['l1_norm_post'], tile)
    h = ffn(h, 1)
    return (h.reshape(nseq, t, dm), k_new.reshape(nseq, t, N_HEADS, HEAD_DIM),
            v_new.reshape(nseq, t, N_HEADS, HEAD_DIM), wkv_new, shift_new, chunk_v.reshape(nseq, t, -1),
            jnp.stack(conv_rows))


def kernel(x_prompt, x_sample, cache_k, cache_v, state_wkv, state_shift, state_ffn_conv, page_table,
           l0_norm_pre, l0_norm_post, l0_w_in, rw_mu, rw_w0, rw_w2, rw_a0, rw_a2, rw_g2, rw_k_k, rw_k_a,
           rw_r_k, rw_gn_w, rw_gn_b, l0_w_out, l1_norm_pre, l1_norm_post, l1_w_in, gm_ln_w, gm_ln_b,
           gm_ws, gm_bs, l1_w_out, ffn_norm_pre, ffn_norm_post, ffn_w_up, ffn_conv_w, ffn_conv_b,
           ffn_w_down):
    prm = dict(l0_norm_pre=l0_norm_pre, l0_norm_post=l0_norm_post, rw_mu=rw_mu, rw_w0=rw_w0, rw_w2=rw_w2,
               rw_a0=rw_a0, rw_a2=rw_a2, rw_g2=rw_g2, rw_k_k=rw_k_k, rw_k_a=rw_k_a, rw_r_k=rw_r_k,
               rw_gn_w=rw_gn_w, rw_gn_b=rw_gn_b, l1_norm_pre=l1_norm_pre, l1_norm_post=l1_norm_post,
               gm_ln_w=gm_ln_w, gm_ln_b=gm_ln_b, ffn_norm_pre=ffn_norm_pre, ffn_norm_post=ffn_norm_post,
               ffn_conv_w=ffn_conv_w, ffn_conv_b=ffn_conv_b)
    d_a = D_HEADS
    dc = l1_w_out.shape[0]
    tri = jnp.tril(jnp.ones((GMLP_CHUNK, GMLP_CHUNK), dtype=bool))
    hd = jnp.arange(D_HEADS) // HEAD_DIM
    w = dict(
        w_qkv=l0_w_in[:, :3 * d_a].astype(BF16),
        w_b=l0_w_in[:, 3 * d_a:].astype(BF16),
        w_out0=l0_w_out.astype(BF16),
        w_up=ffn_w_up.astype(BF16),
        w_down=ffn_w_down.astype(BF16),
        w_in1=l1_w_in.astype(BF16),
        w_out1=l1_w_out.astype(BF16),
        wmix=jnp.where(tri[None], gm_ws, 0.0).astype(BF16),
        gbias=jnp.repeat(gm_bs.T, dc // GMLP_GROUPS, axis=1),
        gmat=(hd[:, None] == hd[None, :]).astype(F32),
    )
    slopes = jnp.exp2(-8.0 * jnp.arange(1, N_HEADS + 1, dtype=F32) / N_HEADS)

    bp, tp, dm = x_prompt.shape
    assert bp == 1
    dt = x_prompt.dtype
    depth = ffn_w_up.shape[0]
    wkv0 = jnp.zeros((bp, N_HEADS, HEAD_DIM, HEAD_DIM), dt)
    shift0 = jnp.zeros((bp, dm), dt)
    conv0 = jnp.zeros((depth, bp, CONV_W - 1, ffn_w_up.shape[2]), dt)

    def attend_prompt(q, k, v):
        return _moba_prompt(q, k, v, slopes)

    (y_prompt, k_prompt, v_prompt, wkv_prompt, shift_prompt, _unused_chunk_v,
     ffn_conv_prompt) = _trunk(x_prompt, attend_prompt, shift0, wkv0, conv0, prm, w, tile=512, long_seq=True)

    bs, ts, _ = x_sample.shape

    def attend_sample(q, k, v):
        o = _moba_sample(q.reshape(bs, ts, -1), k.reshape(bs, ts, -1), v.reshape(bs, ts, -1), cache_k, cache_v,
                         page_table, slopes)
        return o.reshape(bs * ts, -1)

    (y_sample, k_sample, v_sample, wkv_sample, shift_sample, chunk_v_sample,
     ffn_conv_sample) = _trunk(x_sample, attend_sample, state_shift, state_wkv, state_ffn_conv, prm, w,
                               tile=bs * ts, long_seq=False)

    return (y_prompt, y_sample, k_prompt, v_prompt, k_sample, v_sample, wkv_prompt, wkv_sample,
            shift_prompt, shift_sample, chunk_v_sample, ffn_conv_prompt, ffn_conv_sample)
```

```python
import functools

import jax
import jax.numpy as jnp
from jax import lax
from jax.experimental import pallas as pl
from jax.experimental.pallas import tpu as pltpu

F32 = jnp.float32
BF16 = jnp.bfloat16
HI = lax.Precision.HIGHEST

HEAD_DIM = 64
N_HEADS = 8
D_HEADS = N_HEADS * HEAD_DIM
MOBA_BLOCK = 256
MOBA_TOPK = 3
SCALE = HEAD_DIM ** -0.5
D_DECAY_LORA = 64
D_AAA_LORA = 64
GMLP_GROUPS = 8
GMLP_CHUNK = 128
CONV_W = 3
NORM_EPS = 1e-6
LN_EPS = 1e-5
GN_EPS = 64e-5
NEG = -1e30
WKV_CHUNK = 64
V7X_VMEM_LIMIT = 56 * 1024 * 1024


def _cparams(sem, vmem=V7X_VMEM_LIMIT):
    return pltpu.CompilerParams(dimension_semantics=sem, vmem_limit_bytes=vmem)


def _mm(a, b, prec=None):
    return jnp.dot(a, b, precision=prec, preferred_element_type=F32)


def _mm_nt(a, b, prec=None):
    return lax.dot_general(a, b, (((1,), (1,)), ((), ())), precision=prec, preferred_element_type=F32)


def _rms(x, g):
    return x * lax.rsqrt(jnp.mean(x * x, axis=-1, keepdims=True) + NORM_EPS) * g


def _norm_matmul_kernel(x_ref, g_ref, w_ref, o_ref, xn_ref):
    @pl.when(pl.program_id(1) == 0)
    def _():
        xn_ref[...] = _rms(x_ref[...], g_ref[...]).astype(BF16)

    o_ref[...] = _mm(xn_ref[...], w_ref[...])


def _norm_matmul(x, g, w, tm, tn):
    m, kd = x.shape
    n = w.shape[1]
    return pl.pallas_call(
        _norm_matmul_kernel,
        grid=(m // tm, n // tn),
        in_specs=[pl.BlockSpec((tm, kd), lambda i, j: (i, 0)),
                  pl.BlockSpec((1, kd), lambda i, j: (0, 0)),
                  pl.BlockSpec((kd, tn), lambda i, j: (0, j))],
        out_specs=pl.BlockSpec((tm, tn), lambda i, j: (i, j)),
        out_shape=jax.ShapeDtypeStruct((m, n), F32),
        scratch_shapes=[pltpu.VMEM((tm, kd), BF16)],
        compiler_params=_cparams(("parallel", "arbitrary")),
        name="norm_matmul",
    )(x, g.reshape(1, kd), w)


def _shift_rows_kernel(x_ref, g_ref, prev_ref, w_ref, xn_ref, pp_ref):
    xn_ref[...] = _rms(x_ref[...], g_ref[...])
    pp_ref[...] = _mm(prev_ref[...].astype(BF16), w_ref[...])


def _shift_rows(x_last, g, prev_xn, w_b):
    b, kd = x_last.shape
    n = w_b.shape[1]
    bp = -(-b // 8) * 8
    pad = ((0, bp - b), (0, 0))
    xn, pp = pl.pallas_call(
        _shift_rows_kernel,
        out_shape=(jax.ShapeDtypeStruct((bp, kd), F32), jax.ShapeDtypeStruct((bp, n), F32)),
        compiler_params=_cparams(None),
        name="shift_rows",
    )(jnp.pad(x_last, pad), g.reshape(1, kd), jnp.pad(prev_xn, pad), w_b)
    return xn[:b], pp[:b]


def _kmeans_kernel(k_ref, o_ref):
    k = k_ref[0]
    nb = k.shape[0] // MOBA_BLOCK
    o_ref[0] = jnp.mean(k.reshape(nb, MOBA_BLOCK, HEAD_DIM), axis=1)


def _moba_prompt_kernel(slope_ref, qT_ref, km_ref, k_ref, vT_ref, oT_ref, sel_scr):
    i = pl.program_id(1)
    nb = km_ref.shape[1]
    blk = MOBA_BLOCK
    qf = qT_ref[0]
    gate = _mm(km_ref[0], qf, HI)
    nidx = lax.broadcasted_iota(jnp.int32, gate.shape, 0)
    g = jnp.where(nidx < i, gate, -jnp.inf)
    sel = jnp.zeros(gate.shape, F32)
    for _ in range(MOBA_TOPK):
        top = jnp.max(g, axis=0, keepdims=True)
        idx = jnp.min(jnp.where(g == top, nidx, nb), axis=0, keepdims=True)
        pick = (nidx == idx) & (top > -jnp.inf)
        sel = jnp.where(pick, 1.0, sel)
        g = jnp.where(pick, -jnp.inf, g)
    sel_scr[...] = sel

    qs = (qf * SCALE).astype(BF16)
    slope = slope_ref[0]
    ki = lax.broadcasted_iota(jnp.int32, (blk, blk), 0)
    qi = lax.broadcasted_iota(jnp.int32, (blk, blk), 1)
    dm = (ki - qi).astype(F32)

    def scores(n):
        start = pl.multiple_of(n * blk, blk)
        kb = k_ref[0, pl.ds(start, blk), :]
        return _mm(kb, qs) + slope * (dm + ((n - i) * blk).astype(F32))

    def pv(n, p):
        start = pl.multiple_of(n * blk, blk)
        return _mm(vT_ref[0, :, pl.ds(start, blk)], p.astype(BF16))

    s = jnp.where(dm <= 0.0, scores(i), NEG)
    m0 = jnp.max(s, axis=0, keepdims=True)
    p = jnp.exp(s - m0)
    l0 = jnp.sum(p, axis=0, keepdims=True)
    acc0 = pv(i, p)

    def body(n, carry):
        m, l, acc = carry
        s = jnp.where(sel_scr[pl.ds(n, 1), :] > 0.0, scores(n), NEG)
        m_new = jnp.maximum(m, jnp.max(s, axis=0, keepdims=True))
        alpha = jnp.exp(m - m_new)
        p = jnp.exp(s - m_new)
        return m_new, alpha * l + jnp.sum(p, axis=0, keepdims=True), alpha * acc + pv(n, p)

    _, l, acc = lax.fori_loop(0, i, body, (m0, l0, acc0))
    oT_ref[0] = acc / l


def _moba_prompt(q, k, v, slopes):
    t = q.shape[0]
    assert t % MOBA_BLOCK == 0
    nb = t // MOBA_BLOCK

    def heads_t(z):
        return z.reshape(t, N_HEADS, HEAD_DIM).transpose(1, 2, 0)

    k_hm = k.reshape(t, N_HEADS, HEAD_DIM).transpose(1, 0, 2)
    kmean = pl.pallas_call(
        _kmeans_kernel,
        grid=(N_HEADS,),
        in_specs=[pl.BlockSpec((1, t, HEAD_DIM), lambda h: (h, 0, 0))],
        out_specs=pl.BlockSpec((1, nb, HEAD_DIM), lambda h: (h, 0, 0)),
        out_shape=jax.ShapeDtypeStruct((N_HEADS, nb, HEAD_DIM), F32),
        compiler_params=_cparams(("parallel",)),
        name="moba_kmeans",
    )(k_hm)
    oT = pl.pallas_call(
        _moba_prompt_kernel,
        grid=(N_HEADS, nb),
        in_specs=[pl.BlockSpec((1, 1, 1), lambda h, i: (h, 0, 0)),
                  pl.BlockSpec((1, HEAD_DIM, MOBA_BLOCK), lambda h, i: (h, 0, i)),
                  pl.BlockSpec((1, nb, HEAD_DIM), lambda h, i: (h, 0, 0)),
                  pl.BlockSpec((1, t, HEAD_DIM), lambda h, i: (h, 0, 0)),
                  pl.BlockSpec((1, HEAD_DIM, t), lambda h, i: (h, 0, 0))],
        out_specs=pl.BlockSpec((1, HEAD_DIM, MOBA_BLOCK), lambda h, i: (h, 0, i)),
        out_shape=jax.ShapeDtypeStruct((N_HEADS, HEAD_DIM, t), F32),
        scratch_shapes=[pltpu.VMEM((nb, MOBA_BLOCK), F32)],
        compiler_params=_cparams(("parallel", "arbitrary")),
        name="moba_prompt",
    )(slopes.reshape(N_HEADS, 1, 1), heads_t(q), kmean, k_hm.astype(BF16), heads_t(v).astype(BF16))
    return oT.transpose(2, 0, 1).reshape(t, D_HEADS)


def _moba_gate_kernel(pt_ref, q_ref, ka_ref, kb_ref, sel_ref, km_scr):
    del pt_ref
    n = pl.program_id(1)
    nb = km_scr.shape[0]
    km_scr[n] = (jnp.sum(ka_ref[0], axis=0) + jnp.sum(kb_ref[0], axis=0)) * (1.0 / MOBA_BLOCK)

    @pl.when(n == nb - 1)
    def _():
        km = km_scr[...].reshape(nb * N_HEADS, HEAD_DIM)
        nq = q_ref.shape[1]
        qall = q_ref[0].reshape(nq * N_HEADS, HEAD_DIM)
        g3 = _mm_nt(km, qall, HI).reshape(nb, N_HEADS, nq * N_HEADS)
        hrow = lax.broadcasted_iota(jnp.int32, g3.shape, 1)
        hcol = lax.broadcasted_iota(jnp.int32, g3.shape, 2) % N_HEADS
        g = jnp.sum(jnp.where(hrow == hcol, g3, 0.0), axis=1)
        bidx = lax.broadcasted_iota(jnp.int32, g.shape, 0)
        sel = jnp.zeros(g.shape, F32)
        for _ in range(MOBA_TOPK):
            top = jnp.max(g, axis=0, keepdims=True)
            idx = jnp.min(jnp.where(g == top, bidx, nb), axis=0, keepdims=True)
            pick = (bidx == idx) & (top > -jnp.inf)
            sel = jnp.where(pick, 1.0, sel)
            g = jnp.where(pick, -jnp.inf, g)
        sel_ref[0] = sel


def _moba_attend_kernel(pg_ref, slope_ref, q_ref, kn_ref, vn_ref, dist_ref, ck_ref, cv_ref, o_ref,
                        kbuf, vbuf, sem, *, n_visits):
    i = pl.program_id(0)
    nsteps = pl.num_programs(0)
    page = ck_ref.shape[1]
    n_copies = n_visits * (MOBA_BLOCK // page)

    def copies(step, slot):
        h = step % N_HEADS
        out = []
        for c in range(n_copies):
            pg = pg_ref[step * n_copies + c]
            rows = pl.ds(c * page, page)
            out.append(pltpu.make_async_copy(ck_ref.at[pg, :, h, :], kbuf.at[slot, rows, :], sem.at[0, slot]))
            out.append(pltpu.make_async_copy(cv_ref.at[pg, :, h, :], vbuf.at[slot, rows, :], sem.at[1, slot]))
        return out

    @pl.when(i == 0)
    def _():
        for cp in copies(0, 0):
            cp.start()

    @pl.when(i + 1 < nsteps)
    def _():
        for cp in copies(i + 1, (i + 1) % 2):
            cp.start()

    slot = i % 2
    for cp in copies(i, slot):
        cp.wait()

    nq = q_ref.shape[2]
    slope = slope_ref[0]
    qs = (q_ref[0, 0] * SCALE).astype(BF16)
    ncol = n_visits * MOBA_BLOCK
    s = _mm_nt(qs, kbuf[slot].astype(BF16)) - slope * dist_ref[0, 0]
    row = lax.broadcasted_iota(jnp.int32, (nq, ncol), 0)
    col = lax.broadcasted_iota(jnp.int32, (nq, ncol), 1)
    s = jnp.where(col // (ncol // nq) == row, s, NEG)
    nk = kn_ref.shape[2]
    dist_o = (lax.broadcasted_iota(jnp.int32, (nq, nk), 0)
              - lax.broadcasted_iota(jnp.int32, (nq, nk), 1)).astype(F32)
    s_o = jnp.where(dist_o >= 0.0, _mm_nt(qs, kn_ref[0, 0].astype(BF16)) - slope * dist_o, NEG)
    m = jnp.maximum(jnp.max(s, axis=1, keepdims=True), jnp.max(s_o, axis=1, keepdims=True))
    p = jnp.exp(s - m)
    p_o = jnp.exp(s_o - m)
    l = jnp.sum(p, axis=1, keepdims=True) + jnp.sum(p_o, axis=1, keepdims=True)
    acc = _mm(p.astype(BF16), vbuf[slot].astype(BF16)) + _mm(p_o.astype(BF16), vn_ref[0, 0].astype(BF16))
    o_ref[0, 0] = acc / l


def _moba_sample(q, k, v, cache_k, cache_v, page_table, slopes):
    b, s, _ = q.shape
    n_pool, page, _, _ = cache_k.shape
    n_pages = page_table.shape[1]
    past = n_pages * page
    pages_per_blk = MOBA_BLOCK // page
    assert pages_per_blk == 2 and past % MOBA_BLOCK == 0
    nb = past // MOBA_BLOCK
    assert nb >= MOBA_TOPK
    pt = page_table.reshape(-1).astype(jnp.int32)

    def page_spec(j):
        return pl.BlockSpec((1, page, N_HEADS, HEAD_DIM), lambda i, n, pt: (pt[i * n_pages + 2 * n + j], 0, 0, 0))

    sel = pl.pallas_call(
        _moba_gate_kernel,
        grid_spec=pltpu.PrefetchScalarGridSpec(
            num_scalar_prefetch=1,
            grid=(b, nb),
            in_specs=[pl.BlockSpec((1, s, N_HEADS, HEAD_DIM), lambda i, n, pt: (i, 0, 0, 0)),
                      page_spec(0), page_spec(1)],
            out_specs=pl.BlockSpec((1, nb, s * N_HEADS), lambda i, n, pt: (i, 0, 0)),
            scratch_shapes=[pltpu.VMEM((nb, N_HEADS, HEAD_DIM), F32)]),
        out_shape=jax.ShapeDtypeStruct((b, nb, s * N_HEADS), F32),
        compiler_params=_cparams(("parallel", "arbitrary")),
        name="moba_gate",
    )(pt, q.reshape(b, s, N_HEADS, HEAD_DIM), cache_k, cache_k)

    n_visits = s * MOBA_TOPK
    blk = jnp.argsort(-sel, axis=1, stable=True)[:, :MOBA_TOPK, :]
    blk = blk.reshape(b, MOBA_TOPK, s, N_HEADS).transpose(0, 3, 2, 1).reshape(b, N_HEADS, n_visits)
    pages = page_table[jnp.arange(b)[:, None, None, None],
                       blk[..., None] * pages_per_blk + jnp.arange(pages_per_blk)]
    kpos = blk[..., None] * MOBA_BLOCK + jnp.arange(MOBA_BLOCK)
    qpos = past + jnp.arange(n_visits) // MOBA_TOPK
    dist = (qpos[None, None, :, None] - kpos).astype(F32).reshape(b, N_HEADS, 1, n_visits * MOBA_BLOCK)

    nk = 16

    def heads(z, rows):
        z = jnp.pad(z, ((0, 0), (0, rows - s), (0, 0)))
        return z.reshape(b, rows, N_HEADS, HEAD_DIM).transpose(0, 2, 1, 3)

    def bh_spec(r, w):
        return pl.BlockSpec((1, 1, r, w), lambda i, pg: (i // N_HEADS, i % N_HEADS, 0, 0))

    out = pl.pallas_call(
        functools.partial(_moba_attend_kernel, n_visits=n_visits),
        grid_spec=pltpu.PrefetchScalarGridSpec(
            num_scalar_prefetch=1,
            grid=(b * N_HEADS,),
            in_specs=[pl.BlockSpec((1, 1, 1), lambda i, pg: (i % N_HEADS, 0, 0)),
                      bh_spec(s, HEAD_DIM), bh_spec(nk, HEAD_DIM), bh_spec(nk, HEAD_DIM),
                      bh_spec(1, n_visits * MOBA_BLOCK),
                      pl.BlockSpec(memory_space=pl.ANY), pl.BlockSpec(memory_space=pl.ANY)],
            out_specs=bh_spec(s, HEAD_DIM),
            scratch_shapes=[pltpu.VMEM((2, n_visits * MOBA_BLOCK, HEAD_DIM), F32),
                            pltpu.VMEM((2, n_visits * MOBA_BLOCK, HEAD_DIM), F32),
                            pltpu.SemaphoreType.DMA((2, 2))]),
        out_shape=jax.ShapeDtypeStruct((b, N_HEADS, s, HEAD_DIM), F32),
        compiler_params=_cparams(("arbitrary",)),
        name="moba_attend",
    )(pages.reshape(-1).astype(jnp.int32), slopes.reshape(N_HEADS, 1, 1), heads(q, s), heads(k, nk), heads(v, nk),
      dist, cache_k, cache_v)
    return out.transpose(0, 2, 1, 3).reshape(b, s, D_HEADS)


def _softplus(z):
    return jnp.maximum(z, 0.0) + jnp.log1p(jnp.exp(-jnp.abs(z)))


def _rwkv_prep_kernel(pb_ref, first_ref, mu_ref, w0_ref, w2_ref, a0_ref, a2_ref, g2_ref, kk_ref, ka_ref,
                      rk_ref, gm_ref, r_o, lw_o, k_o, v_o, kk_o, b_o, g_o, bonus_o, *, seq_len):
    pb = pb_ref[...]
    tm = pb.shape[0]
    row = lax.broadcasted_iota(jnp.int32, (tm, 1), 0)
    if seq_len is None:
        starts, first = row == 0, first_ref[0]
    else:
        starts, first = (row % seq_len) == 0, first_ref[...]
    prev = jnp.where(starts, first, pltpu.roll(pb, 1, axis=0))
    xm = pb + mu_ref[...] * (prev - pb)
    d = D_HEADS
    o1, o2, o3 = 3 * d, 3 * d + D_DECAY_LORA, 3 * d + D_DECAY_LORA + D_AAA_LORA
    r, k, v = xm[:, 0:d], xm[:, d:2 * d], xm[:, 2 * d:o1]
    xw, xa, xg = xm[:, o1:o2], xm[:, o2:o3], xm[:, o3:]
    logw = -_softplus(-(w0_ref[...] + _mm(jnp.tanh(xw), w2_ref[...]))) - 0.5
    a = jax.nn.sigmoid(a0_ref[...] + _mm(xa, a2_ref[...]))
    gmat = gm_ref[...]
    kk = k * kk_ref[...]
    kk = kk / jnp.maximum(jnp.sqrt(_mm(kk * kk, gmat, HI)), 1e-12)
    k = k * (1.0 + (a - 1.0) * ka_ref[...])
    r_o[...] = r
    lw_o[...] = -jnp.exp(logw)
    k_o[...] = k
    v_o[...] = v
    kk_o[...] = kk
    b_o[...] = kk * a
    g_o[...] = _mm(jax.nn.sigmoid(xg), g2_ref[...])
    bonus_o[...] = _mm(r * k * rk_ref[...], gmat, HI) * v


def _rwkv_prep(pb, first, prm, gmat, tm, seq_len):
    m, n = pb.shape
    d = D_HEADS
    if seq_len is None:
        first_spec = pl.BlockSpec((1, 1, n), lambda i: (i, 0, 0))
    else:
        first_spec = pl.BlockSpec((tm, n), lambda i: (i, 0))

    def full(shape):
        return pl.BlockSpec(shape, lambda i: (0,) * len(shape))

    def vec(z):
        return z.reshape(1, -1)

    outs = pl.pallas_call(
        functools.partial(_rwkv_prep_kernel, seq_len=seq_len),
        grid=(m // tm,),
        in_specs=[pl.BlockSpec((tm, n), lambda i: (i, 0)), first_spec,
                  full((1, n)), full((1, d)), full((D_DECAY_LORA, d)), full((1, d)), full((D_AAA_LORA, d)),
                  full((n - 3 * d - D_DECAY_LORA - D_AAA_LORA, d)), full((1, d)), full((1, d)), full((1, d)),
                  full((d, d))],
        out_specs=[pl.BlockSpec((tm, d), lambda i: (i, 0))] * 8,
        out_shape=[jax.ShapeDtypeStruct((m, d), F32)] * 8,
        compiler_params=_cparams(("parallel",)),
        name="rwkv_prep",
    )(pb, first, vec(prm['rw_mu']), vec(prm['rw_w0']), prm['rw_w2'], vec(prm['rw_a0']), prm['rw_a2'],
      prm['rw_g2'], vec(prm['rw_k_k']), vec(prm['rw_k_a']), vec(prm['rw_r_k']), gmat)
    return outs


def _split3(x):
    p1 = x.astype(BF16)
    r1 = x - p1.astype(F32)
    p2 = r1.astype(BF16)
    p3 = (r1 - p2.astype(F32)).astype(BF16)
    return p1, p2, p3


def _mmb(a, b):
    return _mm(a.astype(BF16), b.astype(BF16))


def _mmb_nt(a, b):
    return _mm_nt(a.astype(BF16), b.astype(BF16))


def _rwkv_scan_kernel(r_ref, lw_ref, k_ref, v_ref, kk_ref, b_ref, kkT_ref, lwT_ref, vT_ref, s0_ref,
                      y_ref, sfin_ref, s_scr, *, chunk, chunks_per_step):
    j = pl.program_id(1)

    @pl.when(j == 0)
    def _():
        s_scr[...] = s0_ref[0]

    step = chunk * chunks_per_step
    d = D_HEADS
    si = lax.broadcasted_iota(jnp.int32, (step, step), 0)
    sj = lax.broadcasted_iota(jnp.int32, (step, step), 1)
    same = (si // chunk) == (sj // chunk)
    lbd = jnp.where(same & (si >= sj), 1.0, 0.0).astype(BF16)
    ubd = jnp.where(same & (si <= sj), 1.0, 0.0).astype(BF16)
    lw = lw_ref[...]
    lwT = lwT_ref[0]
    cl = sum(_mm(lbd, p) for p in _split3(lw))
    clT = sum(_mm(p, ubd) for p in _split3(lwT))
    tots = [cl[(c + 1) * chunk - 1:(c + 1) * chunk, :] for c in range(chunks_per_step)]
    tot_full = jnp.concatenate([jnp.broadcast_to(t, (chunk, d)) for t in tots], axis=0)
    e_neg = jnp.exp(-cl)
    e_end = jnp.exp(tot_full - cl)
    b = b_ref[...]
    k = k_ref[...]
    rho = r_ref[...] * jnp.exp(cl)
    beta, kap = b * e_neg, k * e_neg
    b_end, k_end = (b * e_end).astype(BF16), (k * e_end).astype(BF16)
    alphaT = kkT_ref[0] * jnp.exp(clT - lwT)
    vT = vT_ref[0].astype(BF16)
    v = v_ref[...].astype(BF16)

    ii = lax.broadcasted_iota(jnp.int32, (chunk, chunk), 0)
    jj = lax.broadcasted_iota(jnp.int32, (chunk, chunk), 1)
    low, upper = ii >= jj, ii < jj
    hd = HEAD_DIM
    units = [(slice(c * chunk, (c + 1) * chunk), slice(h * hd, (h + 1) * hd))
             for c in range(chunks_per_step) for h in range(N_HEADS)]
    bk = [jnp.concatenate([beta[rs, cs], kap[rs, cs]], axis=0).astype(BF16) for rs, cs in units]
    aT = [alphaT[cs, rs] for rs, cs in units]
    g1 = [_mm(x, a.astype(BF16)) for x, a in zip(bk, aT)]
    lt = [jnp.where(upper, g[:chunk], 0.0).astype(BF16) for g in g1]
    akT = [jnp.where(upper, g[chunk:], 0.0).astype(BF16) for g in g1]
    g2 = [_mm_nt(rho[rs, cs].astype(BF16), x) for (rs, cs), x in zip(units, bk)]
    arb = [jnp.where(low, g[:, :chunk], 0.0).astype(BF16) for g in g2]
    ark = [jnp.where(low, g[:, chunk:], 0.0).astype(BF16) for g in g2]
    z = [jnp.concatenate([a, _mm(vT[cs, rs], x)], axis=0) for (rs, cs), a, x in zip(units, aT, akT)]
    z = [x - _mm(x.astype(BF16), l) for x, l in zip(z, lt)]
    pw = lt
    n = 2
    while n < chunk:
        pw = [_mm(p, p).astype(BF16) for p in pw]
        z = [x + _mm(x.astype(BF16), p) for x, p in zip(z, pw)]
        n *= 2
    pz = [(-x).astype(BF16) for x in z]
    g3 = [_mm_nt(a, p) for a, p in zip(arb, pz)]
    g4 = [_mm(p, b_end[rs, cs]) for (rs, cs), p in zip(units, pz)]
    yin = [g[:, hd:] + _mm(a, v[rs, cs]) for (rs, cs), g, a in zip(units, g3, ark)]
    vk = [_mm(vT[cs, rs], k_end[rs, cs]) for rs, cs in units]
    state = [s_scr[h] for h in range(N_HEADS)]
    for c in range(chunks_per_step):
        etot = jnp.exp(tots[c])
        for h in range(N_HEADS):
            u = c * N_HEADS + h
            rs, cs = units[u]
            s = state[h]
            sb = s.astype(BF16)
            y_ref[rs, cs] = _mm_nt((rho[rs, cs] + g3[u][:, :hd]).astype(BF16), sb) + yin[u]
            state[h] = s * etot[:, cs] + _mm(sb, g4[u][:hd].astype(BF16)) + g4[u][hd:] + vk[u]
    for h in range(N_HEADS):
        s_scr[h] = state[h]

    @pl.when(j == pl.num_programs(1) - 1)
    def _():
        sfin_ref[0] = s_scr[...]


def _rwkv_scan(r, lw, k, v, kk, b, s0, seq_len, chunks_per_step):
    m, d = r.shape
    nseq = m // seq_len
    step = WKV_CHUNK * chunks_per_step
    nsteps = seq_len // step

    def tr(z):
        return z.reshape(nseq, seq_len, d).transpose(0, 2, 1)

    tm_spec = pl.BlockSpec((step, d), lambda i, j: (i * nsteps + j, 0))
    fm_spec = pl.BlockSpec((1, d, step), lambda i, j: (i, 0, j))
    st_spec = pl.BlockSpec((1, N_HEADS, HEAD_DIM, HEAD_DIM), lambda i, j: (i, 0, 0, 0))
    return pl.pallas_call(
        functools.partial(_rwkv_scan_kernel, chunk=WKV_CHUNK, chunks_per_step=chunks_per_step),
        grid=(nseq, nsteps),
        in_specs=[tm_spec] * 6 + [fm_spec] * 3 + [st_spec],
        out_specs=[tm_spec, st_spec],
        out_shape=[jax.ShapeDtypeStruct((m, d), F32),
                   jax.ShapeDtypeStruct((nseq, N_HEADS, HEAD_DIM, HEAD_DIM), F32)],
        scratch_shapes=[pltpu.VMEM((N_HEADS, HEAD_DIM, HEAD_DIM), F32)],
        compiler_params=_cparams(("parallel", "arbitrary")),
        name="rwkv_scan",
    )(r, lw, k, v, kk, b, tr(kk), tr(lw), tr(v), s0)


def _l0_out_kernel(y_ref, bonus_ref, g_ref, oa_ref, h_ref, gnw_ref, gnb_ref, gm_ref, wa_ref, wb_ref,
                   gpost_ref, o_ref):
    y = y_ref[...]
    gmat = gm_ref[...]
    inv = 1.0 / HEAD_DIM
    yc = y - _mm(y, gmat, HI) * inv
    yn = yc * lax.rsqrt(_mm(yc * yc, gmat, HI) * inv + GN_EPS) * gnw_ref[...] + gnb_ref[...]
    ob = (yn + bonus_ref[...]) * g_ref[...]
    mix = _mm(oa_ref[...].astype(BF16), wa_ref[...]) + _mm(ob.astype(BF16), wb_ref[...])
    o_ref[...] = h_ref[...] + _rms(mix, gpost_ref[...])


def _l0_out(y, bonus, g, oa, h, prm, gmat, w_out, tm):
    m, dm = h.shape
    d = D_HEADS

    def rowblk(w):
        return pl.BlockSpec((tm, w), lambda i: (i, 0))

    def full(shape):
        return pl.BlockSpec(shape, lambda i: (0,) * len(shape))

    return pl.pallas_call(
        _l0_out_kernel,
        grid=(m // tm,),
        in_specs=[rowblk(d), rowblk(d), rowblk(d), rowblk(d), rowblk(dm), full((1, d)), full((1, d)),
                  full((d, d)), full((d, dm)), full((d, dm)), full((1, dm))],
        out_specs=rowblk(dm),
        out_shape=jax.ShapeDtypeStruct((m, dm), F32),
        compiler_params=_cparams(("parallel",)),
        name="l0_out",
    )(y, bonus, g, oa, h, prm['rw_gn_w'].reshape(1, d), prm['rw_gn_b'].reshape(1, d), gmat,
      w_out[:d], w_out[d:], prm['l0_norm_post'].reshape(1, dm))


FFN_COLS = 256


def _ffn_down_kernel(up_ref, pa_ref, pb_ref, cw_ref, cb_ref, wd_ref, h_ref, gpost_ref, o_ref, *, seq_len):
    i = pl.program_id(0)
    tm = up_ref.shape[0]
    dff = wd_ref.shape[0]
    row = lax.broadcasted_iota(jnp.int32, (tm, 1), 0)
    pos = row if seq_len is None else row % seq_len

    def conv(cols):
        u = up_ref[:, cols]
        if seq_len is None:
            p0 = jnp.where(i == 0, pb_ref[6:7, cols], pa_ref[6:7, cols])
            p1 = jnp.where(i == 0, pb_ref[7:8, cols], pa_ref[7:8, cols])
        else:
            p0, p1 = pa_ref[:, cols], pb_ref[:, cols]
        m1 = jnp.where(pos == 0, p1, pltpu.roll(u, 1, axis=0))
        m2 = jnp.where(pos == 0, p0, jnp.where(pos == 1, p1, pltpu.roll(u, 2, axis=0)))
        return cb_ref[:, cols] + cw_ref[0:1, cols] * m2 + cw_ref[1:2, cols] * m1 + cw_ref[2:3, cols] * u

    acc = jnp.zeros(o_ref.shape, F32)
    for c in range(dff // FFN_COLS):
        gate = conv(slice(c * FFN_COLS, (c + 1) * FFN_COLS))
        val = conv(slice(dff + c * FFN_COLS, dff + (c + 1) * FFN_COLS))
        act = (jax.nn.gelu(gate) * val).astype(BF16)
        acc = acc + _mm(act, wd_ref[c * FFN_COLS:(c + 1) * FFN_COLS, :])
    o_ref[...] = h_ref[...] + _rms(acc, gpost_ref[...])


def _ffn_down(up, prev_rows, conv_w, conv_b, w_down, h, g_post, tm, seq_len):
    m, n2 = up.shape
    dff, dm = w_down.shape
    assert dff % FFN_COLS == 0
    if seq_len is None:
        pa, pb = up, jnp.pad(prev_rows[0], ((6, 0), (0, 0)))
        pa_spec = pl.BlockSpec((8, n2), lambda i: (jnp.maximum(i * (tm // 8) - 1, 0), 0))
        pb_spec = pl.BlockSpec((8, n2), lambda i: (0, 0))
    else:
        pa = jnp.repeat(prev_rows[:, 0], seq_len, axis=0)
        pb = jnp.repeat(prev_rows[:, 1], seq_len, axis=0)
        pa_spec = pb_spec = pl.BlockSpec((tm, n2), lambda i: (i, 0))

    def full(shape):
        return pl.BlockSpec(shape, lambda i: (0,) * len(shape))

    return pl.pallas_call(
        functools.partial(_ffn_down_kernel, seq_len=seq_len),
        grid=(m // tm,),
        in_specs=[pl.BlockSpec((tm, n2), lambda i: (i, 0)), pa_spec, pb_spec, full((CONV_W, n2)),
                  full((1, n2)), full((dff, dm)), pl.BlockSpec((tm, dm), lambda i: (i, 0)), full((1, dm))],
        out_specs=pl.BlockSpec((tm, dm), lambda i: (i, 0)),
        out_shape=jax.ShapeDtypeStruct((m, dm), F32),
        compiler_params=_cparams(("parallel",)),
        name="ffn_down",
    )(up, pa, pb, conv_w, conv_b.reshape(1, n2), w_down, h, g_post.reshape(1, dm))


def _gmlp_kernel(h_ref, gpre_ref, win_ref, lnw_ref, lnb_ref, wmix_ref, bias_ref, wout_ref, gpost_ref,
                 o_ref, cv_ref, gated_scr):
    h = h_ref[...]
    tm = h.shape[0]
    dc = wout_ref.shape[0]
    rm = wmix_ref.shape[1]
    gw = dc // GMLP_GROUPS
    z = jax.nn.gelu(_mm(_rms(h, gpre_ref[...]).astype(BF16), win_ref[...]))
    v = z[:, dc:]
    vc = v - jnp.mean(v, axis=-1, keepdims=True)
    vn = vc * lax.rsqrt(jnp.mean(vc * vc, axis=-1, keepdims=True) + LN_EPS) * lnw_ref[...] + lnb_ref[...]
    cv_ref[...] = vn
    for s in range(tm // rm):
        rows = slice(s * rm, (s + 1) * rm)
        for g in range(GMLP_GROUPS):
            cols = slice(g * gw, (g + 1) * gw)
            mixed = _mm(wmix_ref[g], vn[rows, cols].astype(BF16)) + bias_ref[:, cols]
            gated_scr[rows, cols] = (z[rows, cols] * mixed).astype(BF16)
    o_ref[...] = h + _rms(_mm(gated_scr[...], wout_ref[...]), gpost_ref[...])


def _gmlp(h, g_pre, w_in, ln_w, ln_b, wmix, bias, w_out, g_post, tm):
    m, dm = h.shape
    dc = w_out.shape[0]
    rm = wmix.shape[1]

    def full(shape):
        return pl.BlockSpec(shape, lambda i: (0,) * len(shape))

    return pl.pallas_call(
        _gmlp_kernel,
        grid=(m // tm,),
        in_specs=[pl.BlockSpec((tm, dm), lambda i: (i, 0)), full((1, dm)), full((dm, 2 * dc)), full((1, dc)),
                  full((1, dc)), full((GMLP_GROUPS, rm, rm)), full((rm, dc)), full((dc, dm)), full((1, dm))],
        out_specs=[pl.BlockSpec((tm, dm), lambda i: (i, 0)), pl.BlockSpec((tm, dc), lambda i: (i, 0))],
        out_shape=[jax.ShapeDtypeStruct((m, dm), F32), jax.ShapeDtypeStruct((m, dc), F32)],
        scratch_shapes=[pltpu.VMEM((tm, dc), BF16)],
        compiler_params=_cparams(("parallel",)),
        name="gmlp",
    )(h, g_pre.reshape(1, dm), w_in, ln_w.reshape(1, dc), ln_b.reshape(1, dc), wmix, bias, w_out,
      g_post.reshape(1, dm))


def _trunk(x, attend, prev_xn, wkv0, conv0, prm, w, *, tile, long_seq):
    nseq, t, dm = x.shape
    m = nseq * t
    h = x.reshape(m, dm)
    seq_len = None if long_seq else t

    g0 = prm['l0_norm_pre']
    qkv = _norm_matmul(h, g0, w['w_qkv'], tile, 512)
    pb = _norm_matmul(h, g0, w['w_b'], tile, 256)
    shift_new, prev_proj = _shift_rows(x[:, -1], g0, prev_xn, w['w_b'])
    q, k_new, v_new = qkv[:, :D_HEADS], qkv[:, D_HEADS:2 * D_HEADS], qkv[:, 2 * D_HEADS:]
    if long_seq:
        first = jnp.concatenate([prev_proj, pb[tile - 1:m - 1:tile]], axis=0).reshape(m // tile, 1, -1)
    else:
        first = jnp.repeat(prev_proj, t, axis=0)
    r, lw, kmod, vv, kk, bb, gate, bonus = _rwkv_prep(pb, first, prm, w['gmat'], tile, seq_len)
    o_a = attend(q, k_new, v_new)

    t_pad = -(-t // WKV_CHUNK) * WKV_CHUNK

    def padt(z):
        if t_pad == t:
            return z
        return jnp.pad(z.reshape(nseq, t, -1), ((0, 0), (0, t_pad - t), (0, 0))).reshape(nseq * t_pad, -1)

    y, wkv_new = _rwkv_scan(padt(r), padt(lw), padt(kmod), padt(vv), padt(kk), padt(bb), wkv0, t_pad,
                            2 if t_pad % (2 * WKV_CHUNK) == 0 else 1)
    if t_pad != t:
        y = y.reshape(nseq, t_pad, -1)[:, :t].reshape(m, -1)
    h = _l0_out(y, bonus, gate, o_a, h, prm, w['gmat'], w['w_out0'], tile)

    conv_rows = []

    def ffn(h, layer):
        up = _norm_matmul(h, prm['ffn_norm_pre'][layer], w['w_up'][layer], tile, 512)
        conv_rows.append(up.reshape(nseq, t, -1)[:, t - (CONV_W - 1):])
        return _ffn_down(up, conv0[layer], prm['ffn_conv_w'][layer], prm['ffn_conv_b'][layer],
                         w['w_down'][layer], h, prm['ffn_norm_post'][layer], min(tile, 256), seq_len)

    h = ffn(h, 0)

    if long_seq:
        wmix, bias = w['wmix'], w['gbias']
    else:
        eye = jnp.eye(tile // t, dtype=F32)
        wmix = jnp.einsum('ab,gij->gaibj', eye, w['wmix'][:, :t, :t].astype(F32))
        wmix = wmix.reshape(GMLP_GROUPS, tile, tile).astype(BF16)
        bias = jnp.tile(w['gbias'][:t], (tile // t, 1))
    h, chunk_v = _gmlp(h, prm['l1_norm_pre'], w['w_in1'], prm['gm_ln_w'], prm['gm_ln_b'], wmix, bias,
                       w['w_out1'], prm['l1_norm_post'], tile)
    h = ffn(h, 1)
    return (h.reshape(nseq, t, dm), k_new.reshape(nseq, t, N_HEADS, HEAD_DIM),
            v_new.reshape(nseq, t, N_HEADS, HEAD_DIM), wkv_new, shift_new, chunk_v.reshape(nseq, t, -1),
            jnp.stack(conv_rows))


def kernel(x_prompt, x_sample, cache_k, cache_v, state_wkv, state_shift, state_ffn_conv, page_table,
           l0_norm_pre, l0_norm_post, l0_w_in, rw_mu, rw_w0, rw_w2, rw_a0, rw_a2, rw_g2, rw_k_k, rw_k_a,
           rw_r_k, rw_gn_w, rw_gn_b, l0_w_out, l1_norm_pre, l1_norm_post, l1_w_in, gm_ln_w, gm_ln_b,
           gm_ws, gm_bs, l1_w_out, ffn_norm_pre, ffn_norm_post, ffn_w_up, ffn_conv_w, ffn_conv_b,
           ffn_w_down):
    prm = dict(l0_norm_pre=l0_norm_pre, l0_norm_post=l0_norm_post, rw_mu=rw_mu, rw_w0=rw_w0, rw_w2=rw_w2,
               rw_a0=rw_a0, rw_a2=rw_a2, rw_g2=rw_g2, rw_k_k=rw_k_k, rw_k_a=rw_k_a, rw_r_k=rw_r_k,
               rw_gn_w=rw_gn_w, rw_gn_b=rw_gn_b, l1_norm_pre=l1_norm_pre, l1_norm_post=l1_norm_post,
               gm_ln_w=gm_ln_w, gm_ln_b=gm_ln_b, ffn_norm_pre=ffn_norm_pre, ffn_norm_post=ffn_norm_post,
               ffn_conv_w=ffn_conv_w, ffn_conv_b=ffn_conv_b)
    d_a = D_HEADS
    dc = l1_w_out.shape[0]
    tri = jnp.tril(jnp.ones((GMLP_CHUNK, GMLP_CHUNK), dtype=bool))
    hd = jnp.arange(D_HEADS) // HEAD_DIM
    w = dict(
        w_qkv=l0_w_in[:, :3 * d_a].astype(BF16),
        w_b=l0_w_in[:, 3 * d_a:].astype(BF16),
        w_out0=l0_w_out.astype(BF16),
        w_up=ffn_w_up.astype(BF16),
        w_down=ffn_w_down.astype(BF16),
        w_in1=l1_w_in.astype(BF16),
        w_out1=l1_w_out.astype(BF16),
        wmix=jnp.where(tri[None], gm_ws, 0.0).astype(BF16),
        gbias=jnp.repeat(gm_bs.T, dc // GMLP_GROUPS, axis=1),
        gmat=(hd[:, None] == hd[None, :]).astype(F32),
    )
    slopes = jnp.exp2(-8.0 * jnp.arange(1, N_HEADS + 1, dtype=F32) / N_HEADS)

    bp, tp, dm = x_prompt.shape
    assert bp == 1
    dt = x_prompt.dtype
    depth = ffn_w_up.shape[0]
    wkv0 = jnp.zeros((bp, N_HEADS, HEAD_DIM, HEAD_DIM), dt)
    shift0 = jnp.zeros((bp, dm), dt)
    conv0 = jnp.zeros((depth, bp, CONV_W - 1, ffn_w_up.shape[2]), dt)

    def attend_prompt(q, k, v):
        return _moba_prompt(q, k, v, slopes)

    (y_prompt, k_prompt, v_prompt, wkv_prompt, shift_prompt, _unused_chunk_v,
     ffn_conv_prompt) = _trunk(x_prompt, attend_prompt, shift0, wkv0, conv0, prm, w, tile=512, long_seq=True)

    bs, ts, _ = x_sample.shape

    def attend_sample(q, k, v):
        o = _moba_sample(q.reshape(bs, ts, -1), k.reshape(bs, ts, -1), v.reshape(bs, ts, -1), cache_k, cache_v,
                         page_table, slopes)
        return o.reshape(bs * ts, -1)

    (y_sample, k_sample, v_sample, wkv_sample, shift_sample, chunk_v_sample,
     ffn_conv_sample) = _trunk(x_sample, attend_sample, state_shift, state_wkv, state_ffn_conv, prm, w,
                               tile=bs * ts, long_seq=False)

    return (y_prompt, y_sample, k_prompt, v_prompt, k_sample, v_sample, wkv_prompt, wkv_sample,
            shift_prompt, shift_sample, chunk_v_sample, ffn_conv_prompt, ffn_conv_sample)
```

```python
import functools

import jax
import jax.numpy as jnp
from jax import lax
from jax.experimental import pallas as pl
from jax.experimental.pallas import tpu as pltpu

F32 = jnp.float32
BF16 = jnp.bfloat16
HI = lax.Precision.HIGHEST

HEAD_DIM = 64
N_HEADS = 8
D_HEADS = N_HEADS * HEAD_DIM
MOBA_BLOCK = 256
MOBA_TOPK = 3
MOBA_GROUP = 4
SCALE = HEAD_DIM ** -0.5
D_DECAY_LORA = 64
D_AAA_LORA = 64
GMLP_GROUPS = 8
GMLP_CHUNK = 128
CONV_W = 3
NORM_EPS = 1e-6
LN_EPS = 1e-5
GN_EPS = 64e-5
NEG = -1e30
WKV_CHUNK = 64
V7X_VMEM_LIMIT = 56 * 1024 * 1024


def _cparams(sem, vmem=V7X_VMEM_LIMIT):
    return pltpu.CompilerParams(dimension_semantics=sem, vmem_limit_bytes=vmem)


def _mm(a, b, prec=None):
    return jnp.dot(a, b, precision=prec, preferred_element_type=F32)


def _mm_nt(a, b, prec=None):
    return lax.dot_general(a, b, (((1,), (1,)), ((), ())), precision=prec, preferred_element_type=F32)


def _rms(x, g):
    return x * lax.rsqrt(jnp.mean(x * x, axis=-1, keepdims=True) + NORM_EPS) * g


def _norm_matmul_kernel(x_ref, g_ref, w_ref, o_ref, xn_ref):
    @pl.when(pl.program_id(1) == 0)
    def _():
        xn_ref[...] = _rms(x_ref[...], g_ref[...]).astype(BF16)

    o_ref[...] = _mm(xn_ref[...], w_ref[...])


def _norm_matmul(x, g, w, tm, tn):
    m, kd = x.shape
    n = w.shape[1]
    return pl.pallas_call(
        _norm_matmul_kernel,
        grid=(m // tm, n // tn),
        in_specs=[pl.BlockSpec((tm, kd), lambda i, j: (i, 0)),
                  pl.BlockSpec((1, kd), lambda i, j: (0, 0)),
                  pl.BlockSpec((kd, tn), lambda i, j: (0, j))],
        out_specs=pl.BlockSpec((tm, tn), lambda i, j: (i, j)),
        out_shape=jax.ShapeDtypeStruct((m, n), F32),
        scratch_shapes=[pltpu.VMEM((tm, kd), BF16)],
        compiler_params=_cparams(("parallel", "arbitrary")),
        name="norm_matmul",
    )(x, g.reshape(1, kd), w)


def _shift_rows_kernel(x_ref, g_ref, prev_ref, w_ref, xn_ref, pp_ref):
    xn_ref[...] = _rms(x_ref[...], g_ref[...])
    pp_ref[...] = _mm(prev_ref[...].astype(BF16), w_ref[...])


def _shift_rows(x_last, g, prev_xn, w_b):
    b, kd = x_last.shape
    n = w_b.shape[1]
    bp = -(-b // 8) * 8
    pad = ((0, bp - b), (0, 0))
    xn, pp = pl.pallas_call(
        _shift_rows_kernel,
        out_shape=(jax.ShapeDtypeStruct((bp, kd), F32), jax.ShapeDtypeStruct((bp, n), F32)),
        compiler_params=_cparams(None),
        name="shift_rows",
    )(jnp.pad(x_last, pad), g.reshape(1, kd), jnp.pad(prev_xn, pad), w_b)
    return xn[:b], pp[:b]


def _kmeans_kernel(k_ref, o_ref):
    k = k_ref[0]
    nb = k.shape[0] // MOBA_BLOCK
    o_ref[0] = jnp.mean(k.reshape(nb, MOBA_BLOCK, HEAD_DIM), axis=1)


def _moba_prompt_kernel(slope_ref, qT_ref, km_ref, k_ref, vT_ref, oT_ref, sel_scr):
    i = pl.program_id(1)
    nb = km_ref.shape[1]
    blk = MOBA_BLOCK
    qf = qT_ref[0]
    gate = _mm(km_ref[0], qf, HI)
    nidx = lax.broadcasted_iota(jnp.int32, gate.shape, 0)
    g = jnp.where(nidx < i, gate, -jnp.inf)
    sel = jnp.zeros(gate.shape, F32)
    for _ in range(MOBA_TOPK):
        top = jnp.max(g, axis=0, keepdims=True)
        idx = jnp.min(jnp.where(g == top, nidx, nb), axis=0, keepdims=True)
        pick = (nidx == idx) & (top > -jnp.inf)
        sel = jnp.where(pick, 1.0, sel)
        g = jnp.where(pick, -jnp.inf, g)
    sel_scr[...] = sel

    qs = (qf * SCALE).astype(BF16)
    slope = slope_ref[0]
    ki = lax.broadcasted_iota(jnp.int32, (blk, blk), 0)
    qi = lax.broadcasted_iota(jnp.int32, (blk, blk), 1)
    dm = (ki - qi).astype(F32)
    bias = slope * dm

    own = pl.multiple_of(i * blk, blk)
    s = jnp.where(dm <= 0.0, _mm(k_ref[0, pl.ds(own, blk), :], qs) + bias, NEG)
    m0 = jnp.max(s, axis=0, keepdims=True)
    p = jnp.exp(s - m0)
    l0 = jnp.sum(p, axis=0, keepdims=True)
    acc0 = _mm(vT_ref[0, :, pl.ds(own, blk)], p.astype(BF16))

    grp = MOBA_GROUP
    gk = grp * blk
    bias_g = jnp.concatenate([bias] * grp, axis=0)

    def body(g, carry):
        m, l, acc = carry
        n0 = pl.multiple_of(g * grp, grp)
        start = pl.multiple_of(g * gk, gk)
        s = _mm(k_ref[0, pl.ds(start, gk), :], qs) + bias_g
        picked = [sel_scr[pl.ds(n0 + u, 1), :] > 0.0 for u in range(grp)]
        m_new = m
        shift = []
        for u in range(grp):
            c_u = slope * ((n0 + u - i) * blk).astype(F32)
            mb = jnp.max(s[u * blk:(u + 1) * blk], axis=0, keepdims=True) + c_u
            m_new = jnp.where(picked[u], jnp.maximum(m_new, mb), m_new)
            shift.append(c_u)
        alpha = jnp.exp(m - m_new)
        ps = []
        for u in range(grp):
            off = jnp.where(picked[u], m_new - shift[u], -NEG)
            p = jnp.exp(s[u * blk:(u + 1) * blk] - off)
            l = l + jnp.sum(p, axis=0, keepdims=True) if u else alpha * l + jnp.sum(p, axis=0, keepdims=True)
            ps.append(p.astype(BF16))
        acc = alpha * acc + _mm(vT_ref[0, :, pl.ds(start, gk)], jnp.concatenate(ps, axis=0))
        return m_new, l, acc

    _, l, acc = lax.fori_loop(0, (i + grp - 1) // grp, body, (m0, l0, acc0))
    oT_ref[0] = acc / l


def _moba_prompt(q, k, v, slopes):
    t = q.shape[0]
    assert t % (MOBA_BLOCK * MOBA_GROUP) == 0
    nb = t // MOBA_BLOCK

    def heads_t(z):
        return z.reshape(t, N_HEADS, HEAD_DIM).transpose(1, 2, 0)

    k_hm = k.reshape(t, N_HEADS, HEAD_DIM).transpose(1, 0, 2)
    kmean = pl.pallas_call(
        _kmeans_kernel,
        grid=(N_HEADS,),
        in_specs=[pl.BlockSpec((1, t, HEAD_DIM), lambda h: (h, 0, 0))],
        out_specs=pl.BlockSpec((1, nb, HEAD_DIM), lambda h: (h, 0, 0)),
        out_shape=jax.ShapeDtypeStruct((N_HEADS, nb, HEAD_DIM), F32),
        compiler_params=_cparams(("parallel",)),
        name="moba_kmeans",
    )(k_hm)
    oT = pl.pallas_call(
        _moba_prompt_kernel,
        grid=(N_HEADS, nb),
        in_specs=[pl.BlockSpec((1, 1, 1), lambda h, i: (h, 0, 0)),
                  pl.BlockSpec((1, HEAD_DIM, MOBA_BLOCK), lambda h, i: (h, 0, i)),
                  pl.BlockSpec((1, nb, HEAD_DIM), lambda h, i: (h, 0, 0)),
                  pl.BlockSpec((1, t, HEAD_DIM), lambda h, i: (h, 0, 0)),
                  pl.BlockSpec((1, HEAD_DIM, t), lambda h, i: (h, 0, 0))],
        out_specs=pl.BlockSpec((1, HEAD_DIM, MOBA_BLOCK), lambda h, i: (h, 0, i)),
        out_shape=jax.ShapeDtypeStruct((N_HEADS, HEAD_DIM, t), F32),
        scratch_shapes=[pltpu.VMEM((nb, MOBA_BLOCK), F32)],
        compiler_params=_cparams(("parallel", "arbitrary")),
        name="moba_prompt",
    )(slopes.reshape(N_HEADS, 1, 1), heads_t(q), kmean, k_hm.astype(BF16), heads_t(v).astype(BF16))
    return oT.transpose(2, 0, 1).reshape(t, D_HEADS)


def _moba_gate_kernel(pt_ref, q_ref, ka_ref, kb_ref, sel_ref, km_scr):
    del pt_ref
    n = pl.program_id(1)
    nb = km_scr.shape[0]
    km_scr[n] = jnp.sum(ka_ref[0] + kb_ref[0], axis=-1) * (1.0 / MOBA_BLOCK)

    @pl.when(n == nb - 1)
    def _():
        km = km_scr[...].reshape(nb * N_HEADS, HEAD_DIM)
        nq = q_ref.shape[1]
        qall = q_ref[0].reshape(nq * N_HEADS, HEAD_DIM)
        g3 = _mm_nt(km, qall, HI).reshape(nb, N_HEADS, nq * N_HEADS)
        hrow = lax.broadcasted_iota(jnp.int32, g3.shape, 1)
        hcol = lax.broadcasted_iota(jnp.int32, g3.shape, 2) % N_HEADS
        g = jnp.sum(jnp.where(hrow == hcol, g3, 0.0), axis=1)
        bidx = lax.broadcasted_iota(jnp.int32, g.shape, 0)
        sel = jnp.zeros(g.shape, F32)
        for _ in range(MOBA_TOPK):
            top = jnp.max(g, axis=0, keepdims=True)
            idx = jnp.min(jnp.where(g == top, bidx, nb), axis=0, keepdims=True)
            pick = (bidx == idx) & (top > -jnp.inf)
            sel = jnp.where(pick, 1.0, sel)
            g = jnp.where(pick, -jnp.inf, g)
        sel_ref[0] = sel


def _moba_attend_kernel(pg_ref, slope_ref, q_ref, kn_ref, vn_ref, dist_ref, ck_ref, cv_ref, o_ref,
                        kbuf, vbuf, sem, *, n_visits):
    i = pl.program_id(0)
    nsteps = pl.num_programs(0)
    page = ck_ref.shape[3]
    n_copies = n_visits * (MOBA_BLOCK // page)

    def copies(step, slot):
        h = step % N_HEADS
        out = []
        for c in range(n_copies):
            pg = pg_ref[step * n_copies + c]
            cols = pl.ds(c * page, page)
            out.append(pltpu.make_async_copy(ck_ref.at[pg, h], kbuf.at[slot, :, cols], sem.at[0, slot]))
            out.append(pltpu.make_async_copy(cv_ref.at[pg, h], vbuf.at[slot, :, cols], sem.at[1, slot]))
        return out

    @pl.when(i == 0)
    def _():
        for cp in copies(0, 0):
            cp.start()

    @pl.when(i + 1 < nsteps)
    def _():
        for cp in copies(i + 1, (i + 1) % 2):
            cp.start()

    slot = i % 2
    for cp in copies(i, slot):
        cp.wait()

    nq = q_ref.shape[2]
    slope = slope_ref[0]
    qs = (q_ref[0, 0] * SCALE).astype(BF16)
    ncol = n_visits * MOBA_BLOCK
    s = _mm(qs, kbuf[slot].astype(BF16)) - slope * dist_ref[0, 0]
    row = lax.broadcasted_iota(jnp.int32, (nq, ncol), 0)
    col = lax.broadcasted_iota(jnp.int32, (nq, ncol), 1)
    s = jnp.where(col // (ncol // nq) == row, s, NEG)
    nk = kn_ref.shape[2]
    dist_o = (lax.broadcasted_iota(jnp.int32, (nq, nk), 0)
              - lax.broadcasted_iota(jnp.int32, (nq, nk), 1)).astype(F32)
    s_o = jnp.where(dist_o >= 0.0, _mm_nt(qs, kn_ref[0, 0].astype(BF16)) - slope * dist_o, NEG)
    m = jnp.maximum(jnp.max(s, axis=1, keepdims=True), jnp.max(s_o, axis=1, keepdims=True))
    p = jnp.exp(s - m)
    p_o = jnp.exp(s_o - m)
    l = jnp.sum(p, axis=1, keepdims=True) + jnp.sum(p_o, axis=1, keepdims=True)
    acc = _mm_nt(p.astype(BF16), vbuf[slot].astype(BF16)) + _mm(p_o.astype(BF16), vn_ref[0, 0].astype(BF16))
    o_ref[0, 0] = acc / l


def _moba_sample(q, k, v, cache_k, cache_v, page_table, slopes):
    b, s, _ = q.shape
    n_pool, page, _, _ = cache_k.shape
    n_pages = page_table.shape[1]
    past = n_pages * page
    pages_per_blk = MOBA_BLOCK // page
    assert pages_per_blk == 2 and past % MOBA_BLOCK == 0
    nb = past // MOBA_BLOCK
    assert nb >= MOBA_TOPK
    pt = page_table.reshape(-1).astype(jnp.int32)

    ck = cache_k.transpose(0, 2, 3, 1)
    cv = cache_v.transpose(0, 2, 3, 1)

    def page_spec(j):
        return pl.BlockSpec((1, N_HEADS, HEAD_DIM, page), lambda i, n, pt: (pt[i * n_pages + 2 * n + j], 0, 0, 0))

    sel = pl.pallas_call(
        _moba_gate_kernel,
        grid_spec=pltpu.PrefetchScalarGridSpec(
            num_scalar_prefetch=1,
            grid=(b, nb),
            in_specs=[pl.BlockSpec((1, s, N_HEADS, HEAD_DIM), lambda i, n, pt: (i, 0, 0, 0)),
                      page_spec(0), page_spec(1)],
            out_specs=pl.BlockSpec((1, nb, s * N_HEADS), lambda i, n, pt: (i, 0, 0)),
            scratch_shapes=[pltpu.VMEM((nb, N_HEADS, HEAD_DIM), F32)]),
        out_shape=jax.ShapeDtypeStruct((b, nb, s * N_HEADS), F32),
        compiler_params=_cparams(("parallel", "arbitrary")),
        name="moba_gate",
    )(pt, q.reshape(b, s, N_HEADS, HEAD_DIM), ck, ck)

    n_visits = s * MOBA_TOPK
    blk = jnp.argsort(-sel, axis=1, stable=True)[:, :MOBA_TOPK, :]
    blk = blk.reshape(b, MOBA_TOPK, s, N_HEADS).transpose(0, 3, 2, 1).reshape(b, N_HEADS, n_visits)
    pages = page_table[jnp.arange(b)[:, None, None, None],
                       blk[..., None] * pages_per_blk + jnp.arange(pages_per_blk)]
    kpos = blk[..., None] * MOBA_BLOCK + jnp.arange(MOBA_BLOCK)
    qpos = past + jnp.arange(n_visits) // MOBA_TOPK
    dist = (qpos[None, None, :, None] - kpos).astype(F32).reshape(b, N_HEADS, 1, n_visits * MOBA_BLOCK)

    nk = 16

    def heads(z, rows):
        z = jnp.pad(z, ((0, 0), (0, rows - s), (0, 0)))
        return z.reshape(b, rows, N_HEADS, HEAD_DIM).transpose(0, 2, 1, 3)

    def bh_spec(r, w):
        return pl.BlockSpec((1, 1, r, w), lambda i, pg: (i // N_HEADS, i % N_HEADS, 0, 0))

    out = pl.pallas_call(
        functools.partial(_moba_attend_kernel, n_visits=n_visits),
        grid_spec=pltpu.PrefetchScalarGridSpec(
            num_scalar_prefetch=1,
            grid=(b * N_HEADS,),
            in_specs=[pl.BlockSpec((1, 1, 1), lambda i, pg: (i % N_HEADS, 0, 0)),
                      bh_spec(s, HEAD_DIM), bh_spec(nk, HEAD_DIM), bh_spec(nk, HEAD_DIM),
                      bh_spec(1, n_visits * MOBA_BLOCK),
                      pl.BlockSpec(memory_space=pl.ANY), pl.BlockSpec(memory_space=pl.ANY)],
            out_specs=bh_spec(s, HEAD_DIM),
            scratch_shapes=[pltpu.VMEM((2, HEAD_DIM, n_visits * MOBA_BLOCK), F32),
                            pltpu.VMEM((2, HEAD_DIM, n_visits * MOBA_BLOCK), F32),
                            pltpu.SemaphoreType.DMA((2, 2))]),
        out_shape=jax.ShapeDtypeStruct((b, N_HEADS, s, HEAD_DIM), F32),
        compiler_params=_cparams(("arbitrary",)),
        name="moba_attend",
    )(pages.reshape(-1).astype(jnp.int32), slopes.reshape(N_HEADS, 1, 1), heads(q, s), heads(k, nk), heads(v, nk),
      dist, ck, cv)
    return out.transpose(0, 2, 1, 3).reshape(b, s, D_HEADS)


def _softplus(z):
    return jnp.maximum(z, 0.0) + jnp.log1p(jnp.exp(-jnp.abs(z)))


def _rwkv_prep_kernel(pb_ref, first_ref, mu_ref, w0_ref, w2_ref, a0_ref, a2_ref, g2_ref, kk_ref, ka_ref,
                      rk_ref, gm_ref, r_o, lw_o, k_o, v_o, kk_o, b_o, g_o, bonus_o, *, seq_len):
    pb = pb_ref[...]
    tm = pb.shape[0]
    row = lax.broadcasted_iota(jnp.int32, (tm, 1), 0)
    if seq_len is None:
        starts, first = row == 0, first_ref[0]
    else:
        starts, first = (row % seq_len) == 0, first_ref[...]
    prev = jnp.where(starts, first, pltpu.roll(pb, 1, axis=0))
    xm = pb + mu_ref[...] * (prev - pb)
    d = D_HEADS
    o1, o2, o3 = 3 * d, 3 * d + D_DECAY_LORA, 3 * d + D_DECAY_LORA + D_AAA_LORA
    r, k, v = xm[:, 0:d], xm[:, d:2 * d], xm[:, 2 * d:o1]
    xw, xa, xg = xm[:, o1:o2], xm[:, o2:o3], xm[:, o3:]
    logw = -_softplus(-(w0_ref[...] + _mm(jnp.tanh(xw), w2_ref[...]))) - 0.5
    a = jax.nn.sigmoid(a0_ref[...] + _mm(xa, a2_ref[...]))
    gmat = gm_ref[...]
    kk = k * kk_ref[...]
    kk = kk / jnp.maximum(jnp.sqrt(_mm(kk * kk, gmat, HI)), 1e-12)
    k = k * (1.0 + (a - 1.0) * ka_ref[...])
    r_o[...] = r
    lw_o[...] = -jnp.exp(logw)
    k_o[...] = k
    v_o[...] = v
    kk_o[...] = kk
    b_o[...] = kk * a
    g_o[...] = _mm(jax.nn.sigmoid(xg), g2_ref[...])
    bonus_o[...] = _mm(r * k * rk_ref[...], gmat, HI) * v


def _rwkv_prep(pb, first, prm, gmat, tm, seq_len):
    m, n = pb.shape
    d = D_HEADS
    if seq_len is None:
        first_spec = pl.BlockSpec((1, 1, n), lambda i: (i, 0, 0))
    else:
        first_spec = pl.BlockSpec((tm, n), lambda i: (i, 0))

    def full(shape):
        return pl.BlockSpec(shape, lambda i: (0,) * len(shape))

    def vec(z):
        return z.reshape(1, -1)

    outs = pl.pallas_call(
        functools.partial(_rwkv_prep_kernel, seq_len=seq_len),
        grid=(m // tm,),
        in_specs=[pl.BlockSpec((tm, n), lambda i: (i, 0)), first_spec,
                  full((1, n)), full((1, d)), full((D_DECAY_LORA, d)), full((1, d)), full((D_AAA_LORA, d)),
                  full((n - 3 * d - D_DECAY_LORA - D_AAA_LORA, d)), full((1, d)), full((1, d)), full((1, d)),
                  full((d, d))],
        out_specs=[pl.BlockSpec((tm, d), lambda i: (i, 0))] * 8,
        out_shape=[jax.ShapeDtypeStruct((m, d), F32)] * 8,
        compiler_params=_cparams(("parallel",)),
        name="rwkv_prep",
    )(pb, first, vec(prm['rw_mu']), vec(prm['rw_w0']), prm['rw_w2'], vec(prm['rw_a0']), prm['rw_a2'],
      prm['rw_g2'], vec(prm['rw_k_k']), vec(prm['rw_k_a']), vec(prm['rw_r_k']), gmat)
    return outs


def _split3(x):
    p1 = x.astype(BF16)
    r1 = x - p1.astype(F32)
    p2 = r1.astype(BF16)
    p3 = (r1 - p2.astype(F32)).astype(BF16)
    return p1, p2, p3


def _mmb(a, b):
    return _mm(a.astype(BF16), b.astype(BF16))


def _mmb_nt(a, b):
    return _mm_nt(a.astype(BF16), b.astype(BF16))


def _rwkv_scan_kernel(r_ref, lw_ref, k_ref, v_ref, kk_ref, b_ref, kkT_ref, lwT_ref, vT_ref, s0_ref,
                      y_ref, sfin_ref, s_scr, *, chunk, chunks_per_step):
    j = pl.program_id(1)

    @pl.when(j == 0)
    def _():
        s_scr[...] = s0_ref[0]

    step = chunk * chunks_per_step
    d = D_HEADS
    si = lax.broadcasted_iota(jnp.int32, (step, step), 0)
    sj = lax.broadcasted_iota(jnp.int32, (step, step), 1)
    same = (si // chunk) == (sj // chunk)
    lbd = jnp.where(same & (si >= sj), 1.0, 0.0).astype(BF16)
    ubd = jnp.where(same & (si <= sj), 1.0, 0.0).astype(BF16)
    lw = lw_ref[...]
    lwT = lwT_ref[0]
    cl = sum(_mm(lbd, p) for p in _split3(lw))
    clT = sum(_mm(p, ubd) for p in _split3(lwT))
    tots = [cl[(c + 1) * chunk - 1:(c + 1) * chunk, :] for c in range(chunks_per_step)]
    tot_full = jnp.concatenate([jnp.broadcast_to(t, (chunk, d)) for t in tots], axis=0)
    e_neg = jnp.exp(-cl)
    e_end = jnp.exp(tot_full - cl)
    b = b_ref[...]
    k = k_ref[...]
    rho = r_ref[...] * jnp.exp(cl)
    beta, kap = b * e_neg, k * e_neg
    b_end, k_end = (b * e_end).astype(BF16), (k * e_end).astype(BF16)
    alphaT = kkT_ref[0] * jnp.exp(clT - lwT)
    vT = vT_ref[0].astype(BF16)
    v = v_ref[...].astype(BF16)

    ii = lax.broadcasted_iota(jnp.int32, (chunk, chunk), 0)
    jj = lax.broadcasted_iota(jnp.int32, (chunk, chunk), 1)
    low, upper = ii >= jj, ii < jj
    hd = HEAD_DIM
    units = [(slice(c * chunk, (c + 1) * chunk), slice(h * hd, (h + 1) * hd))
             for c in range(chunks_per_step) for h in range(N_HEADS)]
    bk = [jnp.concatenate([beta[rs, cs], kap[rs, cs]], axis=0).astype(BF16) for rs, cs in units]
    aT = [alphaT[cs, rs] for rs, cs in units]
    g1 = [_mm(x, a.astype(BF16)) for x, a in zip(bk, aT)]
    lt = [jnp.where(upper, g[:chunk], 0.0).astype(BF16) for g in g1]
    akT = [jnp.where(upper, g[chunk:], 0.0).astype(BF16) for g in g1]
    g2 = [_mm_nt(rho[rs, cs].astype(BF16), x) for (rs, cs), x in zip(units, bk)]
    arb = [jnp.where(low, g[:, :chunk], 0.0).astype(BF16) for g in g2]
    ark = [jnp.where(low, g[:, chunk:], 0.0).astype(BF16) for g in g2]
    z = [jnp.concatenate([a, _mm(vT[cs, rs], x)], axis=0) for (rs, cs), a, x in zip(units, aT, akT)]
    z = [x - _mm(x.astype(BF16), l) for x, l in zip(z, lt)]
    pw = lt
    n = 2
    while n < chunk:
        pw = [_mm(p, p).astype(BF16) for p in pw]
        z = [x + _mm(x.astype(BF16), p) for x, p in zip(z, pw)]
        n *= 2
    pz = [(-x).astype(BF16) for x in z]
    g3 = [_mm_nt(a, p) for a, p in zip(arb, pz)]
    g4 = [_mm(p, b_end[rs, cs]) for (rs, cs), p in zip(units, pz)]
    yin = [g[:, hd:] + _mm(a, v[rs, cs]) for (rs, cs), g, a in zip(units, g3, ark)]
    vk = [_mm(vT[cs, rs], k_end[rs, cs]) for rs, cs in units]
    state = [s_scr[h] for h in range(N_HEADS)]
    for c in range(chunks_per_step):
        etot = jnp.exp(tots[c])
        for h in range(N_HEADS):
            u = c * N_HEADS + h
            rs, cs = units[u]
            s = state[h]
            sb = s.astype(BF16)
            y_ref[rs, cs] = _mm_nt((rho[rs, cs] + g3[u][:, :hd]).astype(BF16), sb) + yin[u]
            state[h] = s * etot[:, cs] + _mm(sb, g4[u][:hd].astype(BF16)) + g4[u][hd:] + vk[u]
    for h in range(N_HEADS):
        s_scr[h] = state[h]

    @pl.when(j == pl.num_programs(1) - 1)
    def _():
        sfin_ref[0] = s_scr[...]


def _rwkv_scan(r, lw, k, v, kk, b, s0, seq_len, chunks_per_step):
    m, d = r.shape
    nseq = m // seq_len
    step = WKV_CHUNK * chunks_per_step
    nsteps = seq_len // step

    def tr(z):
        return z.reshape(nseq, seq_len, d).transpose(0, 2, 1)

    tm_spec = pl.BlockSpec((step, d), lambda i, j: (i * nsteps + j, 0))
    fm_spec = pl.BlockSpec((1, d, step), lambda i, j: (i, 0, j))
    st_spec = pl.BlockSpec((1, N_HEADS, HEAD_DIM, HEAD_DIM), lambda i, j: (i, 0, 0, 0))
    return pl.pallas_call(
        functools.partial(_rwkv_scan_kernel, chunk=WKV_CHUNK, chunks_per_step=chunks_per_step),
        grid=(nseq, nsteps),
        in_specs=[tm_spec] * 6 + [fm_spec] * 3 + [st_spec],
        out_specs=[tm_spec, st_spec],
        out_shape=[jax.ShapeDtypeStruct((m, d), F32),
                   jax.ShapeDtypeStruct((nseq, N_HEADS, HEAD_DIM, HEAD_DIM), F32)],
        scratch_shapes=[pltpu.VMEM((N_HEADS, HEAD_DIM, HEAD_DIM), F32)],
        compiler_params=_cparams(("parallel", "arbitrary")),
        name="rwkv_scan",
    )(r, lw, k, v, kk, b, tr(kk), tr(lw), tr(v), s0)


def _l0_out_kernel(y_ref, bonus_ref, g_ref, oa_ref, h_ref, gnw_ref, gnb_ref, gm_ref, wa_ref, wb_ref,
                   gpost_ref, o_ref):
    y = y_ref[...]
    gmat = gm_ref[...]
    inv = 1.0 / HEAD_DIM
    yc = y - _mm(y, gmat, HI) * inv
    yn = yc * lax.rsqrt(_mm(yc * yc, gmat, HI) * inv + GN_EPS) * gnw_ref[...] + gnb_ref[...]
    ob = (yn + bonus_ref[...]) * g_ref[...]
    mix = _mm(oa_ref[...].astype(BF16), wa_ref[...]) + _mm(ob.astype(BF16), wb_ref[...])
    o_ref[...] = h_ref[...] + _rms(mix, gpost_ref[...])


def _l0_out(y, bonus, g, oa, h, prm, gmat, w_out, tm):
    m, dm = h.shape
    d = D_HEADS

    def rowblk(w):
        return pl.BlockSpec((tm, w), lambda i: (i, 0))

    def full(shape):
        return pl.BlockSpec(shape, lambda i: (0,) * len(shape))

    return pl.pallas_call(
        _l0_out_kernel,
        grid=(m // tm,),
        in_specs=[rowblk(d), rowblk(d), rowblk(d), rowblk(d), rowblk(dm), full((1, d)), full((1, d)),
                  full((d, d)), full((d, dm)), full((d, dm)), full((1, dm))],
        out_specs=rowblk(dm),
        out_shape=jax.ShapeDtypeStruct((m, dm), F32),
        compiler_params=_cparams(("parallel",)),
        name="l0_out",
    )(y, bonus, g, oa, h, prm['rw_gn_w'].reshape(1, d), prm['rw_gn_b'].reshape(1, d), gmat,
      w_out[:d], w_out[d:], prm['l0_norm_post'].reshape(1, dm))


FFN_COLS = 256


def _ffn_down_kernel(up_ref, pa_ref, pb_ref, cw_ref, cb_ref, wd_ref, h_ref, gpost_ref, o_ref, *, seq_len):
    i = pl.program_id(0)
    tm = up_ref.shape[0]
    dff = wd_ref.shape[0]
    row = lax.broadcasted_iota(jnp.int32, (tm, 1), 0)
    pos = row if seq_len is None else row % seq_len

    def conv(cols):
        u = up_ref[:, cols]
        if seq_len is None:
            p0 = jnp.where(i == 0, pb_ref[6:7, cols], pa_ref[6:7, cols])
            p1 = jnp.where(i == 0, pb_ref[7:8, cols], pa_ref[7:8, cols])
        else:
            p0, p1 = pa_ref[:, cols], pb_ref[:, cols]
        m1 = jnp.where(pos == 0, p1, pltpu.roll(u, 1, axis=0))
        m2 = jnp.where(pos == 0, p0, jnp.where(pos == 1, p1, pltpu.roll(u, 2, axis=0)))
        return cb_ref[:, cols] + cw_ref[0:1, cols] * m2 + cw_ref[1:2, cols] * m1 + cw_ref[2:3, cols] * u

    acc = jnp.zeros(o_ref.shape, F32)
    for c in range(dff // FFN_COLS):
        gate = conv(slice(c * FFN_COLS, (c + 1) * FFN_COLS))
        val = conv(slice(dff + c * FFN_COLS, dff + (c + 1) * FFN_COLS))
        act = (jax.nn.gelu(gate) * val).astype(BF16)
        acc = acc + _mm(act, wd_ref[c * FFN_COLS:(c + 1) * FFN_COLS, :])
    o_ref[...] = h_ref[...] + _rms(acc, gpost_ref[...])


def _ffn_down(up, prev_rows, conv_w, conv_b, w_down, h, g_post, tm, seq_len):
    m, n2 = up.shape
    dff, dm = w_down.shape
    assert dff % FFN_COLS == 0
    if seq_len is None:
        pa, pb = up, jnp.pad(prev_rows[0], ((6, 0), (0, 0)))
        pa_spec = pl.BlockSpec((8, n2), lambda i: (jnp.maximum(i * (tm // 8) - 1, 0), 0))
        pb_spec = pl.BlockSpec((8, n2), lambda i: (0, 0))
    else:
        pa = jnp.repeat(prev_rows[:, 0], seq_len, axis=0)
        pb = jnp.repeat(prev_rows[:, 1], seq_len, axis=0)
        pa_spec = pb_spec = pl.BlockSpec((tm, n2), lambda i: (i, 0))

    def full(shape):
        return pl.BlockSpec(shape, lambda i: (0,) * len(shape))

    return pl.pallas_call(
        functools.partial(_ffn_down_kernel, seq_len=seq_len),
        grid=(m // tm,),
        in_specs=[pl.BlockSpec((tm, n2), lambda i: (i, 0)), pa_spec, pb_spec, full((CONV_W, n2)),
                  full((1, n2)), full((dff, dm)), pl.BlockSpec((tm, dm), lambda i: (i, 0)), full((1, dm))],
        out_specs=pl.BlockSpec((tm, dm), lambda i: (i, 0)),
        out_shape=jax.ShapeDtypeStruct((m, dm), F32),
        compiler_params=_cparams(("parallel",)),
        name="ffn_down",
    )(up, pa, pb, conv_w, conv_b.reshape(1, n2), w_down, h, g_post.reshape(1, dm))


def _gmlp_kernel(h_ref, gpre_ref, win_ref, lnw_ref, lnb_ref, wmix_ref, bias_ref, wout_ref, gpost_ref,
                 o_ref, cv_ref, gated_scr):
    h = h_ref[...]
    tm = h.shape[0]
    dc = wout_ref.shape[0]
    rm = wmix_ref.shape[1]
    gw = dc // GMLP_GROUPS
    z = jax.nn.gelu(_mm(_rms(h, gpre_ref[...]).astype(BF16), win_ref[...]))
    v = z[:, dc:]
    vc = v - jnp.mean(v, axis=-1, keepdims=True)
    vn = vc * lax.rsqrt(jnp.mean(vc * vc, axis=-1, keepdims=True) + LN_EPS) * lnw_ref[...] + lnb_ref[...]
    cv_ref[...] = vn
    for s in range(tm // rm):
        rows = slice(s * rm, (s + 1) * rm)
        for g in range(GMLP_GROUPS):
            cols = slice(g * gw, (g + 1) * gw)
            mixed = _mm(wmix_ref[g], vn[rows, cols].astype(BF16)) + bias_ref[:, cols]
            gated_scr[rows, cols] = (z[rows, cols] * mixed).astype(BF16)
    o_ref[...] = h + _rms(_mm(gated_scr[...], wout_ref[...]), gpost_ref[...])


def _gmlp(h, g_pre, w_in, ln_w, ln_b, wmix, bias, w_out, g_post, tm):
    m, dm = h.shape
    dc = w_out.shape[0]
    rm = wmix.shape[1]

    def full(shape):
        return pl.BlockSpec(shape, lambda i: (0,) * len(shape))

    return pl.pallas_call(
        _gmlp_kernel,
        grid=(m // tm,),
        in_specs=[pl.BlockSpec((tm, dm), lambda i: (i, 0)), full((1, dm)), full((dm, 2 * dc)), full((1, dc)),
                  full((1, dc)), full((GMLP_GROUPS, rm, rm)), full((rm, dc)), full((dc, dm)), full((1, dm))],
        out_specs=[pl.BlockSpec((tm, dm), lambda i: (i, 0)), pl.BlockSpec((tm, dc), lambda i: (i, 0))],
        out_shape=[jax.ShapeDtypeStruct((m, dm), F32), jax.ShapeDtypeStruct((m, dc), F32)],
        scratch_shapes=[pltpu.VMEM((tm, dc), BF16)],
        compiler_params=_cparams(("parallel",)),
        name="gmlp",
    )(h, g_pre.reshape(1, dm), w_in, ln_w.reshape(1, dc), ln_b.reshape(1, dc), wmix, bias, w_out,
      g_post.reshape(1, dm))


def _trunk(x, attend, prev_xn, wkv0, conv0, prm, w, *, tile, long_seq):
    nseq, t, dm = x.shape
    m = nseq * t
    h = x.reshape(m, dm)
    seq_len = None if long_seq else t

    g0 = prm['l0_norm_pre']
    qkv = _norm_matmul(h, g0, w['w_qkv'], tile, 512)
    pb = _norm_matmul(h, g0, w['w_b'], tile, 256)
    shift_new, prev_proj = _shift_rows(x[:, -1], g0, prev_xn, w['w_b'])
    q, k_new, v_new = qkv[:, :D_HEADS], qkv[:, D_HEADS:2 * D_HEADS], qkv[:, 2 * D_HEADS:]
    if long_seq:
        first = jnp.concatenate([prev_proj, pb[tile - 1:m - 1:tile]], axis=0).reshape(m // tile, 1, -1)
    else:
        first = jnp.repeat(prev_proj, t, axis=0)
    r, lw, kmod, vv, kk, bb, gate, bonus = _rwkv_prep(pb, first, prm, w['gmat'], tile, seq_len)
    o_a = attend(q, k_new, v_new)

    t_pad = -(-t // WKV_CHUNK) * WKV_CHUNK

    def padt(z):
        if t_pad == t:
            return z
        return jnp.pad(z.reshape(nseq, t, -1), ((0, 0), (0, t_pad - t), (0, 0))).reshape(nseq * t_pad, -1)

    y, wkv_new = _rwkv_scan(padt(r), padt(lw), padt(kmod), padt(vv), padt(kk), padt(bb), wkv0, t_pad,
                            2 if t_pad % (2 * WKV_CHUNK) == 0 else 1)
    if t_pad != t:
        y = y.reshape(nseq, t_pad, -1)[:, :t].reshape(m, -1)
    h = _l0_out(y, bonus, gate, o_a, h, prm, w['gmat'], w['w_out0'], tile)

    conv_rows = []

    def ffn(h, layer):
        up = _norm_matmul(h, prm['ffn_norm_pre'][layer], w['w_up'][layer], tile, 512)
        conv_rows.append(up.reshape(nseq, t, -1)[:, t - (CONV_W - 1):])
        return _ffn_down(up, conv0[layer], prm['ffn_conv_w'][layer], prm['ffn_conv_b'][layer],
                         w['w_down'][layer], h, prm['ffn_norm_post'][layer], min(tile, 256), seq_len)

    h = ffn(h, 0)

    if long_seq:
        wmix, bias = w['wmix'], w['gbias']
    else:
        eye = jnp.eye(tile // t, dtype=F32)
        wmix = jnp.einsum('ab,gij->gaibj', eye, w['wmix'][:, :t, :t].astype(F32))
        wmix = wmix.reshape(GMLP_GROUPS, tile, tile).astype(BF16)
        bias = jnp.tile(w['gbias'][:t], (tile // t, 1))
    h, chunk_v = _gmlp(h, prm['l1_norm_pre'], w['w_in1'], prm['gm_ln_w'], prm['gm_ln_b'], wmix, bias,
                       w['w_out1'], prm['l1_norm_post'], tile)
    h = ffn(h, 1)
    return (h.reshape(nseq, t, dm), k_new.reshape(nseq, t, N_HEADS, HEAD_DIM),
            v_new.reshape(nseq, t, N_HEADS, HEAD_DIM), wkv_new, shift_new, chunk_v.reshape(nseq, t, -1),
            jnp.stack(conv_rows))


def kernel(x_prompt, x_sample, cache_k, cache_v, state_wkv, state_shift, state_ffn_conv, page_table,
           l0_norm_pre, l0_norm_post, l0_w_in, rw_mu, rw_w0, rw_w2, rw_a0, rw_a2, rw_g2, rw_k_k, rw_k_a,
           rw_r_k, rw_gn_w, rw_gn_b, l0_w_out, l1_norm_pre, l1_norm_post, l1_w_in, gm_ln_w, gm_ln_b,
           gm_ws, gm_bs, l1_w_out, ffn_norm_pre, ffn_norm_post, ffn_w_up, ffn_conv_w, ffn_conv_b,
           ffn_w_down):
    prm = dict(l0_norm_pre=l0_norm_pre, l0_norm_post=l0_norm_post, rw_mu=rw_mu, rw_w0=rw_w0, rw_w2=rw_w2,
               rw_a0=rw_a0, rw_a2=rw_a2, rw_g2=rw_g2, rw_k_k=rw_k_k, rw_k_a=rw_k_a, rw_r_k=rw_r_k,
               rw_gn_w=rw_gn_w, rw_gn_b=rw_gn_b, l1_norm_pre=l1_norm_pre, l1_norm_post=l1_norm_post,
               gm_ln_w=gm_ln_w, gm_ln_b=gm_ln_b, ffn_norm_pre=ffn_norm_pre, ffn_norm_post=ffn_norm_post,
               ffn_conv_w=ffn_conv_w, ffn_conv_b=ffn_conv_b)
    d_a = D_HEADS
    dc = l1_w_out.shape[0]
    tri = jnp.tril(jnp.ones((GMLP_CHUNK, GMLP_CHUNK), dtype=bool))
    hd = jnp.arange(D_HEADS) // HEAD_DIM
    w = dict(
        w_qkv=l0_w_in[:, :3 * d_a].astype(BF16),
        w_b=l0_w_in[:, 3 * d_a:].astype(BF16),
        w_out0=l0_w_out.astype(BF16),
        w_up=ffn_w_up.astype(BF16),
        w_down=ffn_w_down.astype(BF16),
        w_in1=l1_w_in.astype(BF16),
        w_out1=l1_w_out.astype(BF16),
        wmix=jnp.where(tri[None], gm_ws, 0.0).astype(BF16),
        gbias=jnp.repeat(gm_bs.T, dc // GMLP_GROUPS, axis=1),
        gmat=(hd[:, None] == hd[None, :]).astype(F32),
    )
    slopes = jnp.exp2(-8.0 * jnp.arange(1, N_HEADS + 1, dtype=F32) / N_HEADS)

    bp, tp, dm = x_prompt.shape
    assert bp == 1
    dt = x_prompt.dtype
    depth = ffn_w_up.shape[0]
    wkv0 = jnp.zeros((bp, N_HEADS, HEAD_DIM, HEAD_DIM), dt)
    shift0 = jnp.zeros((bp, dm), dt)
    conv0 = jnp.zeros((depth, bp, CONV_W - 1, ffn_w_up.shape[2]), dt)

    def attend_prompt(q, k, v):
        return _moba_prompt(q, k, v, slopes)

    (y_prompt, k_prompt, v_prompt, wkv_prompt, shift_prompt, _unused_chunk_v,
     ffn_conv_prompt) = _trunk(x_prompt, attend_prompt, shift0, wkv0, conv0, prm, w, tile=512, long_seq=True)

    bs, ts, _ = x_sample.shape

    def attend_sample(q, k, v):
        o = _moba_sample(q.reshape(bs, ts, -1), k.reshape(bs, ts, -1), v.reshape(bs, ts, -1), cache_k, cache_v,
                         page_table, slopes)
        return o.reshape(bs * ts, -1)

    (y_sample, k_sample, v_sample, wkv_sample, shift_sample, chunk_v_sample,
     ffn_conv_sample) = _trunk(x_sample, attend_sample, state_shift, state_wkv, state_ffn_conv, prm, w,
                               tile=bs * ts, long_seq=False)

    return (y_prompt, y_sample, k_prompt, v_prompt, k_sample, v_sample, wkv_prompt, wkv_sample,
            shift_prompt, shift_sample, chunk_v_sample, ffn_conv_prompt, ffn_conv_sample)
```

```python
import functools

import jax
import jax.numpy as jnp
from jax import lax
from jax.experimental import pallas as pl
from jax.experimental.pallas import tpu as pltpu

F32 = jnp.float32
BF16 = jnp.bfloat16
HI = lax.Precision.HIGHEST

HEAD_DIM = 64
N_HEADS = 8
D_HEADS = N_HEADS * HEAD_DIM
MOBA_BLOCK = 256
MOBA_TOPK = 3
MOBA_GROUP = 4
SCALE = HEAD_DIM ** -0.5
D_DECAY_LORA = 64
D_AAA_LORA = 64
GMLP_GROUPS = 8
GMLP_CHUNK = 128
CONV_W = 3
NORM_EPS = 1e-6
LN_EPS = 1e-5
GN_EPS = 64e-5
NEG = -1e30
WKV_CHUNK = 64
V7X_VMEM_LIMIT = 56 * 1024 * 1024


def _cparams(sem, vmem=V7X_VMEM_LIMIT):
    return pltpu.CompilerParams(dimension_semantics=sem, vmem_limit_bytes=vmem)


def _mm(a, b, prec=None):
    return jnp.dot(a, b, precision=prec, preferred_element_type=F32)


def _mm_nt(a, b, prec=None):
    return lax.dot_general(a, b, (((1,), (1,)), ((), ())), precision=prec, preferred_element_type=F32)


def _rms(x, g):
    return x * lax.rsqrt(jnp.mean(x * x, axis=-1, keepdims=True) + NORM_EPS) * g


def _norm_matmul_kernel(x_ref, g_ref, w_ref, o_ref, xn_ref):
    @pl.when(pl.program_id(1) == 0)
    def _():
        xn_ref[...] = _rms(x_ref[...], g_ref[...]).astype(BF16)

    o_ref[...] = _mm(xn_ref[...], w_ref[...])


def _norm_matmul(x, g, w, tm, tn):
    m, kd = x.shape
    n = w.shape[1]
    return pl.pallas_call(
        _norm_matmul_kernel,
        grid=(m // tm, n // tn),
        in_specs=[pl.BlockSpec((tm, kd), lambda i, j: (i, 0)),
                  pl.BlockSpec((1, kd), lambda i, j: (0, 0)),
                  pl.BlockSpec((kd, tn), lambda i, j: (0, j))],
        out_specs=pl.BlockSpec((tm, tn), lambda i, j: (i, j)),
        out_shape=jax.ShapeDtypeStruct((m, n), F32),
        scratch_shapes=[pltpu.VMEM((tm, kd), BF16)],
        compiler_params=_cparams(("parallel", "arbitrary")),
        name="norm_matmul",
    )(x, g.reshape(1, kd), w)


def _shift_rows_kernel(x_ref, g_ref, prev_ref, w_ref, xn_ref, pp_ref):
    xn_ref[...] = _rms(x_ref[...], g_ref[...])
    pp_ref[...] = _mm(prev_ref[...].astype(BF16), w_ref[...])


def _shift_rows(x_last, g, prev_xn, w_b):
    b, kd = x_last.shape
    n = w_b.shape[1]
    bp = -(-b // 8) * 8
    pad = ((0, bp - b), (0, 0))
    xn, pp = pl.pallas_call(
        _shift_rows_kernel,
        out_shape=(jax.ShapeDtypeStruct((bp, kd), F32), jax.ShapeDtypeStruct((bp, n), F32)),
        compiler_params=_cparams(None),
        name="shift_rows",
    )(jnp.pad(x_last, pad), g.reshape(1, kd), jnp.pad(prev_xn, pad), w_b)
    return xn[:b], pp[:b]


def _kmeans_kernel(k_ref, o_ref):
    k = k_ref[0]
    nb = k.shape[0] // MOBA_BLOCK
    o_ref[0] = jnp.mean(k.reshape(nb, MOBA_BLOCK, HEAD_DIM), axis=1)


def _moba_prompt_kernel(slope_ref, qT_ref, km_ref, k_ref, vT_ref, oT_ref, sel_scr):
    i = pl.program_id(1)
    nb = km_ref.shape[1]
    blk = MOBA_BLOCK
    qf = qT_ref[0]
    gate = _mm(km_ref[0], qf, HI)
    nidx = lax.broadcasted_iota(jnp.int32, gate.shape, 0)
    g = jnp.where(nidx < i, gate, -jnp.inf)
    sel = jnp.zeros(gate.shape, F32)
    for _ in range(MOBA_TOPK):
        top = jnp.max(g, axis=0, keepdims=True)
        idx = jnp.min(jnp.where(g == top, nidx, nb), axis=0, keepdims=True)
        pick = (nidx == idx) & (top > -jnp.inf)
        sel = jnp.where(pick, 1.0, sel)
        g = jnp.where(pick, -jnp.inf, g)
    sel_scr[...] = sel

    qs = (qf * SCALE).astype(BF16)
    slope = slope_ref[0]
    ki = lax.broadcasted_iota(jnp.int32, (blk, blk), 0)
    qi = lax.broadcasted_iota(jnp.int32, (blk, blk), 1)
    dm = (ki - qi).astype(F32)
    bias = slope * dm

    own = pl.multiple_of(i * blk, blk)
    s = jnp.where(dm <= 0.0, _mm(k_ref[0, pl.ds(own, blk), :], qs) + bias, NEG)
    m0 = jnp.max(s, axis=0, keepdims=True)
    p = jnp.exp(s - m0)
    l0 = jnp.sum(p, axis=0, keepdims=True)
    acc0 = _mm(vT_ref[0, :, pl.ds(own, blk)], p.astype(BF16))

    grp = MOBA_GROUP
    gk = grp * blk
    bias_g = jnp.concatenate([bias] * grp, axis=0)

    def qk(g):
        start = pl.multiple_of(jnp.minimum(g, nb // grp - 1) * gk, gk)
        return _mm(k_ref[0, pl.ds(start, gk), :], qs)

    def body(g, carry):
        m, l, acc, s_raw = carry
        s_next = qk(g + 1)
        n0 = pl.multiple_of(g * grp, grp)
        start = pl.multiple_of(g * gk, gk)
        s = s_raw + bias_g
        picked = [sel_scr[pl.ds(n0 + u, 1), :] > 0.0 for u in range(grp)]
        m_new = m
        shift = []
        for u in range(grp):
            c_u = slope * ((n0 + u - i) * blk).astype(F32)
            mb = jnp.max(s[u * blk:(u + 1) * blk], axis=0, keepdims=True) + c_u
            m_new = jnp.where(picked[u], jnp.maximum(m_new, mb), m_new)
            shift.append(c_u)
        alpha = jnp.exp(m - m_new)
        ps = []
        for u in range(grp):
            off = jnp.where(picked[u], m_new - shift[u], -NEG)
            p = jnp.exp(s[u * blk:(u + 1) * blk] - off)
            l = l + jnp.sum(p, axis=0, keepdims=True) if u else alpha * l + jnp.sum(p, axis=0, keepdims=True)
            ps.append(p.astype(BF16))
        acc = alpha * acc + _mm(vT_ref[0, :, pl.ds(start, gk)], jnp.concatenate(ps, axis=0))
        return m_new, l, acc, s_next

    _, l, acc, _ = lax.fori_loop(0, (i + grp - 1) // grp, body, (m0, l0, acc0, qk(0)))
    oT_ref[0] = acc / l


def _moba_prompt(q, k, v, slopes):
    t = q.shape[0]
    assert t % (MOBA_BLOCK * MOBA_GROUP) == 0
    nb = t // MOBA_BLOCK

    def heads_t(z):
        return z.reshape(t, N_HEADS, HEAD_DIM).transpose(1, 2, 0)

    k_hm = k.reshape(t, N_HEADS, HEAD_DIM).transpose(1, 0, 2)
    kmean = pl.pallas_call(
        _kmeans_kernel,
        grid=(N_HEADS,),
        in_specs=[pl.BlockSpec((1, t, HEAD_DIM), lambda h: (h, 0, 0))],
        out_specs=pl.BlockSpec((1, nb, HEAD_DIM), lambda h: (h, 0, 0)),
        out_shape=jax.ShapeDtypeStruct((N_HEADS, nb, HEAD_DIM), F32),
        compiler_params=_cparams(("parallel",)),
        name="moba_kmeans",
    )(k_hm)
    oT = pl.pallas_call(
        _moba_prompt_kernel,
        grid=(N_HEADS, nb),
        in_specs=[pl.BlockSpec((1, 1, 1), lambda h, i: (h, 0, 0)),
                  pl.BlockSpec((1, HEAD_DIM, MOBA_BLOCK), lambda h, i: (h, 0, i)),
                  pl.BlockSpec((1, nb, HEAD_DIM), lambda h, i: (h, 0, 0)),
                  pl.BlockSpec((1, t, HEAD_DIM), lambda h, i: (h, 0, 0)),
                  pl.BlockSpec((1, HEAD_DIM, t), lambda h, i: (h, 0, 0))],
        out_specs=pl.BlockSpec((1, HEAD_DIM, MOBA_BLOCK), lambda h, i: (h, 0, i)),
        out_shape=jax.ShapeDtypeStruct((N_HEADS, HEAD_DIM, t), F32),
        scratch_shapes=[pltpu.VMEM((nb, MOBA_BLOCK), F32)],
        compiler_params=_cparams(("parallel", "arbitrary")),
        name="moba_prompt",
    )(slopes.reshape(N_HEADS, 1, 1), heads_t(q), kmean, k_hm.astype(BF16), heads_t(v).astype(BF16))
    return oT.transpose(2, 0, 1).reshape(t, D_HEADS)


GATE_PAGES = 16


def _moba_gate_kernel(pt_ref, q_ref, ck_ref, sel_ref, kbuf, km_scr, sem, *, n_pages):
    i = pl.program_id(0)
    nsteps = pl.num_programs(0)
    steps_per_seq = n_pages // GATE_PAGES
    g = i % steps_per_seq
    blk_pages = MOBA_BLOCK // ck_ref.shape[3]
    nb = n_pages // blk_pages

    def copies(step, slot):
        return [pltpu.make_async_copy(ck_ref.at[pt_ref[step * GATE_PAGES + c]], kbuf.at[slot, c], sem.at[slot])
                for c in range(GATE_PAGES)]

    @pl.when(i == 0)
    def _():
        for cp in copies(0, 0):
            cp.start()

    @pl.when(i + 1 < nsteps)
    def _():
        for cp in copies(i + 1, (i + 1) % 2):
            cp.start()

    slot = i % 2
    for cp in copies(i, slot):
        cp.wait()

    @pl.when(g == 0)
    def _():
        km_scr[...] = jnp.zeros(km_scr.shape, F32)

    lane = lax.broadcasted_iota(jnp.int32, km_scr.shape, 2)
    acc = km_scr[...]
    for b in range(GATE_PAGES // blk_pages):
        x = kbuf[slot, blk_pages * b]
        for j in range(1, blk_pages):
            x = x + kbuf[slot, blk_pages * b + j]
        acc = jnp.where(lane == g * (GATE_PAGES // blk_pages) + b, jnp.sum(x, axis=-1, keepdims=True), acc)
    km_scr[...] = acc

    @pl.when(g == steps_per_seq - 1)
    def _():
        km = acc * (1.0 / MOBA_BLOCK)
        gate = jnp.concatenate([_mm(q_ref[0, h], km[h], HI) for h in range(N_HEADS)], axis=0)
        bidx = lax.broadcasted_iota(jnp.int32, gate.shape, 1)
        gt = jnp.where(bidx < nb, gate, -jnp.inf)
        sel = jnp.zeros(gate.shape, F32)
        for _ in range(MOBA_TOPK):
            top = jnp.max(gt, axis=1, keepdims=True)
            idx = jnp.min(jnp.where(gt == top, bidx, nb), axis=1, keepdims=True)
            pick = (bidx == idx) & (top > -jnp.inf)
            sel = jnp.where(pick, 1.0, sel)
            gt = jnp.where(pick, -jnp.inf, gt)
        sel_ref[0] = sel


def _moba_attend_kernel(pg_ref, slope_ref, q_ref, kn_ref, vn_ref, dist_ref, ck_ref, cv_ref, o_ref,
                        kbuf, vbuf, sem, *, n_visits):
    i = pl.program_id(0)
    nsteps = pl.num_programs(0)
    page = ck_ref.shape[3]
    n_copies = n_visits * (MOBA_BLOCK // page)

    def copies(step, slot):
        h = step % N_HEADS
        out = []
        for c in range(n_copies):
            pg = pg_ref[step * n_copies + c]
            cols = pl.ds(c * page, page)
            out.append(pltpu.make_async_copy(ck_ref.at[pg, h], kbuf.at[slot, :, cols], sem.at[0, slot]))
            out.append(pltpu.make_async_copy(cv_ref.at[pg, h], vbuf.at[slot, :, cols], sem.at[1, slot]))
        return out

    @pl.when(i == 0)
    def _():
        for cp in copies(0, 0):
            cp.start()

    @pl.when(i + 1 < nsteps)
    def _():
        for cp in copies(i + 1, (i + 1) % 2):
            cp.start()

    slot = i % 2
    for cp in copies(i, slot):
        cp.wait()

    nq = q_ref.shape[2]
    slope = slope_ref[0]
    qs = (q_ref[0, 0] * SCALE).astype(BF16)
    ncol = n_visits * MOBA_BLOCK
    s = _mm(qs, kbuf[slot].astype(BF16)) - slope * dist_ref[0, 0]
    row = lax.broadcasted_iota(jnp.int32, (nq, ncol), 0)
    col = lax.broadcasted_iota(jnp.int32, (nq, ncol), 1)
    s = jnp.where(col // (ncol // nq) == row, s, NEG)
    nk = kn_ref.shape[2]
    dist_o = (lax.broadcasted_iota(jnp.int32, (nq, nk), 0)
              - lax.broadcasted_iota(jnp.int32, (nq, nk), 1)).astype(F32)
    s_o = jnp.where(dist_o >= 0.0, _mm_nt(qs, kn_ref[0, 0].astype(BF16)) - slope * dist_o, NEG)
    m = jnp.maximum(jnp.max(s, axis=1, keepdims=True), jnp.max(s_o, axis=1, keepdims=True))
    p = jnp.exp(s - m)
    p_o = jnp.exp(s_o - m)
    l = jnp.sum(p, axis=1, keepdims=True) + jnp.sum(p_o, axis=1, keepdims=True)
    acc = _mm_nt(p.astype(BF16), vbuf[slot].astype(BF16)) + _mm(p_o.astype(BF16), vn_ref[0, 0].astype(BF16))
    o_ref[0, 0] = acc / l


def _moba_sample(q, k, v, cache_k, cache_v, page_table, slopes):
    b, s, _ = q.shape
    n_pool, page, _, _ = cache_k.shape
    n_pages = page_table.shape[1]
    past = n_pages * page
    pages_per_blk = MOBA_BLOCK // page
    assert pages_per_blk == 2 and past % MOBA_BLOCK == 0
    nb = past // MOBA_BLOCK
    assert nb >= MOBA_TOPK
    pt = page_table.reshape(-1).astype(jnp.int32)

    ck = cache_k.transpose(0, 2, 3, 1)
    cv = cache_v.transpose(0, 2, 3, 1)

    nk = 16
    lanes = 128
    assert n_pages % GATE_PAGES == 0 and nb <= lanes
    steps_per_seq = n_pages // GATE_PAGES

    def heads(z, rows):
        z = jnp.pad(z, ((0, 0), (0, rows - s), (0, 0)))
        return z.reshape(b, rows, N_HEADS, HEAD_DIM).transpose(0, 2, 1, 3)

    sel = pl.pallas_call(
        functools.partial(_moba_gate_kernel, n_pages=n_pages),
        grid_spec=pltpu.PrefetchScalarGridSpec(
            num_scalar_prefetch=1,
            grid=(b * steps_per_seq,),
            in_specs=[pl.BlockSpec((1, N_HEADS, s, HEAD_DIM), lambda i, pt: (i // steps_per_seq, 0, 0, 0)),
                      pl.BlockSpec(memory_space=pl.ANY)],
            out_specs=pl.BlockSpec((1, N_HEADS * s, lanes), lambda i, pt: (i // steps_per_seq, 0, 0)),
            scratch_shapes=[pltpu.VMEM((2, GATE_PAGES, N_HEADS, HEAD_DIM, page), F32),
                            pltpu.VMEM((N_HEADS, HEAD_DIM, lanes), F32),
                            pltpu.SemaphoreType.DMA((2,))]),
        out_shape=jax.ShapeDtypeStruct((b, N_HEADS * s, lanes), F32),
        compiler_params=_cparams(("arbitrary",)),
        name="moba_gate",
    )(pt, heads(q, s), ck)

    n_visits = s * MOBA_TOPK
    blk = jnp.argsort(-sel[:, :, :nb], axis=2, stable=True)[:, :, :MOBA_TOPK]
    blk = blk.reshape(b, N_HEADS, n_visits)
    pages = page_table[jnp.arange(b)[:, None, None, None],
                       blk[..., None] * pages_per_blk + jnp.arange(pages_per_blk)]
    kpos = blk[..., None] * MOBA_BLOCK + jnp.arange(MOBA_BLOCK)
    qpos = past + jnp.arange(n_visits) // MOBA_TOPK
    dist = (qpos[None, None, :, None] - kpos).astype(F32).reshape(b, N_HEADS, 1, n_visits * MOBA_BLOCK)

    def bh_spec(r, w):
        return pl.BlockSpec((1, 1, r, w), lambda i, pg: (i // N_HEADS, i % N_HEADS, 0, 0))

    out = pl.pallas_call(
        functools.partial(_moba_attend_kernel, n_visits=n_visits),
        grid_spec=pltpu.PrefetchScalarGridSpec(
            num_scalar_prefetch=1,
            grid=(b * N_HEADS,),
            in_specs=[pl.BlockSpec((1, 1, 1), lambda i, pg: (i % N_HEADS, 0, 0)),
                      bh_spec(s, HEAD_DIM), bh_spec(nk, HEAD_DIM), bh_spec(nk, HEAD_DIM),
                      bh_spec(1, n_visits * MOBA_BLOCK),
                      pl.BlockSpec(memory_space=pl.ANY), pl.BlockSpec(memory_space=pl.ANY)],
            out_specs=bh_spec(s, HEAD_DIM),
            scratch_shapes=[pltpu.VMEM((2, HEAD_DIM, n_visits * MOBA_BLOCK), F32),
                            pltpu.VMEM((2, HEAD_DIM, n_visits * MOBA_BLOCK), F32),
                            pltpu.SemaphoreType.DMA((2, 2))]),
        out_shape=jax.ShapeDtypeStruct((b, N_HEADS, s, HEAD_DIM), F32),
        compiler_params=_cparams(("arbitrary",)),
        name="moba_attend",
    )(pages.reshape(-1).astype(jnp.int32), slopes.reshape(N_HEADS, 1, 1), heads(q, s), heads(k, nk), heads(v, nk),
      dist, ck, cv)
    return out.transpose(0, 2, 1, 3).reshape(b, s, D_HEADS)


def _softplus(z):
    return jnp.maximum(z, 0.0) + jnp.log1p(jnp.exp(-jnp.abs(z)))


def _rwkv_prep_kernel(pb_ref, first_ref, mu_ref, w0_ref, w2_ref, a0_ref, a2_ref, g2_ref, kk_ref, ka_ref,
                      rk_ref, gm_ref, r_o, lw_o, k_o, v_o, kk_o, b_o, g_o, bonus_o, *, seq_len):
    pb = pb_ref[...]
    tm = pb.shape[0]
    row = lax.broadcasted_iota(jnp.int32, (tm, 1), 0)
    if seq_len is None:
        starts, first = row == 0, first_ref[0]
    else:
        starts, first = (row % seq_len) == 0, first_ref[...]
    prev = jnp.where(starts, first, pltpu.roll(pb, 1, axis=0))
    xm = pb + mu_ref[...] * (prev - pb)
    d = D_HEADS
    o1, o2, o3 = 3 * d, 3 * d + D_DECAY_LORA, 3 * d + D_DECAY_LORA + D_AAA_LORA
    r, k, v = xm[:, 0:d], xm[:, d:2 * d], xm[:, 2 * d:o1]
    xw, xa, xg = xm[:, o1:o2], xm[:, o2:o3], xm[:, o3:]
    logw = -_softplus(-(w0_ref[...] + _mm(jnp.tanh(xw), w2_ref[...]))) - 0.5
    a = jax.nn.sigmoid(a0_ref[...] + _mm(xa, a2_ref[...]))
    gmat = gm_ref[...]
    kk = k * kk_ref[...]
    kk = kk / jnp.maximum(jnp.sqrt(_mm(kk * kk, gmat, HI)), 1e-12)
    k = k * (1.0 + (a - 1.0) * ka_ref[...])
    r_o[...] = r
    lw_o[...] = -jnp.exp(logw)
    k_o[...] = k
    v_o[...] = v
    kk_o[...] = kk
    b_o[...] = kk * a
    g_o[...] = _mm(jax.nn.sigmoid(xg), g2_ref[...])
    bonus_o[...] = _mm(r * k * rk_ref[...], gmat, HI) * v


def _rwkv_prep(pb, first, prm, gmat, tm, seq_len):
    m, n = pb.shape
    d = D_HEADS
    if seq_len is None:
        first_spec = pl.BlockSpec((1, 1, n), lambda i: (i, 0, 0))
    else:
        first_spec = pl.BlockSpec((tm, n), lambda i: (i, 0))

    def full(shape):
        return pl.BlockSpec(shape, lambda i: (0,) * len(shape))

    def vec(z):
        return z.reshape(1, -1)

    outs = pl.pallas_call(
        functools.partial(_rwkv_prep_kernel, seq_len=seq_len),
        grid=(m // tm,),
        in_specs=[pl.BlockSpec((tm, n), lambda i: (i, 0)), first_spec,
                  full((1, n)), full((1, d)), full((D_DECAY_LORA, d)), full((1, d)), full((D_AAA_LORA, d)),
                  full((n - 3 * d - D_DECAY_LORA - D_AAA_LORA, d)), full((1, d)), full((1, d)), full((1, d)),
                  full((d, d))],
        out_specs=[pl.BlockSpec((tm, d), lambda i: (i, 0))] * 8,
        out_shape=[jax.ShapeDtypeStruct((m, d), F32)] * 8,
        compiler_params=_cparams(("parallel",)),
        name="rwkv_prep",
    )(pb, first, vec(prm['rw_mu']), vec(prm['rw_w0']), prm['rw_w2'], vec(prm['rw_a0']), prm['rw_a2'],
      prm['rw_g2'], vec(prm['rw_k_k']), vec(prm['rw_k_a']), vec(prm['rw_r_k']), gmat)
    return outs


def _split3(x):
    p1 = x.astype(BF16)
    r1 = x - p1.astype(F32)
    p2 = r1.astype(BF16)
    p3 = (r1 - p2.astype(F32)).astype(BF16)
    return p1, p2, p3


def _mmb(a, b):
    return _mm(a.astype(BF16), b.astype(BF16))


def _mmb_nt(a, b):
    return _mm_nt(a.astype(BF16), b.astype(BF16))


def _rwkv_scan_kernel(r_ref, lw_ref, k_ref, v_ref, kk_ref, b_ref, kkT_ref, lwT_ref, vT_ref, s0_ref,
                      y_ref, sfin_ref, s_scr, *, chunk, chunks_per_step):
    j = pl.program_id(1)

    @pl.when(j == 0)
    def _():
        s_scr[...] = s0_ref[0]

    step = chunk * chunks_per_step
    d = D_HEADS
    si = lax.broadcasted_iota(jnp.int32, (step, step), 0)
    sj = lax.broadcasted_iota(jnp.int32, (step, step), 1)
    same = (si // chunk) == (sj // chunk)
    lbd = jnp.where(same & (si >= sj), 1.0, 0.0).astype(BF16)
    ubd = jnp.where(same & (si <= sj), 1.0, 0.0).astype(BF16)
    lw = lw_ref[...]
    lwT = lwT_ref[0]
    cl = sum(_mm(lbd, p) for p in _split3(lw))
    clT = sum(_mm(p, ubd) for p in _split3(lwT))
    tots = [cl[(c + 1) * chunk - 1:(c + 1) * chunk, :] for c in range(chunks_per_step)]
    tot_full = jnp.concatenate([jnp.broadcast_to(t, (chunk, d)) for t in tots], axis=0)
    e_neg = jnp.exp(-cl)
    e_end = jnp.exp(tot_full - cl)
    b = b_ref[...]
    k = k_ref[...]
    rho = r_ref[...] * jnp.exp(cl)
    beta, kap = b * e_neg, k * e_neg
    b_end, k_end = (b * e_end).astype(BF16), (k * e_end).astype(BF16)
    alphaT = kkT_ref[0] * jnp.exp(clT - lwT)
    vT = vT_ref[0].astype(BF16)
    v = v_ref[...].astype(BF16)

    ii = lax.broadcasted_iota(jnp.int32, (chunk, chunk), 0)
    jj = lax.broadcasted_iota(jnp.int32, (chunk, chunk), 1)
    low, upper = ii >= jj, ii < jj
    hd = HEAD_DIM
    units = [(slice(c * chunk, (c + 1) * chunk), slice(h * hd, (h + 1) * hd))
             for c in range(chunks_per_step) for h in range(N_HEADS)]
    bk = [jnp.concatenate([beta[rs, cs], kap[rs, cs]], axis=0).astype(BF16) for rs, cs in units]
    aT = [alphaT[cs, rs] for rs, cs in units]
    g1 = [_mm(x, a.astype(BF16)) for x, a in zip(bk, aT)]
    lt = [jnp.where(upper, g[:chunk], 0.0).astype(BF16) for g in g1]
    akT = [jnp.where(upper, g[chunk:], 0.0).astype(BF16) for g in g1]
    g2 = [_mm_nt(rho[rs, cs].astype(BF16), x) for (rs, cs), x in zip(units, bk)]
    arb = [jnp.where(low, g[:, :chunk], 0.0).astype(BF16) for g in g2]
    ark = [jnp.where(low, g[:, chunk:], 0.0).astype(BF16) for g in g2]
    z = [jnp.concatenate([a, _mm(vT[cs, rs], x)], axis=0) for (rs, cs), a, x in zip(units, aT, akT)]
    z = [x - _mm(x.astype(BF16), l) for x, l in zip(z, lt)]
    pw = lt
    n = 2
    while n < chunk:
        pw = [_mm(p, p).astype(BF16) for p in pw]
        z = [x + _mm(x.astype(BF16), p) for x, p in zip(z, pw)]
        n *= 2
    pz = [(-x).astype(BF16) for x in z]
    g3 = [_mm_nt(a, p) for a, p in zip(arb, pz)]
    g4 = [_mm(p, b_end[rs, cs]) for (rs, cs), p in zip(units, pz)]
    yin = [g[:, hd:] + _mm(a, v[rs, cs]) for (rs, cs), g, a in zip(units, g3, ark)]
    vk = [_mm(vT[cs, rs], k_end[rs, cs]) for rs, cs in units]
    state = [s_scr[h] for h in range(N_HEADS)]
    for c in range(chunks_per_step):
        etot = jnp.exp(tots[c])
        for h in range(N_HEADS):
            u = c * N_HEADS + h
            rs, cs = units[u]
            s = state[h]
            sb = s.astype(BF16)
            y_ref[rs, cs] = _mm_nt((rho[rs, cs] + g3[u][:, :hd]).astype(BF16), sb) + yin[u]
            state[h] = s * etot[:, cs] + _mm(sb, g4[u][:hd].astype(BF16)) + g4[u][hd:] + vk[u]
    for h in range(N_HEADS):
        s_scr[h] = state[h]

    @pl.when(j == pl.num_programs(1) - 1)
    def _():
        sfin_ref[0] = s_scr[...]


def _rwkv_scan(r, lw, k, v, kk, b, s0, seq_len, chunks_per_step):
    m, d = r.shape
    nseq = m // seq_len
    step = WKV_CHUNK * chunks_per_step
    nsteps = seq_len // step

    def tr(z):
        return z.reshape(nseq, seq_len, d).transpose(0, 2, 1)

    tm_spec = pl.BlockSpec((step, d), lambda i, j: (i * nsteps + j, 0))
    fm_spec = pl.BlockSpec((1, d, step), lambda i, j: (i, 0, j))
    st_spec = pl.BlockSpec((1, N_HEADS, HEAD_DIM, HEAD_DIM), lambda i, j: (i, 0, 0, 0))
    return pl.pallas_call(
        functools.partial(_rwkv_scan_kernel, chunk=WKV_CHUNK, chunks_per_step=chunks_per_step),
        grid=(nseq, nsteps),
        in_specs=[tm_spec] * 6 + [fm_spec] * 3 + [st_spec],
        out_specs=[tm_spec, st_spec],
        out_shape=[jax.ShapeDtypeStruct((m, d), F32),
                   jax.ShapeDtypeStruct((nseq, N_HEADS, HEAD_DIM, HEAD_DIM), F32)],
        scratch_shapes=[pltpu.VMEM((N_HEADS, HEAD_DIM, HEAD_DIM), F32)],
        compiler_params=_cparams(("parallel", "arbitrary")),
        name="rwkv_scan",
    )(r, lw, k, v, kk, b, tr(kk), tr(lw), tr(v), s0)


def _l0_out_kernel(y_ref, bonus_ref, g_ref, oa_ref, h_ref, gnw_ref, gnb_ref, gm_ref, wa_ref, wb_ref,
                   gpost_ref, o_ref):
    y = y_ref[...]
    gmat = gm_ref[...]
    inv = 1.0 / HEAD_DIM
    yc = y - _mm(y, gmat, HI) * inv
    yn = yc * lax.rsqrt(_mm(yc * yc, gmat, HI) * inv + GN_EPS) * gnw_ref[...] + gnb_ref[...]
    ob = (yn + bonus_ref[...]) * g_ref[...]
    mix = _mm(oa_ref[...].astype(BF16), wa_ref[...]) + _mm(ob.astype(BF16), wb_ref[...])
    o_ref[...] = h_ref[...] + _rms(mix, gpost_ref[...])


def _l0_out(y, bonus, g, oa, h, prm, gmat, w_out, tm):
    m, dm = h.shape
    d = D_HEADS

    def rowblk(w):
        return pl.BlockSpec((tm, w), lambda i: (i, 0))

    def full(shape):
        return pl.BlockSpec(shape, lambda i: (0,) * len(shape))

    return pl.pallas_call(
        _l0_out_kernel,
        grid=(m // tm,),
        in_specs=[rowblk(d), rowblk(d), rowblk(d), rowblk(d), rowblk(dm), full((1, d)), full((1, d)),
                  full((d, d)), full((d, dm)), full((d, dm)), full((1, dm))],
        out_specs=rowblk(dm),
        out_shape=jax.ShapeDtypeStruct((m, dm), F32),
        compiler_params=_cparams(("parallel",)),
        name="l0_out",
    )(y, bonus, g, oa, h, prm['rw_gn_w'].reshape(1, d), prm['rw_gn_b'].reshape(1, d), gmat,
      w_out[:d], w_out[d:], prm['l0_norm_post'].reshape(1, dm))


FFN_COLS = 256


def _ffn_down_kernel(up_ref, pa_ref, pb_ref, cw_ref, cb_ref, wd_ref, h_ref, gpost_ref, o_ref, *, seq_len):
    i = pl.program_id(0)
    tm = up_ref.shape[0]
    dff = wd_ref.shape[0]
    row = lax.broadcasted_iota(jnp.int32, (tm, 1), 0)
    pos = row if seq_len is None else row % seq_len

    def conv(cols):
        u = up_ref[:, cols]
        if seq_len is None:
            p0 = jnp.where(i == 0, pb_ref[6:7, cols], pa_ref[6:7, cols])
            p1 = jnp.where(i == 0, pb_ref[7:8, cols], pa_ref[7:8, cols])
        else:
            p0, p1 = pa_ref[:, cols], pb_ref[:, cols]
        m1 = jnp.where(pos == 0, p1, pltpu.roll(u, 1, axis=0))
        m2 = jnp.where(pos == 0, p0, jnp.where(pos == 1, p1, pltpu.roll(u, 2, axis=0)))
        return cb_ref[:, cols] + cw_ref[0:1, cols] * m2 + cw_ref[1:2, cols] * m1 + cw_ref[2:3, cols] * u

    acc = jnp.zeros(o_ref.shape, F32)
    for c in range(dff // FFN_COLS):
        gate = conv(slice(c * FFN_COLS, (c + 1) * FFN_COLS))
        val = conv(slice(dff + c * FFN_COLS, dff + (c + 1) * FFN_COLS))
        act = (jax.nn.gelu(gate) * val).astype(BF16)
        acc = acc + _mm(act, wd_ref[c * FFN_COLS:(c + 1) * FFN_COLS, :])
    o_ref[...] = h_ref[...] + _rms(acc, gpost_ref[...])


def _ffn_down(up, prev_rows, conv_w, conv_b, w_down, h, g_post, tm, seq_len):
    m, n2 = up.shape
    dff, dm = w_down.shape
    assert dff % FFN_COLS == 0
    if seq_len is None:
        pa, pb = up, jnp.pad(prev_rows[0], ((6, 0), (0, 0)))
        pa_spec = pl.BlockSpec((8, n2), lambda i: (jnp.maximum(i * (tm // 8) - 1, 0), 0))
        pb_spec = pl.BlockSpec((8, n2), lambda i: (0, 0))
    else:
        pa = jnp.repeat(prev_rows[:, 0], seq_len, axis=0)
        pb = jnp.repeat(prev_rows[:, 1], seq_len, axis=0)
        pa_spec = pb_spec = pl.BlockSpec((tm, n2), lambda i: (i, 0))

    def full(shape):
        return pl.BlockSpec(shape, lambda i: (0,) * len(shape))

    return pl.pallas_call(
        functools.partial(_ffn_down_kernel, seq_len=seq_len),
        grid=(m // tm,),
        in_specs=[pl.BlockSpec((tm, n2), lambda i: (i, 0)), pa_spec, pb_spec, full((CONV_W, n2)),
                  full((1, n2)), full((dff, dm)), pl.BlockSpec((tm, dm), lambda i: (i, 0)), full((1, dm))],
        out_specs=pl.BlockSpec((tm, dm), lambda i: (i, 0)),
        out_shape=jax.ShapeDtypeStruct((m, dm), F32),
        compiler_params=_cparams(("parallel",)),
        name="ffn_down",
    )(up, pa, pb, conv_w, conv_b.reshape(1, n2), w_down, h, g_post.reshape(1, dm))


def _gmlp_kernel(h_ref, gpre_ref, win_ref, lnw_ref, lnb_ref, wmix_ref, bias_ref, wout_ref, gpost_ref,
                 o_ref, cv_ref, gated_scr):
    h = h_ref[...]
    tm = h.shape[0]
    dc = wout_ref.shape[0]
    rm = wmix_ref.shape[1]
    gw = dc // GMLP_GROUPS
    z = jax.nn.gelu(_mm(_rms(h, gpre_ref[...]).astype(BF16), win_ref[...]))
    v = z[:, dc:]
    vc = v - jnp.mean(v, axis=-1, keepdims=True)
    vn = vc * lax.rsqrt(jnp.mean(vc * vc, axis=-1, keepdims=True) + LN_EPS) * lnw_ref[...] + lnb_ref[...]
    cv_ref[...] = vn
    for s in range(tm // rm):
        rows = slice(s * rm, (s + 1) * rm)
        for g in range(GMLP_GROUPS):
            cols = slice(g * gw, (g + 1) * gw)
            mixed = _mm(wmix_ref[g], vn[rows, cols].astype(BF16)) + bias_ref[:, cols]
            gated_scr[rows, cols] = (z[rows, cols] * mixed).astype(BF16)
    o_ref[...] = h + _rms(_mm(gated_scr[...], wout_ref[...]), gpost_ref[...])


def _gmlp(h, g_pre, w_in, ln_w, ln_b, wmix, bias, w_out, g_post, tm):
    m, dm = h.shape
    dc = w_out.shape[0]
    rm = wmix.shape[1]

    def full(shape):
        return pl.BlockSpec(shape, lambda i: (0,) * len(shape))

    return pl.pallas_call(
        _gmlp_kernel,
        grid=(m // tm,),
        in_specs=[pl.BlockSpec((tm, dm), lambda i: (i, 0)), full((1, dm)), full((dm, 2 * dc)), full((1, dc)),
                  full((1, dc)), full((GMLP_GROUPS, rm, rm)), full((rm, dc)), full((dc, dm)), full((1, dm))],
        out_specs=[pl.BlockSpec((tm, dm), lambda i: (i, 0)), pl.BlockSpec((tm, dc), lambda i: (i, 0))],
        out_shape=[jax.ShapeDtypeStruct((m, dm), F32), jax.ShapeDtypeStruct((m, dc), F32)],
        scratch_shapes=[pltpu.VMEM((tm, dc), BF16)],
        compiler_params=_cparams(("parallel",)),
        name="gmlp",
    )(h, g_pre.reshape(1, dm), w_in, ln_w.reshape(1, dc), ln_b.reshape(1, dc), wmix, bias, w_out,
      g_post.reshape(1, dm))


def _trunk(x, attend, prev_xn, wkv0, conv0, prm, w, *, tile, long_seq):
    nseq, t, dm = x.shape
    m = nseq * t
    h = x.reshape(m, dm)
    seq_len = None if long_seq else t

    g0 = prm['l0_norm_pre']
    qkv = _norm_matmul(h, g0, w['w_qkv'], tile, 512)
    pb = _norm_matmul(h, g0, w['w_b'], tile, 256)
    shift_new, prev_proj = _shift_rows(x[:, -1], g0, prev_xn, w['w_b'])
    q, k_new, v_new = qkv[:, :D_HEADS], qkv[:, D_HEADS:2 * D_HEADS], qkv[:, 2 * D_HEADS:]
    if long_seq:
        first = jnp.concatenate([prev_proj, pb[tile - 1:m - 1:tile]], axis=0).reshape(m // tile, 1, -1)
    else:
        first = jnp.repeat(prev_proj, t, axis=0)
    r, lw, kmod, vv, kk, bb, gate, bonus = _rwkv_prep(pb, first, prm, w['gmat'], tile, seq_len)
    o_a = attend(q, k_new, v_new)

    t_pad = -(-t // WKV_CHUNK) * WKV_CHUNK

    def padt(z):
        if t_pad == t:
            return z
        return jnp.pad(z.reshape(nseq, t, -1), ((0, 0), (0, t_pad - t), (0, 0))).reshape(nseq * t_pad, -1)

    y, wkv_new = _rwkv_scan(padt(r), padt(lw), padt(kmod), padt(vv), padt(kk), padt(bb), wkv0, t_pad,
                            2 if t_pad % (2 * WKV_CHUNK) == 0 else 1)
    if t_pad != t:
        y = y.reshape(nseq, t_pad, -1)[:, :t].reshape(m, -1)
    h = _l0_out(y, bonus, gate, o_a, h, prm, w['gmat'], w['w_out0'], tile)

    conv_rows = []

    def ffn(h, layer):
        up = _norm_matmul(h, prm['ffn_norm_pre'][layer], w['w_up'][layer], tile, 512)
        conv_rows.append(up.reshape(nseq, t, -1)[:, t - (CONV_W - 1):])
        return _ffn_down(up, conv0[layer], prm['ffn_conv_w'][layer], prm['ffn_conv_b'][layer],
                         w['w_down'][layer], h, prm['ffn_norm_post'][layer], min(tile, 256), seq_len)

    h = ffn(h, 0)

    if long_seq:
        wmix, bias = w['wmix'], w['gbias']
    else:
        eye = jnp.eye(tile // t, dtype=F32)
        wmix = jnp.einsum('ab,gij->gaibj', eye, w['wmix'][:, :t, :t].astype(F32))
        wmix = wmix.reshape(GMLP_GROUPS, tile, tile).astype(BF16)
        bias = jnp.tile(w['gbias'][:t], (tile // t, 1))
    h, chunk_v = _gmlp(h, prm['l1_norm_pre'], w['w_in1'], prm['gm_ln_w'], prm['gm_ln_b'], wmix, bias,
                       w['w_out1'], prm['l1_norm_post'], tile)
    h = ffn(h, 1)
    return (h.reshape(nseq, t, dm), k_new.reshape(nseq, t, N_HEADS, HEAD_DIM),
            v_new.reshape(nseq, t, N_HEADS, HEAD_DIM), wkv_new, shift_new, chunk_v.reshape(nseq, t, -1),
            jnp.stack(conv_rows))


def kernel(x_prompt, x_sample, cache_k, cache_v, state_wkv, state_shift, state_ffn_conv, page_table,
           l0_norm_pre, l0_norm_post, l0_w_in, rw_mu, rw_w0, rw_w2, rw_a0, rw_a2, rw_g2, rw_k_k, rw_k_a,
           rw_r_k, rw_gn_w, rw_gn_b, l0_w_out, l1_norm_pre, l1_norm_post, l1_w_in, gm_ln_w, gm_ln_b,
           gm_ws, gm_bs, l1_w_out, ffn_norm_pre, ffn_norm_post, ffn_w_up, ffn_conv_w, ffn_conv_b,
           ffn_w_down):
    prm = dict(l0_norm_pre=l0_norm_pre, l0_norm_post=l0_norm_post, rw_mu=rw_mu, rw_w0=rw_w0, rw_w2=rw_w2,
               rw_a0=rw_a0, rw_a2=rw_a2, rw_g2=rw_g2, rw_k_k=rw_k_k, rw_k_a=rw_k_a, rw_r_k=rw_r_k,
               rw_gn_w=rw_gn_w, rw_gn_b=rw_gn_b, l1_norm_pre=l1_norm_pre, l1_norm_post=l1_norm_post,
               gm_ln_w=gm_ln_w, gm_ln_b=gm_ln_b, ffn_norm_pre=ffn_norm_pre, ffn_norm_post=ffn_norm_post,
               ffn_conv_w=ffn_conv_w, ffn_conv_b=ffn_conv_b)
    d_a = D_HEADS
    dc = l1_w_out.shape[0]
    tri = jnp.tril(jnp.ones((GMLP_CHUNK, GMLP_CHUNK), dtype=bool))
    hd = jnp.arange(D_HEADS) // HEAD_DIM
    w = dict(
        w_qkv=l0_w_in[:, :3 * d_a].astype(BF16),
        w_b=l0_w_in[:, 3 * d_a:].astype(BF16),
        w_out0=l0_w_out.astype(BF16),
        w_up=ffn_w_up.astype(BF16),
        w_down=ffn_w_down.astype(BF16),
        w_in1=l1_w_in.astype(BF16),
        w_out1=l1_w_out.astype(BF16),
        wmix=jnp.where(tri[None], gm_ws, 0.0).astype(BF16),
        gbias=jnp.repeat(gm_bs.T, dc // GMLP_GROUPS, axis=1),
        gmat=(hd[:, None] == hd[None, :]).astype(F32),
    )
    slopes = jnp.exp2(-8.0 * jnp.arange(1, N_HEADS + 1, dtype=F32) / N_HEADS)

    bp, tp, dm = x_prompt.shape
    assert bp == 1
    dt = x_prompt.dtype
    depth = ffn_w_up.shape[0]
    wkv0 = jnp.zeros((bp, N_HEADS, HEAD_DIM, HEAD_DIM), dt)
    shift0 = jnp.zeros((bp, dm), dt)
    conv0 = jnp.zeros((depth, bp, CONV_W - 1, ffn_w_up.shape[2]), dt)

    def attend_prompt(q, k, v):
        return _moba_prompt(q, k, v, slopes)

    (y_prompt, k_prompt, v_prompt, wkv_prompt, shift_prompt, _unused_chunk_v,
     ffn_conv_prompt) = _trunk(x_prompt, attend_prompt, shift0, wkv0, conv0, prm, w, tile=512, long_seq=True)

    bs, ts, _ = x_sample.shape

    def attend_sample(q, k, v):
        o = _moba_sample(q.reshape(bs, ts, -1), k.reshape(bs, ts, -1), v.reshape(bs, ts, -1), cache_k, cache_v,
                         page_table, slopes)
        return o.reshape(bs * ts, -1)

    (y_sample, k_sample, v_sample, wkv_sample, shift_sample, chunk_v_sample,
     ffn_conv_sample) = _trunk(x_sample, attend_sample, state_shift, state_wkv, state_ffn_conv, prm, w,
                               tile=bs * ts, long_seq=False)

    return (y_prompt, y_sample, k_prompt, v_prompt, k_sample, v_sample, wkv_prompt, wkv_sample,
            shift_prompt, shift_sample, chunk_v_sample, ffn_conv_prompt, ffn_conv_sample)
```

```python
import functools

import jax
import jax.numpy as jnp
from jax import lax
from jax.experimental import pallas as pl
from jax.experimental.pallas import tpu as pltpu

F32 = jnp.float32
BF16 = jnp.bfloat16
HI = lax.Precision.HIGHEST

HEAD_DIM = 64
N_HEADS = 8
D_HEADS = N_HEADS * HEAD_DIM
MOBA_BLOCK = 256
MOBA_TOPK = 3
MOBA_GROUP = 4
SCALE = HEAD_DIM ** -0.5
D_DECAY_LORA = 64
D_AAA_LORA = 64
GMLP_GROUPS = 8
GMLP_CHUNK = 128
CONV_W = 3
NORM_EPS = 1e-6
LN_EPS = 1e-5
GN_EPS = 64e-5
LOG2E = 1.4426950408889634
MOBA_KAUG = 128
NEG = -1e30
WKV_CHUNK = 64
V7X_VMEM_LIMIT = 56 * 1024 * 1024


def _cparams(sem, vmem=V7X_VMEM_LIMIT):
    return pltpu.CompilerParams(dimension_semantics=sem, vmem_limit_bytes=vmem)


def _mm(a, b, prec=None):
    return jnp.dot(a, b, precision=prec, preferred_element_type=F32)


def _mm_nt(a, b, prec=None):
    return lax.dot_general(a, b, (((1,), (1,)), ((), ())), precision=prec, preferred_element_type=F32)


def _split3(x):
    p1 = x.astype(BF16)
    r1 = x - p1.astype(F32)
    p2 = r1.astype(BF16)
    p3 = (r1 - p2.astype(F32)).astype(BF16)
    return p1, p2, p3


def _rms(x, g):
    return x * lax.rsqrt(jnp.mean(x * x, axis=-1, keepdims=True) + NORM_EPS) * g


def _norm_matmul_kernel(x_ref, g_ref, w_ref, o_ref, xn_ref):
    @pl.when(pl.program_id(1) == 0)
    def _():
        xn_ref[...] = _rms(x_ref[...], g_ref[...]).astype(BF16)

    o_ref[...] = _mm(xn_ref[...], w_ref[...])


def _norm_matmul(x, g, w, tm, tn):
    m, kd = x.shape
    n = w.shape[1]
    return pl.pallas_call(
        _norm_matmul_kernel,
        grid=(m // tm, n // tn),
        in_specs=[pl.BlockSpec((tm, kd), lambda i, j: (i, 0)),
                  pl.BlockSpec((1, kd), lambda i, j: (0, 0)),
                  pl.BlockSpec((kd, tn), lambda i, j: (0, j))],
        out_specs=pl.BlockSpec((tm, tn), lambda i, j: (i, j)),
        out_shape=jax.ShapeDtypeStruct((m, n), F32),
        scratch_shapes=[pltpu.VMEM((tm, kd), BF16)],
        compiler_params=_cparams(("parallel", "arbitrary")),
        name="norm_matmul",
    )(x, g.reshape(1, kd), w)


def _shift_rows_kernel(x_ref, g_ref, prev_ref, w_ref, xn_ref, pp_ref):
    xn_ref[...] = _rms(x_ref[...], g_ref[...])
    pp_ref[...] = _mm(prev_ref[...].astype(BF16), w_ref[...])


def _shift_rows(x_last, g, prev_xn, w_b):
    b, kd = x_last.shape
    n = w_b.shape[1]
    bp = -(-b // 8) * 8
    pad = ((0, bp - b), (0, 0))
    xn, pp = pl.pallas_call(
        _shift_rows_kernel,
        out_shape=(jax.ShapeDtypeStruct((bp, kd), F32), jax.ShapeDtypeStruct((bp, n), F32)),
        compiler_params=_cparams(None),
        name="shift_rows",
    )(jnp.pad(x_last, pad), g.reshape(1, kd), jnp.pad(prev_xn, pad), w_b)
    return xn[:b], pp[:b]


def _kmeans_kernel(k_ref, o_ref):
    k = k_ref[0]
    nb = k.shape[0] // MOBA_BLOCK
    o_ref[0] = jnp.mean(k.reshape(nb, MOBA_BLOCK, HEAD_DIM), axis=1)


def _moba_prompt_kernel(slope_ref, qT_ref, km_ref, k_ref, vT_ref, oT_ref, sel_scr):
    i = pl.program_id(1)
    nb = km_ref.shape[1]
    blk = MOBA_BLOCK
    qf = qT_ref[0]
    gate = _mm(km_ref[0], qf, HI)
    nidx = lax.broadcasted_iota(jnp.int32, gate.shape, 0)
    g = jnp.where(nidx < i, gate, -jnp.inf)
    sel = jnp.zeros(gate.shape, F32)
    for _ in range(MOBA_TOPK):
        top = jnp.max(g, axis=0, keepdims=True)
        idx = jnp.min(jnp.where(g == top, nidx, nb), axis=0, keepdims=True)
        pick = (nidx == idx) & (top > -jnp.inf)
        sel = jnp.where(pick, 1.0, sel)
        g = jnp.where(pick, -jnp.inf, g)
    sel_scr[...] = sel

    slope2 = slope_ref[0] * LOG2E
    pieces = _split3(jnp.broadcast_to(slope2, (1, blk)))
    r16 = lax.broadcasted_iota(jnp.int32, (16, blk), 0)
    aug = jnp.zeros((16, blk), F32)
    for j, piece in enumerate(pieces):
        aug = jnp.where(r16 == j, piece.astype(F32), aug)
    qs = jnp.concatenate([(qf * (SCALE * LOG2E)).astype(BF16), aug.astype(BF16),
                          jnp.zeros((MOBA_KAUG - HEAD_DIM - 16, blk), BF16)], axis=0)
    ki = lax.broadcasted_iota(jnp.int32, (blk, blk), 0)
    qi = lax.broadcasted_iota(jnp.int32, (blk, blk), 1)

    own = pl.multiple_of(i * blk, blk)
    s = jnp.where(ki <= qi, _mm(k_ref[0, pl.ds(own, blk), :], qs), NEG)
    m0 = jnp.max(s, axis=0, keepdims=True)
    p = jnp.exp2(s - m0)
    l0 = jnp.sum(p, axis=0, keepdims=True)
    acc0 = _mm(vT_ref[0, :, pl.ds(own, blk)], p.astype(BF16))

    grp = MOBA_GROUP
    gk = grp * blk

    def qk(g):
        start = pl.multiple_of(jnp.minimum(g, nb // grp - 1) * gk, gk)
        return _mm(k_ref[0, pl.ds(start, gk), :], qs)

    def body(g, carry):
        m, l, acc, s_raw = carry
        s_next = qk(g + 1)
        n0 = pl.multiple_of(g * grp, grp)
        start = pl.multiple_of(g * gk, gk)
        s = s_raw
        picked = [sel_scr[pl.ds(n0 + u, 1), :] > 0.0 for u in range(grp)]
        m_new = m
        shift = []
        for u in range(grp):
            c_u = slope2 * ((n0 + u - i) * blk).astype(F32)
            mb = jnp.max(s[u * blk:(u + 1) * blk], axis=0, keepdims=True) + c_u
            m_new = jnp.where(picked[u], jnp.maximum(m_new, mb), m_new)
            shift.append(c_u)
        alpha = jnp.exp2(m - m_new)
        ps = []
        for u in range(grp):
            off = jnp.where(picked[u], m_new - shift[u], -NEG)
            p = jnp.exp2(s[u * blk:(u + 1) * blk] - off)
            l = l + jnp.sum(p, axis=0, keepdims=True) if u else alpha * l + jnp.sum(p, axis=0, keepdims=True)
            ps.append(p.astype(BF16))
        acc = alpha * acc + _mm(vT_ref[0, :, pl.ds(start, gk)], jnp.concatenate(ps, axis=0))
        return m_new, l, acc, s_next

    _, l, acc, _ = lax.fori_loop(0, (i + grp - 1) // grp, body, (m0, l0, acc0, qk(0)))
    oT_ref[0] = acc / l


def _moba_prompt(q, k, v, slopes):
    t = q.shape[0]
    assert t % (MOBA_BLOCK * MOBA_GROUP) == 0
    nb = t // MOBA_BLOCK

    def heads_t(z):
        return z.reshape(t, N_HEADS, HEAD_DIM).transpose(1, 2, 0)

    k_hm = k.reshape(t, N_HEADS, HEAD_DIM).transpose(1, 0, 2)
    offs = jnp.broadcast_to((jnp.arange(t) % MOBA_BLOCK).astype(BF16)[None, :, None], (N_HEADS, t, 3))
    k_aug = jnp.concatenate([k_hm.astype(BF16), offs,
                             jnp.zeros((N_HEADS, t, MOBA_KAUG - HEAD_DIM - 3), BF16)], axis=-1)
    kmean = pl.pallas_call(
        _kmeans_kernel,
        grid=(N_HEADS,),
        in_specs=[pl.BlockSpec((1, t, HEAD_DIM), lambda h: (h, 0, 0))],
        out_specs=pl.BlockSpec((1, nb, HEAD_DIM), lambda h: (h, 0, 0)),
        out_shape=jax.ShapeDtypeStruct((N_HEADS, nb, HEAD_DIM), F32),
        compiler_params=_cparams(("parallel",)),
        name="moba_kmeans",
    )(k_hm)
    oT = pl.pallas_call(
        _moba_prompt_kernel,
        grid=(N_HEADS, nb),
        in_specs=[pl.BlockSpec((1, 1, 1), lambda h, i: (h, 0, 0)),
                  pl.BlockSpec((1, HEAD_DIM, MOBA_BLOCK), lambda h, i: (h, 0, i)),
                  pl.BlockSpec((1, nb, HEAD_DIM), lambda h, i: (h, 0, 0)),
                  pl.BlockSpec((1, t, MOBA_KAUG), lambda h, i: (h, 0, 0)),
                  pl.BlockSpec((1, HEAD_DIM, t), lambda h, i: (h, 0, 0))],
        out_specs=pl.BlockSpec((1, HEAD_DIM, MOBA_BLOCK), lambda h, i: (h, 0, i)),
        out_shape=jax.ShapeDtypeStruct((N_HEADS, HEAD_DIM, t), F32),
        scratch_shapes=[pltpu.VMEM((nb, MOBA_BLOCK), F32)],
        compiler_params=_cparams(("parallel", "arbitrary")),
        name="moba_prompt",
    )(slopes.reshape(N_HEADS, 1, 1), heads_t(q), kmean, k_aug, heads_t(v).astype(BF16))
    return oT.transpose(2, 0, 1).reshape(t, D_HEADS)


GATE_PAGES = 16


def _moba_gate_kernel(pt_ref, q_ref, ck_ref, sel_ref, kbuf, km_scr, sem, *, n_pages):
    i = pl.program_id(0)
    nsteps = pl.num_programs(0)
    steps_per_seq = n_pages // GATE_PAGES
    g = i % steps_per_seq
    blk_pages = MOBA_BLOCK // ck_ref.shape[3]
    nb = n_pages // blk_pages

    def copies(step, slot):
        return [pltpu.make_async_copy(ck_ref.at[pt_ref[step * GATE_PAGES + c]], kbuf.at[slot, c], sem.at[slot])
                for c in range(GATE_PAGES)]

    @pl.when(i == 0)
    def _():
        for cp in copies(0, 0):
            cp.start()

    @pl.when(i + 1 < nsteps)
    def _():
        for cp in copies(i + 1, (i + 1) % 2):
            cp.start()

    slot = i % 2
    for cp in copies(i, slot):
        cp.wait()

    @pl.when(g == 0)
    def _():
        km_scr[...] = jnp.zeros(km_scr.shape, F32)

    lane = lax.broadcasted_iota(jnp.int32, km_scr.shape, 2)
    acc = km_scr[...]
    for b in range(GATE_PAGES // blk_pages):
        x = kbuf[slot, blk_pages * b]
        for j in range(1, blk_pages):
            x = x + kbuf[slot, blk_pages * b + j]
        acc = jnp.where(lane == g * (GATE_PAGES // blk_pages) + b, jnp.sum(x, axis=-1, keepdims=True), acc)
    km_scr[...] = acc

    @pl.when(g == steps_per_seq - 1)
    def _():
        km = acc * (1.0 / MOBA_BLOCK)
        gate = jnp.concatenate([_mm(q_ref[0, h], km[h], HI) for h in range(N_HEADS)], axis=0)
        bidx = lax.broadcasted_iota(jnp.int32, gate.shape, 1)
        gt = jnp.where(bidx < nb, gate, -jnp.inf)
        sel = jnp.zeros(gate.shape, F32)
        for _ in range(MOBA_TOPK):
            top = jnp.max(gt, axis=1, keepdims=True)
            idx = jnp.min(jnp.where(gt == top, bidx, nb), axis=1, keepdims=True)
            pick = (bidx == idx) & (top > -jnp.inf)
            sel = jnp.where(pick, 1.0, sel)
            gt = jnp.where(pick, -jnp.inf, gt)
        sel_ref[0] = sel


def _moba_attend_kernel(pg_ref, slope_ref, q_ref, kn_ref, vn_ref, dist_ref, ck_ref, cv_ref, o_ref,
                        kbuf, vbuf, sem, *, n_visits):
    i = pl.program_id(0)
    nsteps = pl.num_programs(0)
    page = ck_ref.shape[3]
    n_copies = n_visits * (MOBA_BLOCK // page)

    def copies(step, slot):
        h = step % N_HEADS
        out = []
        for c in range(n_copies):
            pg = pg_ref[step * n_copies + c]
            cols = pl.ds(c * page, page)
            out.append(pltpu.make_async_copy(ck_ref.at[pg, h], kbuf.at[slot, :, cols], sem.at[0, slot]))
            out.append(pltpu.make_async_copy(cv_ref.at[pg, h], vbuf.at[slot, :, cols], sem.at[1, slot]))
        return out

    @pl.when(i == 0)
    def _():
        for cp in copies(0, 0):
            cp.start()

    @pl.when(i + 1 < nsteps)
    def _():
        for cp in copies(i + 1, (i + 1) % 2):
            cp.start()

    slot = i % 2
    for cp in copies(i, slot):
        cp.wait()

    nq = q_ref.shape[2]
    slope = slope_ref[0]
    qs = (q_ref[0, 0] * SCALE).astype(BF16)
    ncol = n_visits * MOBA_BLOCK
    s = _mm(qs, kbuf[slot].astype(BF16)) - slope * dist_ref[0, 0]
    row = lax.broadcasted_iota(jnp.int32, (nq, ncol), 0)
    col = lax.broadcasted_iota(jnp.int32, (nq, ncol), 1)
    s = jnp.where(col // (ncol // nq) == row, s, NEG)
    nk = kn_ref.shape[2]
    dist_o = (lax.broadcasted_iota(jnp.int32, (nq, nk), 0)
              - lax.broadcasted_iota(jnp.int32, (nq, nk), 1)).astype(F32)
    s_o = jnp.where(dist_o >= 0.0, _mm_nt(qs, kn_ref[0, 0].astype(BF16)) - slope * dist_o, NEG)
    m = jnp.maximum(jnp.max(s, axis=1, keepdims=True), jnp.max(s_o, axis=1, keepdims=True))
    p = jnp.exp(s - m)
    p_o = jnp.exp(s_o - m)
    l = jnp.sum(p, axis=1, keepdims=True) + jnp.sum(p_o, axis=1, keepdims=True)
    acc = _mm_nt(p.astype(BF16), vbuf[slot].astype(BF16)) + _mm(p_o.astype(BF16), vn_ref[0, 0].astype(BF16))
    o_ref[0, 0] = acc / l


def _moba_sample(q, k, v, cache_k, cache_v, page_table, slopes):
    b, s, _ = q.shape
    n_pool, page, _, _ = cache_k.shape
    n_pages = page_table.shape[1]
    past = n_pages * page
    pages_per_blk = MOBA_BLOCK // page
    assert pages_per_blk == 2 and past % MOBA_BLOCK == 0
    nb = past // MOBA_BLOCK
    assert nb >= MOBA_TOPK
    pt = page_table.reshape(-1).astype(jnp.int32)

    ck = cache_k.transpose(0, 2, 3, 1)
    cv = cache_v.transpose(0, 2, 3, 1)

    nk = 16
    lanes = 128
    assert n_pages % GATE_PAGES == 0 and nb <= lanes
    steps_per_seq = n_pages // GATE_PAGES

    def heads(z, rows):
        z = jnp.pad(z, ((0, 0), (0, rows - s), (0, 0)))
        return z.reshape(b, rows, N_HEADS, HEAD_DIM).transpose(0, 2, 1, 3)

    sel = pl.pallas_call(
        functools.partial(_moba_gate_kernel, n_pages=n_pages),
        grid_spec=pltpu.PrefetchScalarGridSpec(
            num_scalar_prefetch=1,
            grid=(b * steps_per_seq,),
            in_specs=[pl.BlockSpec((1, N_HEADS, s, HEAD_DIM), lambda i, pt: (i // steps_per_seq, 0, 0, 0)),
                      pl.BlockSpec(memory_space=pl.ANY)],
            out_specs=pl.BlockSpec((1, N_HEADS * s, lanes), lambda i, pt: (i // steps_per_seq, 0, 0)),
            scratch_shapes=[pltpu.VMEM((2, GATE_PAGES, N_HEADS, HEAD_DIM, page), F32),
                            pltpu.VMEM((N_HEADS, HEAD_DIM, lanes), F32),
                            pltpu.SemaphoreType.DMA((2,))]),
        out_shape=jax.ShapeDtypeStruct((b, N_HEADS * s, lanes), F32),
        compiler_params=_cparams(("arbitrary",)),
        name="moba_gate",
    )(pt, heads(q, s), ck)

    n_visits = s * MOBA_TOPK
    blk = jnp.argsort(-sel[:, :, :nb], axis=2, stable=True)[:, :, :MOBA_TOPK]
    blk = blk.reshape(b, N_HEADS, n_visits)
    pages = page_table[jnp.arange(b)[:, None, None, None],
                       blk[..., None] * pages_per_blk + jnp.arange(pages_per_blk)]
    kpos = blk[..., None] * MOBA_BLOCK + jnp.arange(MOBA_BLOCK)
    qpos = past + jnp.arange(n_visits) // MOBA_TOPK
    dist = (qpos[None, None, :, None] - kpos).astype(F32).reshape(b, N_HEADS, 1, n_visits * MOBA_BLOCK)

    def bh_spec(r, w):
        return pl.BlockSpec((1, 1, r, w), lambda i, pg: (i // N_HEADS, i % N_HEADS, 0, 0))

    out = pl.pallas_call(
        functools.partial(_moba_attend_kernel, n_visits=n_visits),
        grid_spec=pltpu.PrefetchScalarGridSpec(
            num_scalar_prefetch=1,
            grid=(b * N_HEADS,),
            in_specs=[pl.BlockSpec((1, 1, 1), lambda i, pg: (i % N_HEADS, 0, 0)),
                      bh_spec(s, HEAD_DIM), bh_spec(nk, HEAD_DIM), bh_spec(nk, HEAD_DIM),
                      bh_spec(1, n_visits * MOBA_BLOCK),
                      pl.BlockSpec(memory_space=pl.ANY), pl.BlockSpec(memory_space=pl.ANY)],
            out_specs=bh_spec(s, HEAD_DIM),
            scratch_shapes=[pltpu.VMEM((2, HEAD_DIM, n_visits * MOBA_BLOCK), F32),
                            pltpu.VMEM((2, HEAD_DIM, n_visits * MOBA_BLOCK), F32),
                            pltpu.SemaphoreType.DMA((2, 2))]),
        out_shape=jax.ShapeDtypeStruct((b, N_HEADS, s, HEAD_DIM), F32),
        compiler_params=_cparams(("arbitrary",)),
        name="moba_attend",
    )(pages.reshape(-1).astype(jnp.int32), slopes.reshape(N_HEADS, 1, 1), heads(q, s), heads(k, nk), heads(v, nk),
      dist, ck, cv)
    return out.transpose(0, 2, 1, 3).reshape(b, s, D_HEADS)


def _softplus(z):
    return jnp.maximum(z, 0.0) + jnp.log1p(jnp.exp(-jnp.abs(z)))


def _rwkv_prep_kernel(pb_ref, first_ref, mu_ref, w0_ref, w2_ref, a0_ref, a2_ref, g2_ref, kk_ref, ka_ref,
                      rk_ref, gm_ref, r_o, lw_o, k_o, v_o, kk_o, b_o, g_o, bonus_o, *, seq_len):
    pb = pb_ref[...]
    tm = pb.shape[0]
    row = lax.broadcasted_iota(jnp.int32, (tm, 1), 0)
    if seq_len is None:
        starts, first = row == 0, first_ref[0]
    else:
        starts, first = (row % seq_len) == 0, first_ref[...]
    prev = jnp.where(starts, first, pltpu.roll(pb, 1, axis=0))
    xm = pb + mu_ref[...] * (prev - pb)
    d = D_HEADS
    o1, o2, o3 = 3 * d, 3 * d + D_DECAY_LORA, 3 * d + D_DECAY_LORA + D_AAA_LORA
    r, k, v = xm[:, 0:d], xm[:, d:2 * d], xm[:, 2 * d:o1]
    xw, xa, xg = xm[:, o1:o2], xm[:, o2:o3], xm[:, o3:]
    logw = -_softplus(-(w0_ref[...] + _mm(jnp.tanh(xw), w2_ref[...]))) - 0.5
    a = jax.nn.sigmoid(a0_ref[...] + _mm(xa, a2_ref[...]))
    gmat = gm_ref[...]
    kk = k * kk_ref[...]
    kk = kk / jnp.maximum(jnp.sqrt(_mm(kk * kk, gmat, HI)), 1e-12)
    k = k * (1.0 + (a - 1.0) * ka_ref[...])
    r_o[...] = r
    lw_o[...] = -jnp.exp(logw)
    k_o[...] = k
    v_o[...] = v
    kk_o[...] = kk
    b_o[...] = kk * a
    g_o[...] = _mm(jax.nn.sigmoid(xg), g2_ref[...])
    bonus_o[...] = _mm(r * k * rk_ref[...], gmat, HI) * v


def _rwkv_prep(pb, first, prm, gmat, tm, seq_len):
    m, n = pb.shape
    d = D_HEADS
    if seq_len is None:
        first_spec = pl.BlockSpec((1, 1, n), lambda i: (i, 0, 0))
    else:
        first_spec = pl.BlockSpec((tm, n), lambda i: (i, 0))

    def full(shape):
        return pl.BlockSpec(shape, lambda i: (0,) * len(shape))

    def vec(z):
        return z.reshape(1, -1)

    outs = pl.pallas_call(
        functools.partial(_rwkv_prep_kernel, seq_len=seq_len),
        grid=(m // tm,),
        in_specs=[pl.BlockSpec((tm, n), lambda i: (i, 0)), first_spec,
                  full((1, n)), full((1, d)), full((D_DECAY_LORA, d)), full((1, d)), full((D_AAA_LORA, d)),
                  full((n - 3 * d - D_DECAY_LORA - D_AAA_LORA, d)), full((1, d)), full((1, d)), full((1, d)),
                  full((d, d))],
        out_specs=[pl.BlockSpec((tm, d), lambda i: (i, 0))] * 8,
        out_shape=[jax.ShapeDtypeStruct((m, d), F32)] * 8,
        compiler_params=_cparams(("parallel",)),
        name="rwkv_prep",
    )(pb, first, vec(prm['rw_mu']), vec(prm['rw_w0']), prm['rw_w2'], vec(prm['rw_a0']), prm['rw_a2'],
      prm['rw_g2'], vec(prm['rw_k_k']), vec(prm['rw_k_a']), vec(prm['rw_r_k']), gmat)
    return outs


def _rwkv_scan_kernel(r_ref, lw_ref, k_ref, v_ref, kk_ref, b_ref, kkT_ref, lwT_ref, vT_ref, s0_ref,
                      y_ref, sfin_ref, s_scr, *, chunk, chunks_per_step):
    j = pl.program_id(1)

    @pl.when(j == 0)
    def _():
        s_scr[...] = s0_ref[0]

    step = chunk * chunks_per_step
    d = D_HEADS
    si = lax.broadcasted_iota(jnp.int32, (step, step), 0)
    sj = lax.broadcasted_iota(jnp.int32, (step, step), 1)
    same = (si // chunk) == (sj // chunk)
    lbd = jnp.where(same & (si >= sj), 1.0, 0.0).astype(BF16)
    ubd = jnp.where(same & (si <= sj), 1.0, 0.0).astype(BF16)
    lw = lw_ref[...]
    lwT = lwT_ref[0]
    cl = sum(_mm(lbd, p) for p in _split3(lw))
    clT = sum(_mm(p, ubd) for p in _split3(lwT))
    tots = [cl[(c + 1) * chunk - 1:(c + 1) * chunk, :] for c in range(chunks_per_step)]
    tot_full = jnp.concatenate([jnp.broadcast_to(t, (chunk, d)) for t in tots], axis=0)
    e_neg = jnp.exp(-cl)
    e_end = jnp.exp(tot_full - cl)
    b = b_ref[...]
    k = k_ref[...]
    rho = r_ref[...] * jnp.exp(cl)
    beta, kap = b * e_neg, k * e_neg
    b_end, k_end = (b * e_end).astype(BF16), (k * e_end).astype(BF16)
    alphaT = kkT_ref[0] * jnp.exp(clT - lwT)
    vT = vT_ref[0].astype(BF16)
    v = v_ref[...].astype(BF16)

    ii = lax.broadcasted_iota(jnp.int32, (chunk, chunk), 0)
    jj = lax.broadcasted_iota(jnp.int32, (chunk, chunk), 1)
    low, upper = ii >= jj, ii < jj
    hd = HEAD_DIM
    units = [(slice(c * chunk, (c + 1) * chunk), slice(h * hd, (h + 1) * hd))
             for c in range(chunks_per_step) for h in range(N_HEADS)]
    bk = [jnp.concatenate([beta[rs, cs], kap[rs, cs]], axis=0).astype(BF16) for rs, cs in units]
    aT = [alphaT[cs, rs] for rs, cs in units]
    g1 = [_mm(x, a.astype(BF16)) for x, a in zip(bk, aT)]
    lt = [jnp.where(upper, g[:chunk], 0.0).astype(BF16) for g in g1]
    akT = [jnp.where(upper, g[chunk:], 0.0).astype(BF16) for g in g1]
    g2 = [_mm_nt(rho[rs, cs].astype(BF16), x) for (rs, cs), x in zip(units, bk)]
    arb = [jnp.where(low, g[:, :chunk], 0.0).astype(BF16) for g in g2]
    ark = [jnp.where(low, g[:, chunk:], 0.0).astype(BF16) for g in g2]
    z = [jnp.concatenate([a, _mm(vT[cs, rs], x)], axis=0) for (rs, cs), a, x in zip(units, aT, akT)]
    z = [x - _mm(x.astype(BF16), l) for x, l in zip(z, lt)]
    pw = lt
    n = 2
    while n < chunk:
        pw = [_mm(p, p).astype(BF16) for p in pw]
        z = [x + _mm(x.astype(BF16), p) for x, p in zip(z, pw)]
        n *= 2
    pz = [(-x).astype(BF16) for x in z]
    g3 = [_mm_nt(a, p) for a, p in zip(arb, pz)]
    g4 = [_mm(p, b_end[rs, cs]) for (rs, cs), p in zip(units, pz)]
    yin = [g[:, hd:] + _mm(a, v[rs, cs]) for (rs, cs), g, a in zip(units, g3, ark)]
    vk = [_mm(vT[cs, rs], k_end[rs, cs]) for rs, cs in units]
    state = [s_scr[h] for h in range(N_HEADS)]
    for c in range(chunks_per_step):
        etot = jnp.exp(tots[c])
        for h in range(N_HEADS):
            u = c * N_HEADS + h
            rs, cs = units[u]
            s = state[h]
            sb = s.astype(BF16)
            y_ref[rs, cs] = _mm_nt((rho[rs, cs] + g3[u][:, :hd]).astype(BF16), sb) + yin[u]
            state[h] = s * etot[:, cs] + _mm(sb, g4[u][:hd].astype(BF16)) + g4[u][hd:] + vk[u]
    for h in range(N_HEADS):
        s_scr[h] = state[h]

    @pl.when(j == pl.num_programs(1) - 1)
    def _():
        sfin_ref[0] = s_scr[...]


def _rwkv_scan(r, lw, k, v, kk, b, s0, seq_len, chunks_per_step):
    m, d = r.shape
    nseq = m // seq_len
    step = WKV_CHUNK * chunks_per_step
    nsteps = seq_len // step

    def tr(z):
        return z.reshape(nseq, seq_len, d).transpose(0, 2, 1)

    tm_spec = pl.BlockSpec((step, d), lambda i, j: (i * nsteps + j, 0))
    fm_spec = pl.BlockSpec((1, d, step), lambda i, j: (i, 0, j))
    st_spec = pl.BlockSpec((1, N_HEADS, HEAD_DIM, HEAD_DIM), lambda i, j: (i, 0, 0, 0))
    return pl.pallas_call(
        functools.partial(_rwkv_scan_kernel, chunk=WKV_CHUNK, chunks_per_step=chunks_per_step),
        grid=(nseq, nsteps),
        in_specs=[tm_spec] * 6 + [fm_spec] * 3 + [st_spec],
        out_specs=[tm_spec, st_spec],
        out_shape=[jax.ShapeDtypeStruct((m, d), F32),
                   jax.ShapeDtypeStruct((nseq, N_HEADS, HEAD_DIM, HEAD_DIM), F32)],
        scratch_shapes=[pltpu.VMEM((N_HEADS, HEAD_DIM, HEAD_DIM), F32)],
        compiler_params=_cparams(("parallel", "arbitrary")),
        name="rwkv_scan",
    )(r, lw, k, v, kk, b, tr(kk), tr(lw), tr(v), s0)


def _l0_out_kernel(y_ref, bonus_ref, g_ref, oa_ref, h_ref, gnw_ref, gnb_ref, gm_ref, wa_ref, wb_ref,
                   gpost_ref, o_ref):
    y = y_ref[...]
    gmat = gm_ref[...]
    inv = 1.0 / HEAD_DIM
    yc = y - _mm(y, gmat, HI) * inv
    yn = yc * lax.rsqrt(_mm(yc * yc, gmat, HI) * inv + GN_EPS) * gnw_ref[...] + gnb_ref[...]
    ob = (yn + bonus_ref[...]) * g_ref[...]
    mix = _mm(oa_ref[...].astype(BF16), wa_ref[...]) + _mm(ob.astype(BF16), wb_ref[...])
    o_ref[...] = h_ref[...] + _rms(mix, gpost_ref[...])


def _l0_out(y, bonus, g, oa, h, prm, gmat, w_out, tm):
    m, dm = h.shape
    d = D_HEADS

    def rowblk(w):
        return pl.BlockSpec((tm, w), lambda i: (i, 0))

    def full(shape):
        return pl.BlockSpec(shape, lambda i: (0,) * len(shape))

    return pl.pallas_call(
        _l0_out_kernel,
        grid=(m // tm,),
        in_specs=[rowblk(d), rowblk(d), rowblk(d), rowblk(d), rowblk(dm), full((1, d)), full((1, d)),
                  full((d, d)), full((d, dm)), full((d, dm)), full((1, dm))],
        out_specs=rowblk(dm),
        out_shape=jax.ShapeDtypeStruct((m, dm), F32),
        compiler_params=_cparams(("parallel",)),
        name="l0_out",
    )(y, bonus, g, oa, h, prm['rw_gn_w'].reshape(1, d), prm['rw_gn_b'].reshape(1, d), gmat,
      w_out[:d], w_out[d:], prm['l0_norm_post'].reshape(1, dm))


FFN_COLS = 256
FFN_HALO = 16


def _ffn_kernel(h_ref, halo_ref, gpre_ref, wg_ref, wv_ref, cwg_ref, cwv_ref, cbg_ref, cbv_ref, pg_ref, pv_ref,
                wd_ref, gpost_ref, o_ref, xn_scr, acc_scr, *, seq_len):
    i = pl.program_id(0)
    c = pl.program_id(1)
    tm = h_ref.shape[0]

    @pl.when(c == 0)
    def _():
        xn_scr[0:tm] = _rms(h_ref[...], gpre_ref[...]).astype(BF16)
        if seq_len is None:
            xn_scr[tm:tm + FFN_HALO] = _rms(halo_ref[...], gpre_ref[...]).astype(BF16)
        acc_scr[...] = jnp.zeros(acc_scr.shape, F32)

    row = lax.broadcasted_iota(jnp.int32, (tm, 1), 0)
    pos = row if seq_len is None else row % seq_len

    def conv(w_ref, cw_ref, cb_ref, p_ref):
        up = _mm(xn_scr[...], w_ref[...])
        u = up[:tm]
        m1 = pltpu.roll(u, 1, axis=0)
        m2 = pltpu.roll(u, 2, axis=0)
        if seq_len is None:
            last = tm + FFN_HALO
            p0 = jnp.where(i == 0, p_ref[6:7], up[last - 2:last - 1])
            p1 = jnp.where(i == 0, p_ref[7:8], up[last - 1:last])
            m1 = jnp.concatenate([jnp.where(pos[:8] == 0, p1, m1[:8]), m1[8:]], axis=0)
            m2 = jnp.concatenate([jnp.where(pos[:8] == 0, p0, jnp.where(pos[:8] == 1, p1, m2[:8])), m2[8:]], axis=0)
        else:
            m1 = jnp.where(pos == 0, p_ref[1], m1)
            m2 = jnp.where(pos == 0, p_ref[0], jnp.where(pos == 1, p_ref[1], m2))
        return cb_ref[...] + cw_ref[0:1] * m2 + cw_ref[1:2] * m1 + cw_ref[2:3] * u

    gate = conv(wg_ref, cwg_ref, cbg_ref, pg_ref)
    val = conv(wv_ref, cwv_ref, cbv_ref, pv_ref)
    acc_scr[...] += _mm((jax.nn.gelu(gate) * val).astype(BF16), wd_ref[...])

    @pl.when(c == pl.num_programs(1) - 1)
    def _():
        o_ref[...] = h_ref[...] + _rms(acc_scr[...], gpost_ref[...])


def _ffn(h, prev_rows, g_pre, w_up, conv_w, conv_b, w_down, g_post, tm, seq_len):
    m, dm = h.shape
    dff = w_down.shape[0]
    n2 = 2 * dff
    assert dff % FFN_COLS == 0
    nc = dff // FFN_COLS
    if seq_len is None:
        prev = jnp.pad(prev_rows[0], ((6, 0), (0, 0)))

        def p_spec(off):
            return pl.BlockSpec((8, FFN_COLS), lambda i, c: (0, off + c))

        halo_spec = pl.BlockSpec((FFN_HALO, dm), lambda i, c: (jnp.maximum(i * (tm // FFN_HALO) - 1, 0), 0))
        xn_rows = tm + FFN_HALO
    else:
        prev = jnp.stack([jnp.repeat(prev_rows[:, 0], seq_len, axis=0), jnp.repeat(prev_rows[:, 1], seq_len, axis=0)])

        def p_spec(off):
            return pl.BlockSpec((2, tm, FFN_COLS), lambda i, c: (0, i, off + c))

        halo_spec = pl.BlockSpec((FFN_HALO, dm), lambda i, c: (0, 0))
        xn_rows = tm

    def col_spec(rows, off):
        return pl.BlockSpec((rows, FFN_COLS), lambda i, c: (0, off + c))

    def full(shape):
        return pl.BlockSpec(shape, lambda i, c: (0,) * len(shape))

    cb = conv_b.reshape(1, n2)
    return pl.pallas_call(
        functools.partial(_ffn_kernel, seq_len=seq_len),
        grid=(m // tm, nc),
        in_specs=[pl.BlockSpec((tm, dm), lambda i, c: (i, 0)), halo_spec, full((1, dm)),
                  col_spec(dm, 0), col_spec(dm, nc), col_spec(CONV_W, 0), col_spec(CONV_W, nc),
                  col_spec(1, 0), col_spec(1, nc), p_spec(0), p_spec(nc),
                  pl.BlockSpec((FFN_COLS, dm), lambda i, c: (c, 0)), full((1, dm))],
        out_specs=pl.BlockSpec((tm, dm), lambda i, c: (i, 0)),
        out_shape=jax.ShapeDtypeStruct((m, dm), F32),
        scratch_shapes=[pltpu.VMEM((xn_rows, dm), BF16), pltpu.VMEM((tm, dm), F32)],
        compiler_params=_cparams(("parallel", "arbitrary")),
        name="ffn",
    )(h, h, g_pre.reshape(1, dm), w_up, w_up, conv_w, conv_w, cb, cb, prev, prev, w_down, g_post.reshape(1, dm))


def _gmlp_kernel(h_ref, gpre_ref, win_ref, lnw_ref, lnb_ref, wmix_ref, bias_ref, wout_ref, gpost_ref,
                 o_ref, cv_ref, gated_scr):
    h = h_ref[...]
    tm = h.shape[0]
    dc = wout_ref.shape[0]
    rm = wmix_ref.shape[1]
    gw = dc // GMLP_GROUPS
    z = jax.nn.gelu(_mm(_rms(h, gpre_ref[...]).astype(BF16), win_ref[...]))
    v = z[:, dc:]
    vc = v - jnp.mean(v, axis=-1, keepdims=True)
    vn = vc * lax.rsqrt(jnp.mean(vc * vc, axis=-1, keepdims=True) + LN_EPS) * lnw_ref[...] + lnb_ref[...]
    cv_ref[...] = vn
    for s in range(tm // rm):
        rows = slice(s * rm, (s + 1) * rm)
        for g in range(GMLP_GROUPS):
            cols = slice(g * gw, (g + 1) * gw)
            mixed = _mm(wmix_ref[g], vn[rows, cols].astype(BF16)) + bias_ref[:, cols]
            gated_scr[rows, cols] = (z[rows, cols] * mixed).astype(BF16)
    o_ref[...] = h + _rms(_mm(gated_scr[...], wout_ref[...]), gpost_ref[...])


def _gmlp(h, g_pre, w_in, ln_w, ln_b, wmix, bias, w_out, g_post, tm):
    m, dm = h.shape
    dc = w_out.shape[0]
    rm = wmix.shape[1]

    def full(shape):
        return pl.BlockSpec(shape, lambda i: (0,) * len(shape))

    return pl.pallas_call(
        _gmlp_kernel,
        grid=(m // tm,),
        in_specs=[pl.BlockSpec((tm, dm), lambda i: (i, 0)), full((1, dm)), full((dm, 2 * dc)), full((1, dc)),
                  full((1, dc)), full((GMLP_GROUPS, rm, rm)), full((rm, dc)), full((dc, dm)), full((1, dm))],
        out_specs=[pl.BlockSpec((tm, dm), lambda i: (i, 0)), pl.BlockSpec((tm, dc), lambda i: (i, 0))],
        out_shape=[jax.ShapeDtypeStruct((m, dm), F32), jax.ShapeDtypeStruct((m, dc), F32)],
        scratch_shapes=[pltpu.VMEM((tm, dc), BF16)],
        compiler_params=_cparams(("parallel",)),
        name="gmlp",
    )(h, g_pre.reshape(1, dm), w_in, ln_w.reshape(1, dc), ln_b.reshape(1, dc), wmix, bias, w_out,
      g_post.reshape(1, dm))


def _trunk(x, attend, prev_xn, wkv0, conv0, prm, w, *, tile, long_seq):
    nseq, t, dm = x.shape
    m = nseq * t
    h = x.reshape(m, dm)
    seq_len = None if long_seq else t

    g0 = prm['l0_norm_pre']
    qkv = _norm_matmul(h, g0, w['w_qkv'], tile, 512)
    pb = _norm_matmul(h, g0, w['w_b'], tile, 256)
    shift_new, prev_proj = _shift_rows(x[:, -1], g0, prev_xn, w['w_b'])
    q, k_new, v_new = qkv[:, :D_HEADS], qkv[:, D_HEADS:2 * D_HEADS], qkv[:, 2 * D_HEADS:]
    if long_seq:
        first = jnp.concatenate([prev_proj, pb[tile - 1:m - 1:tile]], axis=0).reshape(m // tile, 1, -1)
    else:
        first = jnp.repeat(prev_proj, t, axis=0)
    r, lw, kmod, vv, kk, bb, gate, bonus = _rwkv_prep(pb, first, prm, w['gmat'], tile, seq_len)
    o_a = attend(q, k_new, v_new)

    t_pad = -(-t // WKV_CHUNK) * WKV_CHUNK

    def padt(z):
        if t_pad == t:
            return z
        return jnp.pad(z.reshape(nseq, t, -1), ((0, 0), (0, t_pad - t), (0, 0))).reshape(nseq * t_pad, -1)

    y, wkv_new = _rwkv_scan(padt(r), padt(lw), padt(kmod), padt(vv), padt(kk), padt(bb), wkv0, t_pad,
                            2 if t_pad % (2 * WKV_CHUNK) == 0 else 1)
    if t_pad != t:
        y = y.reshape(nseq, t_pad, -1)[:, :t].reshape(m, -1)
    h = _l0_out(y, bonus, gate, o_a, h, prm, w['gmat'], w['w_out0'], tile)

    conv_rows = []
    ffn_tile = min(m, 1024)

    def ffn(h, layer):
        g_pre = prm['ffn_norm_pre'][layer]
        tail = h.reshape(nseq, t, dm)[:, t - min(t, 8):].reshape(-1, dm)
        up_tail = _norm_matmul(tail, g_pre, w['w_up'][layer], tail.shape[0], 512)
        conv_rows.append(up_tail.reshape(nseq, min(t, 8), -1)[:, -(CONV_W - 1):])
        return _ffn(h, conv0[layer], g_pre, w['w_up'][layer], prm['ffn_conv_w'][layer], prm['ffn_conv_b'][layer],
                    w['w_down'][layer], prm['ffn_norm_post'][layer], ffn_tile, seq_len)

    h = ffn(h, 0)

    if long_seq:
        wmix, bias = w['wmix'], w['gbias']
    else:
        eye = jnp.eye(tile // t, dtype=F32)
        wmix = jnp.einsum('ab,gij->gaibj', eye, w['wmix'][:, :t, :t].astype(F32))
        wmix = wmix.reshape(GMLP_GROUPS, tile, tile).astype(BF16)
        bias = jnp.tile(w['gbias'][:t], (tile // t, 1))
    h, chunk_v = _gmlp(h, prm['l1_norm_pre'], w['w_in1'], prm['gm_ln_w'], prm['gm_ln_b'], wmix, bias,
                       w['w_out1'], prm['l1_norm_post'], tile)
    h = ffn(h, 1)
    return (h.reshape(nseq, t, dm), k_new.reshape(nseq, t, N_HEADS, HEAD_DIM),
            v_new.reshape(nseq, t, N_HEADS, HEAD_DIM), wkv_new, shift_new, chunk_v.reshape(nseq, t, -1),
            jnp.stack(conv_rows))


def kernel(x_prompt, x_sample, cache_k, cache_v, state_wkv, state_shift, state_ffn_conv, page_table,
           l0_norm_pre, l0_norm_post, l0_w_in, rw_mu, rw_w0, rw_w2, rw_a0, rw_a2, rw_g2, rw_k_k, rw_k_a,
           rw_r_k, rw_gn_w, rw_gn_b, l0_w_out, l1_norm_pre, l1_norm_post, l1_w_in, gm_ln_w, gm_ln_b,
           gm_ws, gm_bs, l1_w_out, ffn_norm_pre, ffn_norm_post, ffn_w_up, ffn_conv_w, ffn_conv_b,
           ffn_w_down):
    prm = dict(l0_norm_pre=l0_norm_pre, l0_norm_post=l0_norm_post, rw_mu=rw_mu, rw_w0=rw_w0, rw_w2=rw_w2,
               rw_a0=rw_a0, rw_a2=rw_a2, rw_g2=rw_g2, rw_k_k=rw_k_k, rw_k_a=rw_k_a, rw_r_k=rw_r_k,
               rw_gn_w=rw_gn_w, rw_gn_b=rw_gn_b, l1_norm_pre=l1_norm_pre, l1_norm_post=l1_norm_post,
               gm_ln_w=gm_ln_w, gm_ln_b=gm_ln_b, ffn_norm_pre=ffn_norm_pre, ffn_norm_post=ffn_norm_post,
               ffn_conv_w=ffn_conv_w, ffn_conv_b=ffn_conv_b)
    d_a = D_HEADS
    dc = l1_w_out.shape[0]
    tri = jnp.tril(jnp.ones((GMLP_CHUNK, GMLP_CHUNK), dtype=bool))
    hd = jnp.arange(D_HEADS) // HEAD_DIM
    w = dict(
        w_qkv=l0_w_in[:, :3 * d_a].astype(BF16),
        w_b=l0_w_in[:, 3 * d_a:].astype(BF16),
        w_out0=l0_w_out.astype(BF16),
        w_up=ffn_w_up.astype(BF16),
        w_down=ffn_w_down.astype(BF16),
        w_in1=l1_w_in.astype(BF16),
        w_out1=l1_w_out.astype(BF16),
        wmix=jnp.where(tri[None], gm_ws, 0.0).astype(BF16),
        gbias=jnp.repeat(gm_bs.T, dc // GMLP_GROUPS, axis=1),
        gmat=(hd[:, None] == hd[None, :]).astype(F32),
    )
    slopes = jnp.exp2(-8.0 * jnp.arange(1, N_HEADS + 1, dtype=F32) / N_HEADS)

    bp, tp, dm = x_prompt.shape
    assert bp == 1
    dt = x_prompt.dtype
    depth = ffn_w_up.shape[0]
    wkv0 = jnp.zeros((bp, N_HEADS, HEAD_DIM, HEAD_DIM), dt)
    shift0 = jnp.zeros((bp, dm), dt)
    conv0 = jnp.zeros((depth, bp, CONV_W - 1, ffn_w_up.shape[2]), dt)

    def attend_prompt(q, k, v):
        return _moba_prompt(q, k, v, slopes)

    (y_prompt, k_prompt, v_prompt, wkv_prompt, shift_prompt, _unused_chunk_v,
     ffn_conv_prompt) = _trunk(x_prompt, attend_prompt, shift0, wkv0, conv0, prm, w, tile=512, long_seq=True)

    bs, ts, _ = x_sample.shape

    def attend_sample(q, k, v):
        o = _moba_sample(q.reshape(bs, ts, -1), k.reshape(bs, ts, -1), v.reshape(bs, ts, -1), cache_k, cache_v,
                         page_table, slopes)
        return o.reshape(bs * ts, -1)

    (y_sample, k_sample, v_sample, wkv_sample, shift_sample, chunk_v_sample,
     ffn_conv_sample) = _trunk(x_sample, attend_sample, state_shift, state_wkv, state_ffn_conv, prm, w,
                               tile=bs * ts, long_seq=False)

    return (y_prompt, y_sample, k_prompt, v_prompt, k_sample, v_sample, wkv_prompt, wkv_sample,
            shift_prompt, shift_sample, chunk_v_sample, ffn_conv_prompt, ffn_conv_sample)
```

```python
import functools

import jax
import jax.numpy as jnp
from jax import lax
from jax.experimental import pallas as pl
from jax.experimental.pallas import tpu as pltpu

F32 = jnp.float32
BF16 = jnp.bfloat16
HI = lax.Precision.HIGHEST

HEAD_DIM = 64
N_HEADS = 8
D_HEADS = N_HEADS * HEAD_DIM
MOBA_BLOCK = 256
MOBA_TOPK = 3
MOBA_GROUP = 4
SCALE = HEAD_DIM ** -0.5
D_DECAY_LORA = 64
D_AAA_LORA = 64
GMLP_GROUPS = 8
GMLP_CHUNK = 128
CONV_W = 3
NORM_EPS = 1e-6
LN_EPS = 1e-5
GN_EPS = 64e-5
LOG2E = 1.4426950408889634
MOBA_KAUG = 128
MOBA_VAUG = 80
NEG = -1e30
WKV_CHUNK = 64
V7X_VMEM_LIMIT = 56 * 1024 * 1024


def _cparams(sem, vmem=V7X_VMEM_LIMIT):
    return pltpu.CompilerParams(dimension_semantics=sem, vmem_limit_bytes=vmem)


def _mm(a, b, prec=None):
    return jnp.dot(a, b, precision=prec, preferred_element_type=F32)


def _mm_nt(a, b, prec=None):
    return lax.dot_general(a, b, (((1,), (1,)), ((), ())), precision=prec, preferred_element_type=F32)


def _split3(x):
    p1 = x.astype(BF16)
    r1 = x - p1.astype(F32)
    p2 = r1.astype(BF16)
    p3 = (r1 - p2.astype(F32)).astype(BF16)
    return p1, p2, p3


def _head_sums(x, gmat):
    return sum(_mm(p, gmat) for p in _split3(x))


def _rms(x, g):
    return x * lax.rsqrt(jnp.mean(x * x, axis=-1, keepdims=True) + NORM_EPS) * g


def _norm_matmul_kernel(x_ref, g_ref, w_ref, o_ref, xn_ref):
    @pl.when(pl.program_id(1) == 0)
    def _():
        xn_ref[...] = _rms(x_ref[...], g_ref[...]).astype(BF16)

    o_ref[...] = _mm(xn_ref[...], w_ref[...])


def _norm_matmul(x, g, w, tm, tn):
    m, kd = x.shape
    n = w.shape[1]
    return pl.pallas_call(
        _norm_matmul_kernel,
        grid=(m // tm, n // tn),
        in_specs=[pl.BlockSpec((tm, kd), lambda i, j: (i, 0)),
                  pl.BlockSpec((1, kd), lambda i, j: (0, 0)),
                  pl.BlockSpec((kd, tn), lambda i, j: (0, j))],
        out_specs=pl.BlockSpec((tm, tn), lambda i, j: (i, j)),
        out_shape=jax.ShapeDtypeStruct((m, n), F32),
        scratch_shapes=[pltpu.VMEM((tm, kd), BF16)],
        compiler_params=_cparams(("parallel", "arbitrary")),
        name="norm_matmul",
    )(x, g.reshape(1, kd), w)


def _shift_rows_kernel(x_ref, g_ref, prev_ref, w_ref, xn_ref, pp_ref):
    xn_ref[...] = _rms(x_ref[...], g_ref[...])
    pp_ref[...] = _mm(prev_ref[...].astype(BF16), w_ref[...])


def _shift_rows(x_last, g, prev_xn, w_b):
    b, kd = x_last.shape
    n = w_b.shape[1]
    bp = -(-b // 8) * 8
    pad = ((0, bp - b), (0, 0))
    xn, pp = pl.pallas_call(
        _shift_rows_kernel,
        out_shape=(jax.ShapeDtypeStruct((bp, kd), F32), jax.ShapeDtypeStruct((bp, n), F32)),
        compiler_params=_cparams(None),
        name="shift_rows",
    )(jnp.pad(x_last, pad), g.reshape(1, kd), jnp.pad(prev_xn, pad), w_b)
    return xn[:b], pp[:b]


def _kmeans_kernel(k_ref, o_ref):
    k = k_ref[0]
    nb = k.shape[0] // MOBA_BLOCK
    o_ref[0] = jnp.mean(k.reshape(nb, MOBA_BLOCK, HEAD_DIM), axis=1)


def _moba_prompt_kernel(slope_ref, qT_ref, km_ref, k_ref, vT_ref, oT_ref, sel_scr):
    i = pl.program_id(1)
    nb = km_ref.shape[1]
    blk = MOBA_BLOCK
    qf = qT_ref[0]
    gate = _mm(km_ref[0], qf, HI)
    nidx = lax.broadcasted_iota(jnp.int32, gate.shape, 0)
    g = jnp.where(nidx < i, gate, -jnp.inf)
    sel = jnp.zeros(gate.shape, F32)
    for _ in range(MOBA_TOPK):
        top = jnp.max(g, axis=0, keepdims=True)
        idx = jnp.min(jnp.where(g == top, nidx, nb), axis=0, keepdims=True)
        pick = (nidx == idx) & (top > -jnp.inf)
        sel = jnp.where(pick, 1.0, sel)
        g = jnp.where(pick, -jnp.inf, g)
    sel_scr[...] = sel

    slope2 = slope_ref[0] * LOG2E
    pieces = _split3(jnp.broadcast_to(slope2, (1, blk)))
    r16 = lax.broadcasted_iota(jnp.int32, (16, blk), 0)
    aug = jnp.zeros((16, blk), F32)
    for j, piece in enumerate(pieces):
        aug = jnp.where(r16 == j, piece.astype(F32), aug)
    qs = jnp.concatenate([(qf * (SCALE * LOG2E)).astype(BF16), aug.astype(BF16),
                          jnp.zeros((MOBA_KAUG - HEAD_DIM - 16, blk), BF16)], axis=0)
    ki = lax.broadcasted_iota(jnp.int32, (blk, blk), 0)
    qi = lax.broadcasted_iota(jnp.int32, (blk, blk), 1)

    own = pl.multiple_of(i * blk, blk)
    s = jnp.where(ki <= qi, _mm(k_ref[0, pl.ds(own, blk), :], qs), NEG)
    m0 = jnp.max(s, axis=0, keepdims=True)
    acc0 = _mm(vT_ref[0, :, pl.ds(own, blk)], jnp.exp2(s - m0).astype(BF16))

    grp = MOBA_GROUP
    gk = grp * blk

    def qk(g):
        start = pl.multiple_of(jnp.minimum(g, nb // grp - 1) * gk, gk)
        return _mm(k_ref[0, pl.ds(start, gk), :], qs)

    def block_max(s):
        return tuple(jnp.max(s[u * blk:(u + 1) * blk], axis=0, keepdims=True) for u in range(grp))

    def body(g, carry):
        m, acc, s, mbs = carry
        s_next = qk(g + 1)
        n0 = pl.multiple_of(g * grp, grp)
        picked = [sel_scr[pl.ds(n0 + u, 1), :] > 0.0 for u in range(grp)]
        shift = [slope2 * ((n0 + u - i) * blk).astype(F32) for u in range(grp)]
        m_new = m
        for u in range(grp):
            m_new = jnp.where(picked[u], jnp.maximum(m_new, mbs[u] + shift[u]), m_new)
        acc = jnp.exp2(m - m_new) * acc
        for u in range(grp):
            off = jnp.where(picked[u], m_new - shift[u], -NEG)
            p = jnp.exp2(s[u * blk:(u + 1) * blk] - off).astype(BF16)
            start = pl.multiple_of((n0 + u) * blk, blk)
            acc = acc + _mm(vT_ref[0, :, pl.ds(start, blk)], p)
        return m_new, acc, s_next, block_max(s_next)

    s0 = qk(0)
    _, acc, _, _ = lax.fori_loop(0, (i + grp - 1) // grp, body, (m0, acc0, s0, block_max(s0)))
    oT_ref[0] = acc[:HEAD_DIM] / acc[HEAD_DIM:HEAD_DIM + 1]


def _moba_prompt(q, k, v, slopes):
    t = q.shape[0]
    assert t % (MOBA_BLOCK * MOBA_GROUP) == 0
    nb = t // MOBA_BLOCK

    def heads_t(z):
        return z.reshape(t, N_HEADS, HEAD_DIM).transpose(1, 2, 0)

    k_hm = k.reshape(t, N_HEADS, HEAD_DIM).transpose(1, 0, 2)
    offs = jnp.broadcast_to((jnp.arange(t) % MOBA_BLOCK).astype(BF16)[None, :, None], (N_HEADS, t, 3))
    k_aug = jnp.concatenate([k_hm.astype(BF16), offs,
                             jnp.zeros((N_HEADS, t, MOBA_KAUG - HEAD_DIM - 3), BF16)], axis=-1)
    v_aug = jnp.concatenate([heads_t(v).astype(BF16), jnp.ones((N_HEADS, 1, t), BF16),
                             jnp.zeros((N_HEADS, MOBA_VAUG - HEAD_DIM - 1, t), BF16)], axis=1)
    kmean = pl.pallas_call(
        _kmeans_kernel,
        grid=(N_HEADS,),
        in_specs=[pl.BlockSpec((1, t, HEAD_DIM), lambda h: (h, 0, 0))],
        out_specs=pl.BlockSpec((1, nb, HEAD_DIM), lambda h: (h, 0, 0)),
        out_shape=jax.ShapeDtypeStruct((N_HEADS, nb, HEAD_DIM), F32),
        compiler_params=_cparams(("parallel",)),
        name="moba_kmeans",
    )(k_hm)
    oT = pl.pallas_call(
        _moba_prompt_kernel,
        grid=(N_HEADS, nb),
        in_specs=[pl.BlockSpec((1, 1, 1), lambda h, i: (h, 0, 0)),
                  pl.BlockSpec((1, HEAD_DIM, MOBA_BLOCK), lambda h, i: (h, 0, i)),
                  pl.BlockSpec((1, nb, HEAD_DIM), lambda h, i: (h, 0, 0)),
                  pl.BlockSpec((1, t, MOBA_KAUG), lambda h, i: (h, 0, 0)),
                  pl.BlockSpec((1, MOBA_VAUG, t), lambda h, i: (h, 0, 0))],
        out_specs=pl.BlockSpec((1, HEAD_DIM, MOBA_BLOCK), lambda h, i: (h, 0, i)),
        out_shape=jax.ShapeDtypeStruct((N_HEADS, HEAD_DIM, t), F32),
        scratch_shapes=[pltpu.VMEM((nb, MOBA_BLOCK), F32)],
        compiler_params=_cparams(("parallel", "arbitrary")),
        name="moba_prompt",
    )(slopes.reshape(N_HEADS, 1, 1), heads_t(q), kmean, k_aug, v_aug)
    return oT.transpose(2, 0, 1).reshape(t, D_HEADS)


GATE_PAGES = 16


def _moba_gate_kernel(pt_ref, q_ref, ck_ref, sel_ref, kbuf, km_scr, sem, *, n_pages):
    i = pl.program_id(0)
    nsteps = pl.num_programs(0)
    steps_per_seq = n_pages // GATE_PAGES
    g = i % steps_per_seq
    blk_pages = MOBA_BLOCK // ck_ref.shape[3]
    nb = n_pages // blk_pages

    def copies(step, slot):
        return [pltpu.make_async_copy(ck_ref.at[pt_ref[step * GATE_PAGES + c]], kbuf.at[slot, c], sem.at[slot])
                for c in range(GATE_PAGES)]

    @pl.when(i == 0)
    def _():
        for cp in copies(0, 0):
            cp.start()

    @pl.when(i + 1 < nsteps)
    def _():
        for cp in copies(i + 1, (i + 1) % 2):
            cp.start()

    slot = i % 2
    for cp in copies(i, slot):
        cp.wait()

    @pl.when(g == 0)
    def _():
        km_scr[...] = jnp.zeros(km_scr.shape, F32)

    lane = lax.broadcasted_iota(jnp.int32, km_scr.shape, 2)
    acc = km_scr[...]
    for b in range(GATE_PAGES // blk_pages):
        x = kbuf[slot, blk_pages * b]
        for j in range(1, blk_pages):
            x = x + kbuf[slot, blk_pages * b + j]
        acc = jnp.where(lane == g * (GATE_PAGES // blk_pages) + b, jnp.sum(x, axis=-1, keepdims=True), acc)
    km_scr[...] = acc

    @pl.when(g == steps_per_seq - 1)
    def _():
        km = acc * (1.0 / MOBA_BLOCK)
        gate = jnp.concatenate([_mm(q_ref[0, h], km[h], HI) for h in range(N_HEADS)], axis=0)
        bidx = lax.broadcasted_iota(jnp.int32, gate.shape, 1)
        gt = jnp.where(bidx < nb, gate, -jnp.inf)
        sel = jnp.zeros(gate.shape, F32)
        for _ in range(MOBA_TOPK):
            top = jnp.max(gt, axis=1, keepdims=True)
            idx = jnp.min(jnp.where(gt == top, bidx, nb), axis=1, keepdims=True)
            pick = (bidx == idx) & (top > -jnp.inf)
            sel = jnp.where(pick, 1.0, sel)
            gt = jnp.where(pick, -jnp.inf, gt)
        sel_ref[0] = sel


def _moba_attend_kernel(pg_ref, slope_ref, q_ref, kn_ref, vn_ref, dist_ref, ck_ref, cv_ref, o_ref,
                        kbuf, vbuf, sem, *, n_visits):
    i = pl.program_id(0)
    nsteps = pl.num_programs(0)
    page = ck_ref.shape[3]
    n_copies = n_visits * (MOBA_BLOCK // page)

    def copies(step, slot):
        h = step % N_HEADS
        out = []
        for c in range(n_copies):
            pg = pg_ref[step * n_copies + c]
            cols = pl.ds(c * page, page)
            out.append(pltpu.make_async_copy(ck_ref.at[pg, h], kbuf.at[slot, :, cols], sem.at[0, slot]))
            out.append(pltpu.make_async_copy(cv_ref.at[pg, h], vbuf.at[slot, :, cols], sem.at[1, slot]))
        return out

    @pl.when(i == 0)
    def _():
        for cp in copies(0, 0):
            cp.start()

    @pl.when(i + 1 < nsteps)
    def _():
        for cp in copies(i + 1, (i + 1) % 2):
            cp.start()

    slot = i % 2
    for cp in copies(i, slot):
        cp.wait()

    nq = q_ref.shape[2]
    slope = slope_ref[0]
    qs = (q_ref[0, 0] * SCALE).astype(BF16)
    ncol = n_visits * MOBA_BLOCK
    s = _mm(qs, kbuf[slot].astype(BF16)) - slope * dist_ref[0, 0]
    row = lax.broadcasted_iota(jnp.int32, (nq, ncol), 0)
    col = lax.broadcasted_iota(jnp.int32, (nq, ncol), 1)
    s = jnp.where(col // (ncol // nq) == row, s, NEG)
    nk = kn_ref.shape[2]
    dist_o = (lax.broadcasted_iota(jnp.int32, (nq, nk), 0)
              - lax.broadcasted_iota(jnp.int32, (nq, nk), 1)).astype(F32)
    s_o = jnp.where(dist_o >= 0.0, _mm_nt(qs, kn_ref[0, 0].astype(BF16)) - slope * dist_o, NEG)
    m = jnp.maximum(jnp.max(s, axis=1, keepdims=True), jnp.max(s_o, axis=1, keepdims=True))
    p = jnp.exp(s - m)
    p_o = jnp.exp(s_o - m)
    l = jnp.sum(p, axis=1, keepdims=True) + jnp.sum(p_o, axis=1, keepdims=True)
    acc = _mm_nt(p.astype(BF16), vbuf[slot].astype(BF16)) + _mm(p_o.astype(BF16), vn_ref[0, 0].astype(BF16))
    o_ref[0, 0] = acc / l


def _moba_sample(q, k, v, cache_k, cache_v, page_table, slopes):
    b, s, _ = q.shape
    n_pool, page, _, _ = cache_k.shape
    n_pages = page_table.shape[1]
    past = n_pages * page
    pages_per_blk = MOBA_BLOCK // page
    assert pages_per_blk == 2 and past % MOBA_BLOCK == 0
    nb = past // MOBA_BLOCK
    assert nb >= MOBA_TOPK
    pt = page_table.reshape(-1).astype(jnp.int32)

    ck = cache_k.transpose(0, 2, 3, 1)
    cv = cache_v.transpose(0, 2, 3, 1)

    nk = 16
    lanes = 128
    assert n_pages % GATE_PAGES == 0 and nb <= lanes
    steps_per_seq = n_pages // GATE_PAGES

    def heads(z, rows):
        z = jnp.pad(z, ((0, 0), (0, rows - s), (0, 0)))
        return z.reshape(b, rows, N_HEADS, HEAD_DIM).transpose(0, 2, 1, 3)

    sel = pl.pallas_call(
        functools.partial(_moba_gate_kernel, n_pages=n_pages),
        grid_spec=pltpu.PrefetchScalarGridSpec(
            num_scalar_prefetch=1,
            grid=(b * steps_per_seq,),
            in_specs=[pl.BlockSpec((1, N_HEADS, s, HEAD_DIM), lambda i, pt: (i // steps_per_seq, 0, 0, 0)),
                      pl.BlockSpec(memory_space=pl.ANY)],
            out_specs=pl.BlockSpec((1, N_HEADS * s, lanes), lambda i, pt: (i // steps_per_seq, 0, 0)),
            scratch_shapes=[pltpu.VMEM((2, GATE_PAGES, N_HEADS, HEAD_DIM, page), F32),
                            pltpu.VMEM((N_HEADS, HEAD_DIM, lanes), F32),
                            pltpu.SemaphoreType.DMA((2,))]),
        out_shape=jax.ShapeDtypeStruct((b, N_HEADS * s, lanes), F32),
        compiler_params=_cparams(("arbitrary",)),
        name="moba_gate",
    )(pt, heads(q, s), ck)

    n_visits = s * MOBA_TOPK
    blk = jnp.argsort(-sel[:, :, :nb], axis=2, stable=True)[:, :, :MOBA_TOPK]
    blk = blk.reshape(b, N_HEADS, n_visits)
    pages = page_table[jnp.arange(b)[:, None, None, None],
                       blk[..., None] * pages_per_blk + jnp.arange(pages_per_blk)]
    kpos = blk[..., None] * MOBA_BLOCK + jnp.arange(MOBA_BLOCK)
    qpos = past + jnp.arange(n_visits) // MOBA_TOPK
    dist = (qpos[None, None, :, None] - kpos).astype(F32).reshape(b, N_HEADS, 1, n_visits * MOBA_BLOCK)

    def bh_spec(r, w):
        return pl.BlockSpec((1, 1, r, w), lambda i, pg: (i // N_HEADS, i % N_HEADS, 0, 0))

    out = pl.pallas_call(
        functools.partial(_moba_attend_kernel, n_visits=n_visits),
        grid_spec=pltpu.PrefetchScalarGridSpec(
            num_scalar_prefetch=1,
            grid=(b * N_HEADS,),
            in_specs=[pl.BlockSpec((1, 1, 1), lambda i, pg: (i % N_HEADS, 0, 0)),
                      bh_spec(s, HEAD_DIM), bh_spec(nk, HEAD_DIM), bh_spec(nk, HEAD_DIM),
                      bh_spec(1, n_visits * MOBA_BLOCK),
                      pl.BlockSpec(memory_space=pl.ANY), pl.BlockSpec(memory_space=pl.ANY)],
            out_specs=bh_spec(s, HEAD_DIM),
            scratch_shapes=[pltpu.VMEM((2, HEAD_DIM, n_visits * MOBA_BLOCK), F32),
                            pltpu.VMEM((2, HEAD_DIM, n_visits * MOBA_BLOCK), F32),
                            pltpu.SemaphoreType.DMA((2, 2))]),
        out_shape=jax.ShapeDtypeStruct((b, N_HEADS, s, HEAD_DIM), F32),
        compiler_params=_cparams(("arbitrary",)),
        name="moba_attend",
    )(pages.reshape(-1).astype(jnp.int32), slopes.reshape(N_HEADS, 1, 1), heads(q, s), heads(k, nk), heads(v, nk),
      dist, ck, cv)
    return out.transpose(0, 2, 1, 3).reshape(b, s, D_HEADS)


def _softplus(z):
    return jnp.maximum(z, 0.0) + jnp.log1p(jnp.exp(-jnp.abs(z)))


def _rwkv_prep_kernel(pb_ref, first_ref, mu_ref, w0_ref, w2_ref, a0_ref, a2_ref, g2_ref, kk_ref, ka_ref,
                      rk_ref, gm_ref, r_o, lw_o, k_o, v_o, kk_o, b_o, g_o, bonus_o, *, seq_len):
    pb = pb_ref[...]
    tm = pb.shape[0]
    row = lax.broadcasted_iota(jnp.int32, (tm, 1), 0)
    if seq_len is None:
        starts, first = row == 0, first_ref[0]
    else:
        starts, first = (row % seq_len) == 0, first_ref[...]
    prev = jnp.where(starts, first, pltpu.roll(pb, 1, axis=0))
    xm = pb + mu_ref[...] * (prev - pb)
    d = D_HEADS
    o1, o2, o3 = 3 * d, 3 * d + D_DECAY_LORA, 3 * d + D_DECAY_LORA + D_AAA_LORA
    r, k, v = xm[:, 0:d], xm[:, d:2 * d], xm[:, 2 * d:o1]
    xw, xa, xg = xm[:, o1:o2], xm[:, o2:o3], xm[:, o3:]
    logw = -_softplus(-(w0_ref[...] + _mm(jnp.tanh(xw), w2_ref[...]))) - 0.5
    a = jax.nn.sigmoid(a0_ref[...] + _mm(xa, a2_ref[...]))
    gmat = gm_ref[...]
    kk = k * kk_ref[...]
    kk = kk / jnp.maximum(jnp.sqrt(_head_sums(kk * kk, gmat)), 1e-12)
    k = k * (1.0 + (a - 1.0) * ka_ref[...])
    r_o[...] = r
    lw_o[...] = -jnp.exp(logw)
    k_o[...] = k
    v_o[...] = v
    kk_o[...] = kk
    b_o[...] = kk * a
    g_o[...] = _mm(jax.nn.sigmoid(xg), g2_ref[...])
    bonus_o[...] = _head_sums(r * k * rk_ref[...], gmat) * v


def _rwkv_prep(pb, first, prm, gmat, tm, seq_len):
    m, n = pb.shape
    d = D_HEADS
    if seq_len is None:
        first_spec = pl.BlockSpec((1, 1, n), lambda i: (i, 0, 0))
    else:
        first_spec = pl.BlockSpec((tm, n), lambda i: (i, 0))

    def full(shape):
        return pl.BlockSpec(shape, lambda i: (0,) * len(shape))

    def vec(z):
        return z.reshape(1, -1)

    outs = pl.pallas_call(
        functools.partial(_rwkv_prep_kernel, seq_len=seq_len),
        grid=(m // tm,),
        in_specs=[pl.BlockSpec((tm, n), lambda i: (i, 0)), first_spec,
                  full((1, n)), full((1, d)), full((D_DECAY_LORA, d)), full((1, d)), full((D_AAA_LORA, d)),
                  full((n - 3 * d - D_DECAY_LORA - D_AAA_LORA, d)), full((1, d)), full((1, d)), full((1, d)),
                  full((d, d))],
        out_specs=[pl.BlockSpec((tm, d), lambda i: (i, 0))] * 8,
        out_shape=[jax.ShapeDtypeStruct((m, d), F32)] * 8,
        compiler_params=_cparams(("parallel",)),
        name="rwkv_prep",
    )(pb, first, vec(prm['rw_mu']), vec(prm['rw_w0']), prm['rw_w2'], vec(prm['rw_a0']), prm['rw_a2'],
      prm['rw_g2'], vec(prm['rw_k_k']), vec(prm['rw_k_a']), vec(prm['rw_r_k']), gmat)
    return outs


def _rwkv_scan_kernel(r_ref, lw_ref, k_ref, v_ref, kk_ref, b_ref, kkT_ref, lwT_ref, vT_ref, s0_ref,
                      y_ref, sfin_ref, s_scr, *, chunk, chunks_per_step):
    j = pl.program_id(1)

    @pl.when(j == 0)
    def _():
        s_scr[...] = s0_ref[0]

    step = chunk * chunks_per_step
    d = D_HEADS
    si = lax.broadcasted_iota(jnp.int32, (step, step), 0)
    sj = lax.broadcasted_iota(jnp.int32, (step, step), 1)
    same = (si // chunk) == (sj // chunk)
    lbd = jnp.where(same & (si >= sj), 1.0, 0.0).astype(BF16)
    ubd = jnp.where(same & (si <= sj), 1.0, 0.0).astype(BF16)
    lw = lw_ref[...]
    lwT = lwT_ref[0]
    cl = sum(_mm(lbd, p) for p in _split3(lw))
    clT = sum(_mm(p, ubd) for p in _split3(lwT))
    tots = [cl[(c + 1) * chunk - 1:(c + 1) * chunk, :] for c in range(chunks_per_step)]
    tot_full = jnp.concatenate([jnp.broadcast_to(t, (chunk, d)) for t in tots], axis=0)
    e_neg = jnp.exp(-cl)
    e_end = jnp.exp(tot_full - cl)
    b = b_ref[...]
    k = k_ref[...]
    rho = r_ref[...] * jnp.exp(cl)
    beta, kap = b * e_neg, k * e_neg
    b_end, k_end = (b * e_end).astype(BF16), (k * e_end).astype(BF16)
    alphaT = kkT_ref[0] * jnp.exp(clT - lwT)
    vT = vT_ref[0].astype(BF16)
    v = v_ref[...].astype(BF16)

    ii = lax.broadcasted_iota(jnp.int32, (chunk, chunk), 0)
    jj = lax.broadcasted_iota(jnp.int32, (chunk, chunk), 1)
    low, upper = ii >= jj, ii < jj
    hd = HEAD_DIM
    units = [(slice(c * chunk, (c + 1) * chunk), slice(h * hd, (h + 1) * hd))
             for c in range(chunks_per_step) for h in range(N_HEADS)]
    bk = [jnp.concatenate([beta[rs, cs], kap[rs, cs]], axis=0).astype(BF16) for rs, cs in units]
    aT = [alphaT[cs, rs] for rs, cs in units]
    g1 = [_mm(x, a.astype(BF16)) for x, a in zip(bk, aT)]
    lt = [jnp.where(upper, g[:chunk], 0.0).astype(BF16) for g in g1]
    akT = [jnp.where(upper, g[chunk:], 0.0).astype(BF16) for g in g1]
    g2 = [_mm_nt(rho[rs, cs].astype(BF16), x) for (rs, cs), x in zip(units, bk)]
    arb = [jnp.where(low, g[:, :chunk], 0.0).astype(BF16) for g in g2]
    ark = [jnp.where(low, g[:, chunk:], 0.0).astype(BF16) for g in g2]
    z = [jnp.concatenate([a, _mm(vT[cs, rs], x)], axis=0) for (rs, cs), a, x in zip(units, aT, akT)]
    z = [x - _mm(x.astype(BF16), l) for x, l in zip(z, lt)]
    pw = lt
    n = 2
    while n < chunk:
        pw = [_mm(p, p).astype(BF16) for p in pw]
        z = [x + _mm(x.astype(BF16), p) for x, p in zip(z, pw)]
        n *= 2
    pz = [(-x).astype(BF16) for x in z]
    g3 = [_mm_nt(a, p) for a, p in zip(arb, pz)]
    g4 = [_mm(p, b_end[rs, cs]) for (rs, cs), p in zip(units, pz)]
    yin = [g[:, hd:] + _mm(a, v[rs, cs]) for (rs, cs), g, a in zip(units, g3, ark)]
    vk = [_mm(vT[cs, rs], k_end[rs, cs]) for rs, cs in units]
    state = [s_scr[h] for h in range(N_HEADS)]
    for c in range(chunks_per_step):
        etot = jnp.exp(tots[c])
        for h in range(N_HEADS):
            u = c * N_HEADS + h
            rs, cs = units[u]
            s = state[h]
            sb = s.astype(BF16)
            y_ref[rs, cs] = _mm_nt((rho[rs, cs] + g3[u][:, :hd]).astype(BF16), sb) + yin[u]
            state[h] = s * etot[:, cs] + _mm(sb, g4[u][:hd].astype(BF16)) + g4[u][hd:] + vk[u]
    for h in range(N_HEADS):
        s_scr[h] = state[h]

    @pl.when(j == pl.num_programs(1) - 1)
    def _():
        sfin_ref[0] = s_scr[...]


def _rwkv_scan(r, lw, k, v, kk, b, s0, seq_len, chunks_per_step):
    m, d = r.shape
    nseq = m // seq_len
    step = WKV_CHUNK * chunks_per_step
    nsteps = seq_len // step

    def tr(z):
        return z.reshape(nseq, seq_len, d).transpose(0, 2, 1)

    tm_spec = pl.BlockSpec((step, d), lambda i, j: (i * nsteps + j, 0))
    fm_spec = pl.BlockSpec((1, d, step), lambda i, j: (i, 0, j))
    st_spec = pl.BlockSpec((1, N_HEADS, HEAD_DIM, HEAD_DIM), lambda i, j: (i, 0, 0, 0))
    return pl.pallas_call(
        functools.partial(_rwkv_scan_kernel, chunk=WKV_CHUNK, chunks_per_step=chunks_per_step),
        grid=(nseq, nsteps),
        in_specs=[tm_spec] * 6 + [fm_spec] * 3 + [st_spec],
        out_specs=[tm_spec, st_spec],
        out_shape=[jax.ShapeDtypeStruct((m, d), F32),
                   jax.ShapeDtypeStruct((nseq, N_HEADS, HEAD_DIM, HEAD_DIM), F32)],
        scratch_shapes=[pltpu.VMEM((N_HEADS, HEAD_DIM, HEAD_DIM), F32)],
        compiler_params=_cparams(("parallel", "arbitrary")),
        name="rwkv_scan",
    )(r, lw, k, v, kk, b, tr(kk), tr(lw), tr(v), s0)


def _l0_out_kernel(y_ref, bonus_ref, g_ref, oa_ref, h_ref, gnw_ref, gnb_ref, gm_ref, wa_ref, wb_ref,
                   gpost_ref, o_ref):
    y = y_ref[...]
    gmat = gm_ref[...]
    inv = 1.0 / HEAD_DIM
    yc = y - _head_sums(y, gmat) * inv
    yn = yc * lax.rsqrt(_head_sums(yc * yc, gmat) * inv + GN_EPS) * gnw_ref[...] + gnb_ref[...]
    ob = (yn + bonus_ref[...]) * g_ref[...]
    mix = _mm(oa_ref[...].astype(BF16), wa_ref[...]) + _mm(ob.astype(BF16), wb_ref[...])
    o_ref[...] = h_ref[...] + _rms(mix, gpost_ref[...])


def _l0_out(y, bonus, g, oa, h, prm, gmat, w_out, tm):
    m, dm = h.shape
    d = D_HEADS

    def rowblk(w):
        return pl.BlockSpec((tm, w), lambda i: (i, 0))

    def full(shape):
        return pl.BlockSpec(shape, lambda i: (0,) * len(shape))

    return pl.pallas_call(
        _l0_out_kernel,
        grid=(m // tm,),
        in_specs=[rowblk(d), rowblk(d), rowblk(d), rowblk(d), rowblk(dm), full((1, d)), full((1, d)),
                  full((d, d)), full((d, dm)), full((d, dm)), full((1, dm))],
        out_specs=rowblk(dm),
        out_shape=jax.ShapeDtypeStruct((m, dm), F32),
        compiler_params=_cparams(("parallel",)),
        name="l0_out",
    )(y, bonus, g, oa, h, prm['rw_gn_w'].reshape(1, d), prm['rw_gn_b'].reshape(1, d), gmat,
      w_out[:d], w_out[d:], prm['l0_norm_post'].reshape(1, dm))


FFN_COLS = 256
FFN_HALO = 16


def _ffn_kernel(h_ref, halo_ref, gpre_ref, wg_ref, wv_ref, cwg_ref, cwv_ref, cbg_ref, cbv_ref, pg_ref, pv_ref,
                wd_ref, gpost_ref, o_ref, xn_scr, acc_scr, *, seq_len):
    i = pl.program_id(0)
    c = pl.program_id(1)
    tm = h_ref.shape[0]

    @pl.when(c == 0)
    def _():
        xn_scr[0:tm] = _rms(h_ref[...], gpre_ref[...]).astype(BF16)
        if seq_len is None:
            xn_scr[tm:tm + FFN_HALO] = _rms(halo_ref[...], gpre_ref[...]).astype(BF16)
        acc_scr[...] = jnp.zeros(acc_scr.shape, F32)

    row = lax.broadcasted_iota(jnp.int32, (tm, 1), 0)
    pos = row if seq_len is None else row % seq_len

    def conv(w_ref, cw_ref, cb_ref, p_ref):
        up = _mm(xn_scr[...], w_ref[...])
        u = up[:tm]
        m1 = pltpu.roll(u, 1, axis=0)
        m2 = pltpu.roll(u, 2, axis=0)
        if seq_len is None:
            last = tm + FFN_HALO
            p0 = jnp.where(i == 0, p_ref[6:7], up[last - 2:last - 1])
            p1 = jnp.where(i == 0, p_ref[7:8], up[last - 1:last])
            m1 = jnp.concatenate([jnp.where(pos[:8] == 0, p1, m1[:8]), m1[8:]], axis=0)
            m2 = jnp.concatenate([jnp.where(pos[:8] == 0, p0, jnp.where(pos[:8] == 1, p1, m2[:8])), m2[8:]], axis=0)
        else:
            m1 = jnp.where(pos == 0, p_ref[1], m1)
            m2 = jnp.where(pos == 0, p_ref[0], jnp.where(pos == 1, p_ref[1], m2))
        return cb_ref[...] + cw_ref[0:1] * m2 + cw_ref[1:2] * m1 + cw_ref[2:3] * u

    gate = conv(wg_ref, cwg_ref, cbg_ref, pg_ref)
    val = conv(wv_ref, cwv_ref, cbv_ref, pv_ref)
    acc_scr[...] += _mm((jax.nn.gelu(gate) * val).astype(BF16), wd_ref[...])

    @pl.when(c == pl.num_programs(1) - 1)
    def _():
        o_ref[...] = h_ref[...] + _rms(acc_scr[...], gpost_ref[...])


def _ffn(h, prev_rows, g_pre, w_up, conv_w, conv_b, w_down, g_post, tm, seq_len):
    m, dm = h.shape
    dff = w_down.shape[0]
    n2 = 2 * dff
    assert dff % FFN_COLS == 0
    nc = dff // FFN_COLS
    if seq_len is None:
        prev = jnp.pad(prev_rows[0], ((6, 0), (0, 0)))

        def p_spec(off):
            return pl.BlockSpec((8, FFN_COLS), lambda i, c: (0, off + c))

        halo_spec = pl.BlockSpec((FFN_HALO, dm), lambda i, c: (jnp.maximum(i * (tm // FFN_HALO) - 1, 0), 0))
        xn_rows = tm + FFN_HALO
    else:
        prev = jnp.stack([jnp.repeat(prev_rows[:, 0], seq_len, axis=0), jnp.repeat(prev_rows[:, 1], seq_len, axis=0)])

        def p_spec(off):
            return pl.BlockSpec((2, tm, FFN_COLS), lambda i, c: (0, i, off + c))

        halo_spec = pl.BlockSpec((FFN_HALO, dm), lambda i, c: (0, 0))
        xn_rows = tm

    def col_spec(rows, off):
        return pl.BlockSpec((rows, FFN_COLS), lambda i, c: (0, off + c))

    def full(shape):
        return pl.BlockSpec(shape, lambda i, c: (0,) * len(shape))

    cb = conv_b.reshape(1, n2)
    return pl.pallas_call(
        functools.partial(_ffn_kernel, seq_len=seq_len),
        grid=(m // tm, nc),
        in_specs=[pl.BlockSpec((tm, dm), lambda i, c: (i, 0)), halo_spec, full((1, dm)),
                  col_spec(dm, 0), col_spec(dm, nc), col_spec(CONV_W, 0), col_spec(CONV_W, nc),
                  col_spec(1, 0), col_spec(1, nc), p_spec(0), p_spec(nc),
                  pl.BlockSpec((FFN_COLS, dm), lambda i, c: (c, 0)), full((1, dm))],
        out_specs=pl.BlockSpec((tm, dm), lambda i, c: (i, 0)),
        out_shape=jax.ShapeDtypeStruct((m, dm), F32),
        scratch_shapes=[pltpu.VMEM((xn_rows, dm), BF16), pltpu.VMEM((tm, dm), F32)],
        compiler_params=_cparams(("parallel", "arbitrary")),
        name="ffn",
    )(h, h, g_pre.reshape(1, dm), w_up, w_up, conv_w, conv_w, cb, cb, prev, prev, w_down, g_post.reshape(1, dm))


def _gmlp_kernel(h_ref, gpre_ref, win_ref, lnw_ref, lnb_ref, wmix_ref, bias_ref, wout_ref, gpost_ref,
                 o_ref, cv_ref, gated_scr):
    h = h_ref[...]
    tm = h.shape[0]
    dc = wout_ref.shape[0]
    rm = wmix_ref.shape[1]
    gw = dc // GMLP_GROUPS
    z = jax.nn.gelu(_mm(_rms(h, gpre_ref[...]).astype(BF16), win_ref[...]))
    v = z[:, dc:]
    vc = v - jnp.mean(v, axis=-1, keepdims=True)
    vn = vc * lax.rsqrt(jnp.mean(vc * vc, axis=-1, keepdims=True) + LN_EPS) * lnw_ref[...] + lnb_ref[...]
    cv_ref[...] = vn
    for s in range(tm // rm):
        rows = slice(s * rm, (s + 1) * rm)
        for g in range(GMLP_GROUPS):
            cols = slice(g * gw, (g + 1) * gw)
            mixed = _mm(wmix_ref[g], vn[rows, cols].astype(BF16)) + bias_ref[:, cols]
            gated_scr[rows, cols] = (z[rows, cols] * mixed).astype(BF16)
    o_ref[...] = h + _rms(_mm(gated_scr[...], wout_ref[...]), gpost_ref[...])


def _gmlp(h, g_pre, w_in, ln_w, ln_b, wmix, bias, w_out, g_post, tm):
    m, dm = h.shape
    dc = w_out.shape[0]
    rm = wmix.shape[1]

    def full(shape):
        return pl.BlockSpec(shape, lambda i: (0,) * len(shape))

    return pl.pallas_call(
        _gmlp_kernel,
        grid=(m // tm,),
        in_specs=[pl.BlockSpec((tm, dm), lambda i: (i, 0)), full((1, dm)), full((dm, 2 * dc)), full((1, dc)),
                  full((1, dc)), full((GMLP_GROUPS, rm, rm)), full((rm, dc)), full((dc, dm)), full((1, dm))],
        out_specs=[pl.BlockSpec((tm, dm), lambda i: (i, 0)), pl.BlockSpec((tm, dc), lambda i: (i, 0))],
        out_shape=[jax.ShapeDtypeStruct((m, dm), F32), jax.ShapeDtypeStruct((m, dc), F32)],
        scratch_shapes=[pltpu.VMEM((tm, dc), BF16)],
        compiler_params=_cparams(("parallel",)),
        name="gmlp",
    )(h, g_pre.reshape(1, dm), w_in, ln_w.reshape(1, dc), ln_b.reshape(1, dc), wmix, bias, w_out,
      g_post.reshape(1, dm))


def _trunk(x, attend, prev_xn, wkv0, conv0, prm, w, *, tile, long_seq):
    nseq, t, dm = x.shape
    m = nseq * t
    h = x.reshape(m, dm)
    seq_len = None if long_seq else t

    g0 = prm['l0_norm_pre']
    qkv = _norm_matmul(h, g0, w['w_qkv'], tile, 512)
    pb = _norm_matmul(h, g0, w['w_b'], tile, 256)
    shift_new, prev_proj = _shift_rows(x[:, -1], g0, prev_xn, w['w_b'])
    q, k_new, v_new = qkv[:, :D_HEADS], qkv[:, D_HEADS:2 * D_HEADS], qkv[:, 2 * D_HEADS:]
    if long_seq:
        first = jnp.concatenate([prev_proj, pb[tile - 1:m - 1:tile]], axis=0).reshape(m // tile, 1, -1)
    else:
        first = jnp.repeat(prev_proj, t, axis=0)
    r, lw, kmod, vv, kk, bb, gate, bonus = _rwkv_prep(pb, first, prm, w['gmat'], tile, seq_len)
    o_a = attend(q, k_new, v_new)

    t_pad = -(-t // WKV_CHUNK) * WKV_CHUNK

    def padt(z):
        if t_pad == t:
            return z
        return jnp.pad(z.reshape(nseq, t, -1), ((0, 0), (0, t_pad - t), (0, 0))).reshape(nseq * t_pad, -1)

    y, wkv_new = _rwkv_scan(padt(r), padt(lw), padt(kmod), padt(vv), padt(kk), padt(bb), wkv0, t_pad,
                            4 if t_pad % (4 * WKV_CHUNK) == 0 else 1)
    if t_pad != t:
        y = y.reshape(nseq, t_pad, -1)[:, :t].reshape(m, -1)
    h = _l0_out(y, bonus, gate, o_a, h, prm, w['gmat'], w['w_out0'], tile)

    conv_rows = []
    ffn_tile = min(m, 1024)

    def ffn(h, layer):
        g_pre = prm['ffn_norm_pre'][layer]
        tail = h.reshape(nseq, t, dm)[:, t - min(t, 8):].reshape(-1, dm)
        up_tail = _norm_matmul(tail, g_pre, w['w_up'][layer], tail.shape[0], 512)
        conv_rows.append(up_tail.reshape(nseq, min(t, 8), -1)[:, -(CONV_W - 1):])
        return _ffn(h, conv0[layer], g_pre, w['w_up'][layer], prm['ffn_conv_w'][layer], prm['ffn_conv_b'][layer],
                    w['w_down'][layer], prm['ffn_norm_post'][layer], ffn_tile, seq_len)

    h = ffn(h, 0)

    if long_seq:
        wmix, bias = w['wmix'], w['gbias']
    else:
        eye = jnp.eye(tile // t, dtype=F32)
        wmix = jnp.einsum('ab,gij->gaibj', eye, w['wmix'][:, :t, :t].astype(F32))
        wmix = wmix.reshape(GMLP_GROUPS, tile, tile).astype(BF16)
        bias = jnp.tile(w['gbias'][:t], (tile // t, 1))
    h, chunk_v = _gmlp(h, prm['l1_norm_pre'], w['w_in1'], prm['gm_ln_w'], prm['gm_ln_b'], wmix, bias,
                       w['w_out1'], prm['l1_norm_post'], tile)
    h = ffn(h, 1)
    return (h.reshape(nseq, t, dm), k_new.reshape(nseq, t, N_HEADS, HEAD_DIM),
            v_new.reshape(nseq, t, N_HEADS, HEAD_DIM), wkv_new, shift_new, chunk_v.reshape(nseq, t, -1),
            jnp.stack(conv_rows))


def kernel(x_prompt, x_sample, cache_k, cache_v, state_wkv, state_shift, state_ffn_conv, page_table,
           l0_norm_pre, l0_norm_post, l0_w_in, rw_mu, rw_w0, rw_w2, rw_a0, rw_a2, rw_g2, rw_k_k, rw_k_a,
           rw_r_k, rw_gn_w, rw_gn_b, l0_w_out, l1_norm_pre, l1_norm_post, l1_w_in, gm_ln_w, gm_ln_b,
           gm_ws, gm_bs, l1_w_out, ffn_norm_pre, ffn_norm_post, ffn_w_up, ffn_conv_w, ffn_conv_b,
           ffn_w_down):
    prm = dict(l0_norm_pre=l0_norm_pre, l0_norm_post=l0_norm_post, rw_mu=rw_mu, rw_w0=rw_w0, rw_w2=rw_w2,
               rw_a0=rw_a0, rw_a2=rw_a2, rw_g2=rw_g2, rw_k_k=rw_k_k, rw_k_a=rw_k_a, rw_r_k=rw_r_k,
               rw_gn_w=rw_gn_w, rw_gn_b=rw_gn_b, l1_norm_pre=l1_norm_pre, l1_norm_post=l1_norm_post,
               gm_ln_w=gm_ln_w, gm_ln_b=gm_ln_b, ffn_norm_pre=ffn_norm_pre, ffn_norm_post=ffn_norm_post,
               ffn_conv_w=ffn_conv_w, ffn_conv_b=ffn_conv_b)
    d_a = D_HEADS
    dc = l1_w_out.shape[0]
    tri = jnp.tril(jnp.ones((GMLP_CHUNK, GMLP_CHUNK), dtype=bool))
    hd = jnp.arange(D_HEADS) // HEAD_DIM
    w = dict(
        w_qkv=l0_w_in[:, :3 * d_a].astype(BF16),
        w_b=l0_w_in[:, 3 * d_a:].astype(BF16),
        w_out0=l0_w_out.astype(BF16),
        w_up=ffn_w_up.astype(BF16),
        w_down=ffn_w_down.astype(BF16),
        w_in1=l1_w_in.astype(BF16),
        w_out1=l1_w_out.astype(BF16),
        wmix=jnp.where(tri[None], gm_ws, 0.0).astype(BF16),
        gbias=jnp.repeat(gm_bs.T, dc // GMLP_GROUPS, axis=1),
        gmat=(hd[:, None] == hd[None, :]).astype(BF16),
    )
    slopes = jnp.exp2(-8.0 * jnp.arange(1, N_HEADS + 1, dtype=F32) / N_HEADS)

    bp, tp, dm = x_prompt.shape
    assert bp == 1
    dt = x_prompt.dtype
    depth = ffn_w_up.shape[0]
    wkv0 = jnp.zeros((bp, N_HEADS, HEAD_DIM, HEAD_DIM), dt)
    shift0 = jnp.zeros((bp, dm), dt)
    conv0 = jnp.zeros((depth, bp, CONV_W - 1, ffn_w_up.shape[2]), dt)

    def attend_prompt(q, k, v):
        return _moba_prompt(q, k, v, slopes)

    (y_prompt, k_prompt, v_prompt, wkv_prompt, shift_prompt, _unused_chunk_v,
     ffn_conv_prompt) = _trunk(x_prompt, attend_prompt, shift0, wkv0, conv0, prm, w, tile=512, long_seq=True)

    bs, ts, _ = x_sample.shape

    def attend_sample(q, k, v):
        o = _moba_sample(q.reshape(bs, ts, -1), k.reshape(bs, ts, -1), v.reshape(bs, ts, -1), cache_k, cache_v,
                         page_table, slopes)
        return o.reshape(bs * ts, -1)

    (y_sample, k_sample, v_sample, wkv_sample, shift_sample, chunk_v_sample,
     ffn_conv_sample) = _trunk(x_sample, attend_sample, state_shift, state_wkv, state_ffn_conv, prm, w,
                               tile=bs * ts, long_seq=False)

    return (y_prompt, y_sample, k_prompt, v_prompt, k_sample, v_sample, wkv_prompt, wkv_sample,
            shift_prompt, shift_sample, chunk_v_sample, ffn_conv_prompt, ffn_conv_sample)
```

```python
import functools

import jax
import jax.numpy as jnp
from jax import lax
from jax.experimental import pallas as pl
from jax.experimental.pallas import tpu as pltpu

F32 = jnp.float32
BF16 = jnp.bfloat16
HI = lax.Precision.HIGHEST

HEAD_DIM = 64
N_HEADS = 8
D_HEADS = N_HEADS * HEAD_DIM
MOBA_BLOCK = 256
MOBA_TOPK = 3
MOBA_GROUP = 4
SCALE = HEAD_DIM ** -0.5
D_DECAY_LORA = 64
D_AAA_LORA = 64
GMLP_GROUPS = 8
GMLP_CHUNK = 128
CONV_W = 3
NORM_EPS = 1e-6
LN_EPS = 1e-5
GN_EPS = 64e-5
LOG2E = 1.4426950408889634
MOBA_KAUG = 128
MOBA_VAUG = 80
NEG = -1e30
WKV_CHUNK = 64
V7X_VMEM_LIMIT = 56 * 1024 * 1024


def _cparams(sem, vmem=V7X_VMEM_LIMIT):
    return pltpu.CompilerParams(dimension_semantics=sem, vmem_limit_bytes=vmem)


def _mm(a, b, prec=None):
    return jnp.dot(a, b, precision=prec, preferred_element_type=F32)


def _mm_nt(a, b, prec=None):
    return lax.dot_general(a, b, (((1,), (1,)), ((), ())), precision=prec, preferred_element_type=F32)


def _split3(x):
    p1 = x.astype(BF16)
    r1 = x - p1.astype(F32)
    p2 = r1.astype(BF16)
    p3 = (r1 - p2.astype(F32)).astype(BF16)
    return p1, p2, p3


def _head_sums(x, gmat):
    return sum(_mm(p, gmat) for p in _split3(x))


def _rms(x, g):
    return x * lax.rsqrt(jnp.mean(x * x, axis=-1, keepdims=True) + NORM_EPS) * g


def _norm_matmul_kernel(x_ref, g_ref, w_ref, o_ref, xn_ref):
    @pl.when(pl.program_id(1) == 0)
    def _():
        xn_ref[...] = _rms(x_ref[...], g_ref[...]).astype(BF16)

    o_ref[...] = _mm(xn_ref[...], w_ref[...])


def _norm_matmul(x, g, w, tm, tn):
    m, kd = x.shape
    n = w.shape[1]
    return pl.pallas_call(
        _norm_matmul_kernel,
        grid=(m // tm, n // tn),
        in_specs=[pl.BlockSpec((tm, kd), lambda i, j: (i, 0)),
                  pl.BlockSpec((1, kd), lambda i, j: (0, 0)),
                  pl.BlockSpec((kd, tn), lambda i, j: (0, j))],
        out_specs=pl.BlockSpec((tm, tn), lambda i, j: (i, j)),
        out_shape=jax.ShapeDtypeStruct((m, n), F32),
        scratch_shapes=[pltpu.VMEM((tm, kd), BF16)],
        compiler_params=_cparams(("parallel", "arbitrary")),
        name="norm_matmul",
    )(x, g.reshape(1, kd), w)


def _norm_matmul_split_kernel(x_ref, g_ref, w_ref, *refs):
    outs, xn_ref = refs[:-1], refs[-1]
    j = pl.program_id(1)

    @pl.when(j == 0)
    def _():
        xn_ref[...] = _rms(x_ref[...], g_ref[...]).astype(BF16)

    y = _mm(xn_ref[...], w_ref[...])
    for n, o_ref in enumerate(outs):
        @pl.when(j == n)
        def _(o_ref=o_ref):
            o_ref[...] = y


def _norm_matmul_split(x, g, w, tm, n_out):
    m, kd = x.shape
    tn = w.shape[1] // n_out
    return pl.pallas_call(
        _norm_matmul_split_kernel,
        grid=(m // tm, n_out),
        in_specs=[pl.BlockSpec((tm, kd), lambda i, j: (i, 0)),
                  pl.BlockSpec((1, kd), lambda i, j: (0, 0)),
                  pl.BlockSpec((kd, tn), lambda i, j: (0, j))],
        out_specs=[pl.BlockSpec((tm, tn), lambda i, j: (i, 0))] * n_out,
        out_shape=[jax.ShapeDtypeStruct((m, tn), F32)] * n_out,
        scratch_shapes=[pltpu.VMEM((tm, kd), BF16)],
        compiler_params=_cparams(("parallel", "arbitrary")),
        name="norm_matmul_split",
    )(x, g.reshape(1, kd), w)


def _shift_rows_kernel(x_ref, g_ref, prev_ref, w_ref, xn_ref, pp_ref):
    xn_ref[...] = _rms(x_ref[...], g_ref[...])
    pp_ref[...] = _mm(prev_ref[...].astype(BF16), w_ref[...])


def _shift_rows(x_last, g, prev_xn, w_b):
    b, kd = x_last.shape
    n = w_b.shape[1]
    bp = -(-b // 8) * 8
    pad = ((0, bp - b), (0, 0))
    xn, pp = pl.pallas_call(
        _shift_rows_kernel,
        out_shape=(jax.ShapeDtypeStruct((bp, kd), F32), jax.ShapeDtypeStruct((bp, n), F32)),
        compiler_params=_cparams(None),
        name="shift_rows",
    )(jnp.pad(x_last, pad), g.reshape(1, kd), jnp.pad(prev_xn, pad), w_b)
    return xn[:b], pp[:b]


def _kmeans_kernel(k_ref, o_ref):
    k = k_ref[0]
    nb = k.shape[0] // MOBA_BLOCK
    o_ref[0] = jnp.mean(k.reshape(nb, MOBA_BLOCK, HEAD_DIM), axis=1)


def _moba_prompt_kernel(slope_ref, qT_ref, km_ref, k_ref, vT_ref, oT_ref, sel_scr):
    i = pl.program_id(1)
    nb = km_ref.shape[1]
    blk = MOBA_BLOCK
    qf = qT_ref[0]
    gate = _mm(km_ref[0], qf, HI)
    nidx = lax.broadcasted_iota(jnp.int32, gate.shape, 0)
    g = jnp.where(nidx < i, gate, -jnp.inf)
    sel = jnp.zeros(gate.shape, F32)
    for _ in range(MOBA_TOPK):
        top = jnp.max(g, axis=0, keepdims=True)
        idx = jnp.min(jnp.where(g == top, nidx, nb), axis=0, keepdims=True)
        pick = (nidx == idx) & (top > -jnp.inf)
        sel = jnp.where(pick, 1.0, sel)
        g = jnp.where(pick, -jnp.inf, g)
    sel_scr[...] = sel

    slope2 = slope_ref[0] * LOG2E
    pieces = _split3(jnp.broadcast_to(slope2, (1, blk)))
    r16 = lax.broadcasted_iota(jnp.int32, (16, blk), 0)
    aug = jnp.zeros((16, blk), F32)
    for j, piece in enumerate(pieces):
        aug = jnp.where(r16 == j, piece.astype(F32), aug)
    qs = jnp.concatenate([(qf * (SCALE * LOG2E)).astype(BF16), aug.astype(BF16),
                          jnp.zeros((MOBA_KAUG - HEAD_DIM - 16, blk), BF16)], axis=0)
    ki = lax.broadcasted_iota(jnp.int32, (blk, blk), 0)
    qi = lax.broadcasted_iota(jnp.int32, (blk, blk), 1)

    own = pl.multiple_of(i * blk, blk)
    s = jnp.where(ki <= qi, _mm(k_ref[0, pl.ds(own, blk), :], qs), NEG)
    m0 = jnp.max(s, axis=0, keepdims=True)
    acc0 = _mm(vT_ref[0, :, pl.ds(own, blk)], jnp.exp2(s - m0).astype(BF16))

    grp = MOBA_GROUP
    gk = grp * blk

    def qk(g):
        start = pl.multiple_of(jnp.minimum(g, nb // grp - 1) * gk, gk)
        return _mm(k_ref[0, pl.ds(start, gk), :], qs)

    def block_max(s):
        return tuple(jnp.max(s[u * blk:(u + 1) * blk], axis=0, keepdims=True) for u in range(grp))

    def body(g, carry):
        m, acc, s, mbs = carry
        s_next = qk(g + 1)
        n0 = pl.multiple_of(g * grp, grp)
        picked = [sel_scr[pl.ds(n0 + u, 1), :] > 0.0 for u in range(grp)]
        shift = [slope2 * ((n0 + u - i) * blk).astype(F32) for u in range(grp)]
        m_new = m
        for u in range(grp):
            m_new = jnp.where(picked[u], jnp.maximum(m_new, mbs[u] + shift[u]), m_new)
        acc = jnp.exp2(m - m_new) * acc
        for u in range(grp):
            off = jnp.where(picked[u], m_new - shift[u], -NEG)
            p = jnp.exp2(s[u * blk:(u + 1) * blk] - off).astype(BF16)
            start = pl.multiple_of((n0 + u) * blk, blk)
            acc = acc + _mm(vT_ref[0, :, pl.ds(start, blk)], p)
        return m_new, acc, s_next, block_max(s_next)

    s0 = qk(0)
    _, acc, _, _ = lax.fori_loop(0, (i + grp - 1) // grp, body, (m0, acc0, s0, block_max(s0)))
    oT_ref[0] = acc[:HEAD_DIM] / acc[HEAD_DIM:HEAD_DIM + 1]


def _moba_prompt(q, k, v, slopes):
    t = q.shape[0]
    assert t % (MOBA_BLOCK * MOBA_GROUP) == 0
    nb = t // MOBA_BLOCK

    def heads_t(z):
        return z.reshape(t, N_HEADS, HEAD_DIM).transpose(1, 2, 0)

    k_hm = k.reshape(t, N_HEADS, HEAD_DIM).transpose(1, 0, 2)
    offs = jnp.broadcast_to((jnp.arange(t) % MOBA_BLOCK).astype(BF16)[None, :, None], (N_HEADS, t, 3))
    k_aug = jnp.concatenate([k_hm.astype(BF16), offs,
                             jnp.zeros((N_HEADS, t, MOBA_KAUG - HEAD_DIM - 3), BF16)], axis=-1)
    v_aug = jnp.concatenate([heads_t(v).astype(BF16), jnp.ones((N_HEADS, 1, t), BF16),
                             jnp.zeros((N_HEADS, MOBA_VAUG - HEAD_DIM - 1, t), BF16)], axis=1)
    kmean = pl.pallas_call(
        _kmeans_kernel,
        grid=(N_HEADS,),
        in_specs=[pl.BlockSpec((1, t, HEAD_DIM), lambda h: (h, 0, 0))],
        out_specs=pl.BlockSpec((1, nb, HEAD_DIM), lambda h: (h, 0, 0)),
        out_shape=jax.ShapeDtypeStruct((N_HEADS, nb, HEAD_DIM), F32),
        compiler_params=_cparams(("parallel",)),
        name="moba_kmeans",
    )(k_hm)
    oT = pl.pallas_call(
        _moba_prompt_kernel,
        grid=(N_HEADS, nb),
        in_specs=[pl.BlockSpec((1, 1, 1), lambda h, i: (h, 0, 0)),
                  pl.BlockSpec((1, HEAD_DIM, MOBA_BLOCK), lambda h, i: (h, 0, i)),
                  pl.BlockSpec((1, nb, HEAD_DIM), lambda h, i: (h, 0, 0)),
                  pl.BlockSpec((1, t, MOBA_KAUG), lambda h, i: (h, 0, 0)),
                  pl.BlockSpec((1, MOBA_VAUG, t), lambda h, i: (h, 0, 0))],
        out_specs=pl.BlockSpec((1, HEAD_DIM, MOBA_BLOCK), lambda h, i: (h, 0, i)),
        out_shape=jax.ShapeDtypeStruct((N_HEADS, HEAD_DIM, t), F32),
        scratch_shapes=[pltpu.VMEM((nb, MOBA_BLOCK), F32)],
        compiler_params=_cparams(("parallel", "arbitrary")),
        name="moba_prompt",
    )(slopes.reshape(N_HEADS, 1, 1), heads_t(q), kmean, k_aug, v_aug)
    return oT.transpose(2, 0, 1).reshape(t, D_HEADS)


GATE_PAGES = 16


def _moba_gate_kernel(pt_ref, q_ref, ck_ref, sel_ref, kbuf, km_scr, sem, *, n_pages):
    i = pl.program_id(0)
    nsteps = pl.num_programs(0)
    steps_per_seq = n_pages // GATE_PAGES
    g = i % steps_per_seq
    blk_pages = MOBA_BLOCK // ck_ref.shape[3]
    nb = n_pages // blk_pages

    def copies(step, slot):
        return [pltpu.make_async_copy(ck_ref.at[pt_ref[step * GATE_PAGES + c]], kbuf.at[slot, c], sem.at[slot])
                for c in range(GATE_PAGES)]

    @pl.when(i == 0)
    def _():
        for cp in copies(0, 0):
            cp.start()

    @pl.when(i + 1 < nsteps)
    def _():
        for cp in copies(i + 1, (i + 1) % 2):
            cp.start()

    slot = i % 2
    for cp in copies(i, slot):
        cp.wait()

    @pl.when(g == 0)
    def _():
        km_scr[...] = jnp.zeros(km_scr.shape, F32)

    lane = lax.broadcasted_iota(jnp.int32, km_scr.shape, 2)
    acc = km_scr[...]
    for b in range(GATE_PAGES // blk_pages):
        x = kbuf[slot, blk_pages * b]
        for j in range(1, blk_pages):
            x = x + kbuf[slot, blk_pages * b + j]
        acc = jnp.where(lane == g * (GATE_PAGES // blk_pages) + b, jnp.sum(x, axis=-1, keepdims=True), acc)
    km_scr[...] = acc

    @pl.when(g == steps_per_seq - 1)
    def _():
        km = acc * (1.0 / MOBA_BLOCK)
        gate = jnp.concatenate([_mm(q_ref[0, h], km[h], HI) for h in range(N_HEADS)], axis=0)
        bidx = lax.broadcasted_iota(jnp.int32, gate.shape, 1)
        gt = jnp.where(bidx < nb, gate, -jnp.inf)
        sel = jnp.zeros(gate.shape, F32)
        for _ in range(MOBA_TOPK):
            top = jnp.max(gt, axis=1, keepdims=True)
            idx = jnp.min(jnp.where(gt == top, bidx, nb), axis=1, keepdims=True)
            pick = (bidx == idx) & (top > -jnp.inf)
            sel = jnp.where(pick, 1.0, sel)
            gt = jnp.where(pick, -jnp.inf, gt)
        sel_ref[0] = sel


def _moba_attend_kernel(pg_ref, blk_ref, slope_ref, q_ref, kn_ref, vn_ref, ck_ref, cv_ref, o_ref,
                        kbuf, vbuf, sem, *, n_visits, past):
    i = pl.program_id(0)
    nsteps = pl.num_programs(0)
    page = ck_ref.shape[3]
    n_copies = n_visits * (MOBA_BLOCK // page)

    def copies(step, slot):
        h = step % N_HEADS
        out = []
        for c in range(n_copies):
            pg = pg_ref[step * n_copies + c]
            cols = pl.ds(c * page, page)
            out.append(pltpu.make_async_copy(ck_ref.at[pg, h], kbuf.at[slot, :, cols], sem.at[0, slot]))
            out.append(pltpu.make_async_copy(cv_ref.at[pg, h], vbuf.at[slot, :, cols], sem.at[1, slot]))
        return out

    @pl.when(i == 0)
    def _():
        for cp in copies(0, 0):
            cp.start()

    @pl.when(i + 1 < nsteps)
    def _():
        for cp in copies(i + 1, (i + 1) % 2):
            cp.start()

    slot = i % 2
    for cp in copies(i, slot):
        cp.wait()

    nq = q_ref.shape[2]
    slope = slope_ref[0]
    qs = (q_ref[0, 0] * SCALE).astype(BF16)
    ncol = n_visits * MOBA_BLOCK
    row = lax.broadcasted_iota(jnp.int32, (nq, ncol), 0)
    col = lax.broadcasted_iota(jnp.int32, (nq, ncol), 1)
    vis = lax.broadcasted_iota(jnp.int32, (1, ncol), 1) // MOBA_BLOCK
    kstart = jnp.zeros((1, ncol), jnp.int32)
    for v in range(n_visits):
        kstart = jnp.where(vis == v, blk_ref[i * n_visits + v] * MOBA_BLOCK, kstart)
    kpos = kstart + lax.broadcasted_iota(jnp.int32, (1, ncol), 1) % MOBA_BLOCK
    dist = (past + vis // (n_visits // nq) - kpos).astype(F32)
    s = _mm(qs, kbuf[slot].astype(BF16)) - slope * dist
    s = jnp.where(col // (ncol // nq) == row, s, NEG)
    nk = kn_ref.shape[2]
    dist_o = (lax.broadcasted_iota(jnp.int32, (nq, nk), 0)
              - lax.broadcasted_iota(jnp.int32, (nq, nk), 1)).astype(F32)
    s_o = jnp.where(dist_o >= 0.0, _mm_nt(qs, kn_ref[0, 0].astype(BF16)) - slope * dist_o, NEG)
    m = jnp.maximum(jnp.max(s, axis=1, keepdims=True), jnp.max(s_o, axis=1, keepdims=True))
    p = jnp.exp(s - m)
    p_o = jnp.exp(s_o - m)
    l = jnp.sum(p, axis=1, keepdims=True) + jnp.sum(p_o, axis=1, keepdims=True)
    acc = _mm_nt(p.astype(BF16), vbuf[slot].astype(BF16)) + _mm(p_o.astype(BF16), vn_ref[0, 0].astype(BF16))
    o_ref[0, 0] = acc / l


def _moba_sample(q, k, v, cache_k, cache_v, page_table, slopes):
    b, s, _ = q.shape
    n_pool, page, _, _ = cache_k.shape
    n_pages = page_table.shape[1]
    past = n_pages * page
    pages_per_blk = MOBA_BLOCK // page
    assert pages_per_blk == 2 and past % MOBA_BLOCK == 0
    nb = past // MOBA_BLOCK
    assert nb >= MOBA_TOPK
    pt = page_table.reshape(-1).astype(jnp.int32)

    ck = cache_k.transpose(0, 2, 3, 1)
    cv = cache_v.transpose(0, 2, 3, 1)

    nk = 16
    lanes = 128
    assert n_pages % GATE_PAGES == 0 and nb <= lanes
    steps_per_seq = n_pages // GATE_PAGES

    def heads(z, rows):
        z = jnp.pad(z, ((0, 0), (0, rows - s), (0, 0)))
        return z.reshape(b, rows, N_HEADS, HEAD_DIM).transpose(0, 2, 1, 3)

    sel = pl.pallas_call(
        functools.partial(_moba_gate_kernel, n_pages=n_pages),
        grid_spec=pltpu.PrefetchScalarGridSpec(
            num_scalar_prefetch=1,
            grid=(b * steps_per_seq,),
            in_specs=[pl.BlockSpec((1, N_HEADS, s, HEAD_DIM), lambda i, pt: (i // steps_per_seq, 0, 0, 0)),
                      pl.BlockSpec(memory_space=pl.ANY)],
            out_specs=pl.BlockSpec((1, N_HEADS * s, lanes), lambda i, pt: (i // steps_per_seq, 0, 0)),
            scratch_shapes=[pltpu.VMEM((2, GATE_PAGES, N_HEADS, HEAD_DIM, page), F32),
                            pltpu.VMEM((N_HEADS, HEAD_DIM, lanes), F32),
                            pltpu.SemaphoreType.DMA((2,))]),
        out_shape=jax.ShapeDtypeStruct((b, N_HEADS * s, lanes), F32),
        compiler_params=_cparams(("arbitrary",)),
        name="moba_gate",
    )(pt, heads(q, s), ck)

    n_visits = s * MOBA_TOPK
    blk = jnp.argsort(-sel[:, :, :nb], axis=2, stable=True)[:, :, :MOBA_TOPK]
    blk = blk.reshape(b, N_HEADS, n_visits)
    pages = page_table[jnp.arange(b)[:, None, None, None],
                       blk[..., None] * pages_per_blk + jnp.arange(pages_per_blk)]

    def bh_spec(r, w):
        return pl.BlockSpec((1, 1, r, w), lambda i, pg, bk: (i // N_HEADS, i % N_HEADS, 0, 0))

    out = pl.pallas_call(
        functools.partial(_moba_attend_kernel, n_visits=n_visits, past=past),
        grid_spec=pltpu.PrefetchScalarGridSpec(
            num_scalar_prefetch=2,
            grid=(b * N_HEADS,),
            in_specs=[pl.BlockSpec((1, 1, 1), lambda i, pg, bk: (i % N_HEADS, 0, 0)),
                      bh_spec(s, HEAD_DIM), bh_spec(nk, HEAD_DIM), bh_spec(nk, HEAD_DIM),
                      pl.BlockSpec(memory_space=pl.ANY), pl.BlockSpec(memory_space=pl.ANY)],
            out_specs=bh_spec(s, HEAD_DIM),
            scratch_shapes=[pltpu.VMEM((2, HEAD_DIM, n_visits * MOBA_BLOCK), F32),
                            pltpu.VMEM((2, HEAD_DIM, n_visits * MOBA_BLOCK), F32),
                            pltpu.SemaphoreType.DMA((2, 2))]),
        out_shape=jax.ShapeDtypeStruct((b, N_HEADS, s, HEAD_DIM), F32),
        compiler_params=_cparams(("arbitrary",)),
        name="moba_attend",
    )(pages.reshape(-1).astype(jnp.int32), blk.reshape(-1).astype(jnp.int32), slopes.reshape(N_HEADS, 1, 1),
      heads(q, s), heads(k, nk), heads(v, nk), ck, cv)
    return out.transpose(0, 2, 1, 3).reshape(b, s, D_HEADS)


def _softplus(z):
    return jnp.maximum(z, 0.0) + jnp.log1p(jnp.exp(-jnp.abs(z)))


def _rwkv_prep_kernel(pb_ref, first_ref, mu_ref, w0_ref, w2_ref, a0_ref, a2_ref, g2_ref, kk_ref, ka_ref,
                      rk_ref, gm_ref, r_o, lw_o, k_o, v_o, kk_o, b_o, g_o, bonus_o, *, seq_len):
    pb = pb_ref[...]
    tm = pb.shape[0]
    row = lax.broadcasted_iota(jnp.int32, (tm, 1), 0)
    if seq_len is None:
        starts, first = row == 0, first_ref[0]
    else:
        starts, first = (row % seq_len) == 0, first_ref[...]
    prev = jnp.where(starts, first, pltpu.roll(pb, 1, axis=0))
    xm = pb + mu_ref[...] * (prev - pb)
    d = D_HEADS
    o1, o2, o3 = 3 * d, 3 * d + D_DECAY_LORA, 3 * d + D_DECAY_LORA + D_AAA_LORA
    r, k, v = xm[:, 0:d], xm[:, d:2 * d], xm[:, 2 * d:o1]
    xw, xa, xg = xm[:, o1:o2], xm[:, o2:o3], xm[:, o3:]
    logw = -_softplus(-(w0_ref[...] + _mm(jnp.tanh(xw), w2_ref[...]))) - 0.5
    a = jax.nn.sigmoid(a0_ref[...] + _mm(xa, a2_ref[...]))
    gmat = gm_ref[...]
    kk = k * kk_ref[...]
    kk = kk / jnp.maximum(jnp.sqrt(_head_sums(kk * kk, gmat)), 1e-12)
    k = k * (1.0 + (a - 1.0) * ka_ref[...])
    r_o[...] = r
    lw_o[...] = -jnp.exp(logw)
    k_o[...] = k
    v_o[...] = v
    kk_o[...] = kk
    b_o[...] = kk * a
    g_o[...] = _mm(jax.nn.sigmoid(xg), g2_ref[...])
    bonus_o[...] = _head_sums(r * k * rk_ref[...], gmat) * v


def _rwkv_prep(pb, first, prm, gmat, tm, seq_len):
    m, n = pb.shape
    d = D_HEADS
    if seq_len is None:
        first_spec = pl.BlockSpec((1, 1, n), lambda i: (i, 0, 0))
    else:
        first_spec = pl.BlockSpec((tm, n), lambda i: (i, 0))

    def full(shape):
        return pl.BlockSpec(shape, lambda i: (0,) * len(shape))

    def vec(z):
        return z.reshape(1, -1)

    outs = pl.pallas_call(
        functools.partial(_rwkv_prep_kernel, seq_len=seq_len),
        grid=(m // tm,),
        in_specs=[pl.BlockSpec((tm, n), lambda i: (i, 0)), first_spec,
                  full((1, n)), full((1, d)), full((D_DECAY_LORA, d)), full((1, d)), full((D_AAA_LORA, d)),
                  full((n - 3 * d - D_DECAY_LORA - D_AAA_LORA, d)), full((1, d)), full((1, d)), full((1, d)),
                  full((d, d))],
        out_specs=[pl.BlockSpec((tm, d), lambda i: (i, 0))] * 8,
        out_shape=[jax.ShapeDtypeStruct((m, d), F32)] * 8,
        compiler_params=_cparams(("parallel",)),
        name="rwkv_prep",
    )(pb, first, vec(prm['rw_mu']), vec(prm['rw_w0']), prm['rw_w2'], vec(prm['rw_a0']), prm['rw_a2'],
      prm['rw_g2'], vec(prm['rw_k_k']), vec(prm['rw_k_a']), vec(prm['rw_r_k']), gmat)
    return outs


def _rwkv_scan_kernel(r_ref, lw_ref, k_ref, v_ref, kk_ref, b_ref, kkT_ref, lwT_ref, vT_ref, s0_ref,
                      y_ref, sfin_ref, s_scr, *, chunk, chunks_per_step):
    j = pl.program_id(1)

    @pl.when(j == 0)
    def _():
        s_scr[...] = s0_ref[0]

    step = chunk * chunks_per_step
    d = D_HEADS
    si = lax.broadcasted_iota(jnp.int32, (step, step), 0)
    sj = lax.broadcasted_iota(jnp.int32, (step, step), 1)
    same = (si // chunk) == (sj // chunk)
    lbd = jnp.where(same & (si >= sj), 1.0, 0.0).astype(BF16)
    ubd = jnp.where(same & (si <= sj), 1.0, 0.0).astype(BF16)
    lw = lw_ref[...]
    lwT = lwT_ref[0]
    cl = sum(_mm(lbd, p) for p in _split3(lw))
    clT = sum(_mm(p, ubd) for p in _split3(lwT))
    tots = [cl[(c + 1) * chunk - 1:(c + 1) * chunk, :] for c in range(chunks_per_step)]
    tot_full = jnp.concatenate([jnp.broadcast_to(t, (chunk, d)) for t in tots], axis=0)
    e_neg = jnp.exp(-cl)
    e_end = jnp.exp(tot_full - cl)
    b = b_ref[...]
    k = k_ref[...]
    rho = r_ref[...] * jnp.exp(cl)
    beta, kap = b * e_neg, k * e_neg
    b_end, k_end = (b * e_end).astype(BF16), (k * e_end).astype(BF16)
    alphaT = kkT_ref[0] * jnp.exp(clT - lwT)
    vT = vT_ref[0].astype(BF16)
    v = v_ref[...].astype(BF16)

    ii = lax.broadcasted_iota(jnp.int32, (chunk, chunk), 0)
    jj = lax.broadcasted_iota(jnp.int32, (chunk, chunk), 1)
    low, upper = ii >= jj, ii < jj
    hd = HEAD_DIM
    units = [(slice(c * chunk, (c + 1) * chunk), slice(h * hd, (h + 1) * hd))
             for c in range(chunks_per_step) for h in range(N_HEADS)]
    bk = [jnp.concatenate([beta[rs, cs], kap[rs, cs]], axis=0).astype(BF16) for rs, cs in units]
    aT = [alphaT[cs, rs] for rs, cs in units]
    g1 = [_mm(x, a.astype(BF16)) for x, a in zip(bk, aT)]
    lt = [jnp.where(upper, g[:chunk], 0.0).astype(BF16) for g in g1]
    akT = [jnp.where(upper, g[chunk:], 0.0).astype(BF16) for g in g1]
    g2 = [_mm_nt(rho[rs, cs].astype(BF16), x) for (rs, cs), x in zip(units, bk)]
    arb = [jnp.where(low, g[:, :chunk], 0.0).astype(BF16) for g in g2]
    ark = [jnp.where(low, g[:, chunk:], 0.0).astype(BF16) for g in g2]
    z = [jnp.concatenate([a, _mm(vT[cs, rs], x)], axis=0) for (rs, cs), a, x in zip(units, aT, akT)]
    z = [x - _mm(x.astype(BF16), l) for x, l in zip(z, lt)]
    pw = lt
    n = 2
    while n < chunk:
        pw = [_mm(p, p).astype(BF16) for p in pw]
        z = [x + _mm(x.astype(BF16), p) for x, p in zip(z, pw)]
        n *= 2
    pz = [(-x).astype(BF16) for x in z]
    g3 = [_mm_nt(a, p) for a, p in zip(arb, pz)]
    g4 = [_mm(p, b_end[rs, cs]) for (rs, cs), p in zip(units, pz)]
    yin = [g[:, hd:] + _mm(a, v[rs, cs]) for (rs, cs), g, a in zip(units, g3, ark)]
    vk = [_mm(vT[cs, rs], k_end[rs, cs]) for rs, cs in units]
    state = [s_scr[h] for h in range(N_HEADS)]
    for c in range(chunks_per_step):
        etot = jnp.exp(tots[c])
        for h in range(N_HEADS):
            u = c * N_HEADS + h
            rs, cs = units[u]
            s = state[h]
            sb = s.astype(BF16)
            y_ref[rs, cs] = _mm_nt((rho[rs, cs] + g3[u][:, :hd]).astype(BF16), sb) + yin[u]
            state[h] = s * etot[:, cs] + _mm(sb, g4[u][:hd].astype(BF16)) + g4[u][hd:] + vk[u]
    for h in range(N_HEADS):
        s_scr[h] = state[h]

    @pl.when(j == pl.num_programs(1) - 1)
    def _():
        sfin_ref[0] = s_scr[...]


def _rwkv_scan(r, lw, k, v, kk, b, s0, seq_len, chunks_per_step):
    m, d = r.shape
    nseq = m // seq_len
    step = WKV_CHUNK * chunks_per_step
    nsteps = seq_len // step

    def tr(z):
        return z.reshape(nseq, seq_len, d).transpose(0, 2, 1)

    tm_spec = pl.BlockSpec((step, d), lambda i, j: (i * nsteps + j, 0))
    fm_spec = pl.BlockSpec((1, d, step), lambda i, j: (i, 0, j))
    st_spec = pl.BlockSpec((1, N_HEADS, HEAD_DIM, HEAD_DIM), lambda i, j: (i, 0, 0, 0))
    return pl.pallas_call(
        functools.partial(_rwkv_scan_kernel, chunk=WKV_CHUNK, chunks_per_step=chunks_per_step),
        grid=(nseq, nsteps),
        in_specs=[tm_spec] * 6 + [fm_spec] * 3 + [st_spec],
        out_specs=[tm_spec, st_spec],
        out_shape=[jax.ShapeDtypeStruct((m, d), F32),
                   jax.ShapeDtypeStruct((nseq, N_HEADS, HEAD_DIM, HEAD_DIM), F32)],
        scratch_shapes=[pltpu.VMEM((N_HEADS, HEAD_DIM, HEAD_DIM), F32)],
        compiler_params=_cparams(("parallel", "arbitrary")),
        name="rwkv_scan",
    )(r, lw, k, v, kk, b, tr(kk), tr(lw), tr(v), s0)


def _l0_out_kernel(y_ref, bonus_ref, g_ref, oa_ref, h_ref, gnw_ref, gnb_ref, gm_ref, wa_ref, wb_ref,
                   gpost_ref, o_ref):
    y = y_ref[...]
    gmat = gm_ref[...]
    inv = 1.0 / HEAD_DIM
    yc = y - _head_sums(y, gmat) * inv
    yn = yc * lax.rsqrt(_head_sums(yc * yc, gmat) * inv + GN_EPS) * gnw_ref[...] + gnb_ref[...]
    ob = (yn + bonus_ref[...]) * g_ref[...]
    mix = _mm(oa_ref[...].astype(BF16), wa_ref[...]) + _mm(ob.astype(BF16), wb_ref[...])
    o_ref[...] = h_ref[...] + _rms(mix, gpost_ref[...])


def _l0_out(y, bonus, g, oa, h, prm, gmat, w_out, tm):
    m, dm = h.shape
    d = D_HEADS

    def rowblk(w):
        return pl.BlockSpec((tm, w), lambda i: (i, 0))

    def full(shape):
        return pl.BlockSpec(shape, lambda i: (0,) * len(shape))

    return pl.pallas_call(
        _l0_out_kernel,
        grid=(m // tm,),
        in_specs=[rowblk(d), rowblk(d), rowblk(d), rowblk(d), rowblk(dm), full((1, d)), full((1, d)),
                  full((d, d)), full((d, dm)), full((d, dm)), full((1, dm))],
        out_specs=rowblk(dm),
        out_shape=jax.ShapeDtypeStruct((m, dm), F32),
        compiler_params=_cparams(("parallel",)),
        name="l0_out",
    )(y, bonus, g, oa, h, prm['rw_gn_w'].reshape(1, d), prm['rw_gn_b'].reshape(1, d), gmat,
      w_out[:d], w_out[d:], prm['l0_norm_post'].reshape(1, dm))


FFN_COLS = 256
FFN_HALO = 16


def _ffn_kernel(h_ref, halo_ref, gpre_ref, wg_ref, wv_ref, cwg_ref, cwv_ref, cbg_ref, cbv_ref, pg_ref, pv_ref,
                wd_ref, gpost_ref, o_ref, xn_scr, acc_scr, *, seq_len):
    i = pl.program_id(0)
    c = pl.program_id(1)
    tm = h_ref.shape[0]

    @pl.when(c == 0)
    def _():
        xn_scr[0:tm] = _rms(h_ref[...], gpre_ref[...]).astype(BF16)
        if seq_len is None:
            xn_scr[tm:tm + FFN_HALO] = _rms(halo_ref[...], gpre_ref[...]).astype(BF16)
        acc_scr[...] = jnp.zeros(acc_scr.shape, F32)

    row = lax.broadcasted_iota(jnp.int32, (tm, 1), 0)
    pos = row if seq_len is None else row % seq_len

    def conv(w_ref, cw_ref, cb_ref, p_ref):
        up = _mm(xn_scr[...], w_ref[...])
        u = up[:tm]
        m1 = pltpu.roll(u, 1, axis=0)
        m2 = pltpu.roll(u, 2, axis=0)
        if seq_len is None:
            last = tm + FFN_HALO
            p0 = jnp.where(i == 0, p_ref[6:7], up[last - 2:last - 1])
            p1 = jnp.where(i == 0, p_ref[7:8], up[last - 1:last])
            m1 = jnp.concatenate([jnp.where(pos[:8] == 0, p1, m1[:8]), m1[8:]], axis=0)
            m2 = jnp.concatenate([jnp.where(pos[:8] == 0, p0, jnp.where(pos[:8] == 1, p1, m2[:8])), m2[8:]], axis=0)
        else:
            m1 = jnp.where(pos == 0, p_ref[1], m1)
            m2 = jnp.where(pos == 0, p_ref[0], jnp.where(pos == 1, p_ref[1], m2))
        return cb_ref[...] + cw_ref[0:1] * m2 + cw_ref[1:2] * m1 + cw_ref[2:3] * u

    gate = conv(wg_ref, cwg_ref, cbg_ref, pg_ref)
    val = conv(wv_ref, cwv_ref, cbv_ref, pv_ref)
    acc_scr[...] += _mm((jax.nn.gelu(gate) * val).astype(BF16), wd_ref[...])

    @pl.when(c == pl.num_programs(1) - 1)
    def _():
        o_ref[...] = h_ref[...] + _rms(acc_scr[...], gpost_ref[...])


def _ffn(h, prev_rows, g_pre, w_up, conv_w, conv_b, w_down, g_post, tm, seq_len):
    m, dm = h.shape
    dff = w_down.shape[0]
    n2 = 2 * dff
    assert dff % FFN_COLS == 0
    nc = dff // FFN_COLS
    if seq_len is None:
        prev = jnp.pad(prev_rows[0], ((6, 0), (0, 0)))

        def p_spec(off):
            return pl.BlockSpec((8, FFN_COLS), lambda i, c: (0, off + c))

        halo_spec = pl.BlockSpec((FFN_HALO, dm), lambda i, c: (jnp.maximum(i * (tm // FFN_HALO) - 1, 0), 0))
        xn_rows = tm + FFN_HALO
    else:
        prev = jnp.stack([jnp.repeat(prev_rows[:, 0], seq_len, axis=0), jnp.repeat(prev_rows[:, 1], seq_len, axis=0)])

        def p_spec(off):
            return pl.BlockSpec((2, tm, FFN_COLS), lambda i, c: (0, i, off + c))

        halo_spec = pl.BlockSpec((FFN_HALO, dm), lambda i, c: (0, 0))
        xn_rows = tm

    def col_spec(rows, off):
        return pl.BlockSpec((rows, FFN_COLS), lambda i, c: (0, off + c))

    def full(shape):
        return pl.BlockSpec(shape, lambda i, c: (0,) * len(shape))

    cb = conv_b.reshape(1, n2)
    return pl.pallas_call(
        functools.partial(_ffn_kernel, seq_len=seq_len),
        grid=(m // tm, nc),
        in_specs=[pl.BlockSpec((tm, dm), lambda i, c: (i, 0)), halo_spec, full((1, dm)),
                  col_spec(dm, 0), col_spec(dm, nc), col_spec(CONV_W, 0), col_spec(CONV_W, nc),
                  col_spec(1, 0), col_spec(1, nc), p_spec(0), p_spec(nc),
                  pl.BlockSpec((FFN_COLS, dm), lambda i, c: (c, 0)), full((1, dm))],
        out_specs=pl.BlockSpec((tm, dm), lambda i, c: (i, 0)),
        out_shape=jax.ShapeDtypeStruct((m, dm), F32),
        scratch_shapes=[pltpu.VMEM((xn_rows, dm), BF16), pltpu.VMEM((tm, dm), F32)],
        compiler_params=_cparams(("parallel", "arbitrary")),
        name="ffn",
    )(h, h, g_pre.reshape(1, dm), w_up, w_up, conv_w, conv_w, cb, cb, prev, prev, w_down, g_post.reshape(1, dm))


def _gmlp_kernel(h_ref, gpre_ref, win_ref, lnw_ref, lnb_ref, wmix_ref, bias_ref, wout_ref, gpost_ref,
                 o_ref, cv_ref, gated_scr):
    h = h_ref[...]
    tm = h.shape[0]
    dc = wout_ref.shape[0]
    rm = wmix_ref.shape[1]
    gw = dc // GMLP_GROUPS
    z = jax.nn.gelu(_mm(_rms(h, gpre_ref[...]).astype(BF16), win_ref[...]))
    v = z[:, dc:]
    vc = v - jnp.mean(v, axis=-1, keepdims=True)
    vn = vc * lax.rsqrt(jnp.mean(vc * vc, axis=-1, keepdims=True) + LN_EPS) * lnw_ref[...] + lnb_ref[...]
    cv_ref[...] = vn
    for s in range(tm // rm):
        rows = slice(s * rm, (s + 1) * rm)
        for g in range(GMLP_GROUPS):
            cols = slice(g * gw, (g + 1) * gw)
            mixed = _mm(wmix_ref[g], vn[rows, cols].astype(BF16)) + bias_ref[:, cols]
            gated_scr[rows, cols] = (z[rows, cols] * mixed).astype(BF16)
    o_ref[...] = h + _rms(_mm(gated_scr[...], wout_ref[...]), gpost_ref[...])


def _gmlp(h, g_pre, w_in, ln_w, ln_b, wmix, bias, w_out, g_post, tm):
    m, dm = h.shape
    dc = w_out.shape[0]
    rm = wmix.shape[1]

    def full(shape):
        return pl.BlockSpec(shape, lambda i: (0,) * len(shape))

    return pl.pallas_call(
        _gmlp_kernel,
        grid=(m // tm,),
        in_specs=[pl.BlockSpec((tm, dm), lambda i: (i, 0)), full((1, dm)), full((dm, 2 * dc)), full((1, dc)),
                  full((1, dc)), full((GMLP_GROUPS, rm, rm)), full((rm, dc)), full((dc, dm)), full((1, dm))],
        out_specs=[pl.BlockSpec((tm, dm), lambda i: (i, 0)), pl.BlockSpec((tm, dc), lambda i: (i, 0))],
        out_shape=[jax.ShapeDtypeStruct((m, dm), F32), jax.ShapeDtypeStruct((m, dc), F32)],
        scratch_shapes=[pltpu.VMEM((tm, dc), BF16)],
        compiler_params=_cparams(("parallel",)),
        name="gmlp",
    )(h, g_pre.reshape(1, dm), w_in, ln_w.reshape(1, dc), ln_b.reshape(1, dc), wmix, bias, w_out,
      g_post.reshape(1, dm))


def _trunk(x, attend, prev_xn, wkv0, conv0, prm, w, *, tile, long_seq):
    nseq, t, dm = x.shape
    m = nseq * t
    h = x.reshape(m, dm)
    seq_len = None if long_seq else t

    g0 = prm['l0_norm_pre']
    q, k_new, v_new = _norm_matmul_split(h, g0, w['w_qkv'], tile, 3)
    pb = _norm_matmul(h, g0, w['w_b'], tile, w['w_b'].shape[1] // 2)
    shift_new, prev_proj = _shift_rows(x[:, -1], g0, prev_xn, w['w_b'])
    if long_seq:
        first = jnp.concatenate([prev_proj, pb[tile - 1:m - 1:tile]], axis=0).reshape(m // tile, 1, -1)
    else:
        first = jnp.repeat(prev_proj, t, axis=0)
    r, lw, kmod, vv, kk, bb, gate, bonus = _rwkv_prep(pb, first, prm, w['gmat'], tile, seq_len)
    o_a = attend(q, k_new, v_new)

    t_pad = -(-t // WKV_CHUNK) * WKV_CHUNK

    def padt(z):
        if t_pad == t:
            return z
        return jnp.pad(z.reshape(nseq, t, -1), ((0, 0), (0, t_pad - t), (0, 0))).reshape(nseq * t_pad, -1)

    y, wkv_new = _rwkv_scan(padt(r), padt(lw), padt(kmod), padt(vv), padt(kk), padt(bb), wkv0, t_pad,
                            4 if t_pad % (4 * WKV_CHUNK) == 0 else 1)
    if t_pad != t:
        y = y.reshape(nseq, t_pad, -1)[:, :t].reshape(m, -1)
    h = _l0_out(y, bonus, gate, o_a, h, prm, w['gmat'], w['w_out0'], tile)

    conv_rows = []
    ffn_tile = min(m, 1024)

    def ffn(h, layer):
        g_pre = prm['ffn_norm_pre'][layer]
        tail = h.reshape(nseq, t, dm)[:, t - min(t, 8):].reshape(-1, dm)
        up_tail = _norm_matmul(tail, g_pre, w['w_up'][layer], tail.shape[0], 512)
        conv_rows.append(up_tail.reshape(nseq, min(t, 8), -1)[:, -(CONV_W - 1):])
        return _ffn(h, conv0[layer], g_pre, w['w_up'][layer], prm['ffn_conv_w'][layer], prm['ffn_conv_b'][layer],
                    w['w_down'][layer], prm['ffn_norm_post'][layer], ffn_tile, seq_len)

    h = ffn(h, 0)

    if long_seq:
        wmix, bias = w['wmix'], w['gbias']
    else:
        eye = jnp.eye(tile // t, dtype=F32)
        wmix = jnp.einsum('ab,gij->gaibj', eye, w['wmix'][:, :t, :t].astype(F32))
        wmix = wmix.reshape(GMLP_GROUPS, tile, tile).astype(BF16)
        bias = jnp.tile(w['gbias'][:t], (tile // t, 1))
    h, chunk_v = _gmlp(h, prm['l1_norm_pre'], w['w_in1'], prm['gm_ln_w'], prm['gm_ln_b'], wmix, bias,
                       w['w_out1'], prm['l1_norm_post'], tile)
    h = ffn(h, 1)
    return (h.reshape(nseq, t, dm), k_new.reshape(nseq, t, N_HEADS, HEAD_DIM),
            v_new.reshape(nseq, t, N_HEADS, HEAD_DIM), wkv_new, shift_new, chunk_v.reshape(nseq, t, -1),
            jnp.stack(conv_rows))


def kernel(x_prompt, x_sample, cache_k, cache_v, state_wkv, state_shift, state_ffn_conv, page_table,
           l0_norm_pre, l0_norm_post, l0_w_in, rw_mu, rw_w0, rw_w2, rw_a0, rw_a2, rw_g2, rw_k_k, rw_k_a,
           rw_r_k, rw_gn_w, rw_gn_b, l0_w_out, l1_norm_pre, l1_norm_post, l1_w_in, gm_ln_w, gm_ln_b,
           gm_ws, gm_bs, l1_w_out, ffn_norm_pre, ffn_norm_post, ffn_w_up, ffn_conv_w, ffn_conv_b,
           ffn_w_down):
    prm = dict(l0_norm_pre=l0_norm_pre, l0_norm_post=l0_norm_post, rw_mu=rw_mu, rw_w0=rw_w0, rw_w2=rw_w2,
               rw_a0=rw_a0, rw_a2=rw_a2, rw_g2=rw_g2, rw_k_k=rw_k_k, rw_k_a=rw_k_a, rw_r_k=rw_r_k,
               rw_gn_w=rw_gn_w, rw_gn_b=rw_gn_b, l1_norm_pre=l1_norm_pre, l1_norm_post=l1_norm_post,
               gm_ln_w=gm_ln_w, gm_ln_b=gm_ln_b, ffn_norm_pre=ffn_norm_pre, ffn_norm_post=ffn_norm_post,
               ffn_conv_w=ffn_conv_w, ffn_conv_b=ffn_conv_b)
    d_a = D_HEADS
    dc = l1_w_out.shape[0]
    tri = jnp.tril(jnp.ones((GMLP_CHUNK, GMLP_CHUNK), dtype=bool))
    hd = jnp.arange(D_HEADS) // HEAD_DIM
    w = dict(
        w_qkv=l0_w_in[:, :3 * d_a].astype(BF16),
        w_b=l0_w_in[:, 3 * d_a:].astype(BF16),
        w_out0=l0_w_out.astype(BF16),
        w_up=ffn_w_up.astype(BF16),
        w_down=ffn_w_down.astype(BF16),
        w_in1=l1_w_in.astype(BF16),
        w_out1=l1_w_out.astype(BF16),
        wmix=jnp.where(tri[None], gm_ws, 0.0).astype(BF16),
        gbias=jnp.repeat(gm_bs.T, dc // GMLP_GROUPS, axis=1),
        gmat=(hd[:, None] == hd[None, :]).astype(BF16),
    )
    slopes = jnp.exp2(-8.0 * jnp.arange(1, N_HEADS + 1, dtype=F32) / N_HEADS)

    bp, tp, dm = x_prompt.shape
    assert bp == 1
    dt = x_prompt.dtype
    depth = ffn_w_up.shape[0]
    wkv0 = jnp.zeros((bp, N_HEADS, HEAD_DIM, HEAD_DIM), dt)
    shift0 = jnp.zeros((bp, dm), dt)
    conv0 = jnp.zeros((depth, bp, CONV_W - 1, ffn_w_up.shape[2]), dt)

    def attend_prompt(q, k, v):
        return _moba_prompt(q, k, v, slopes)

    (y_prompt, k_prompt, v_prompt, wkv_prompt, shift_prompt, _unused_chunk_v,
     ffn_conv_prompt) = _trunk(x_prompt, attend_prompt, shift0, wkv0, conv0, prm, w, tile=512, long_seq=True)

    bs, ts, _ = x_sample.shape

    def attend_sample(q, k, v):
        o = _moba_sample(q.reshape(bs, ts, -1), k.reshape(bs, ts, -1), v.reshape(bs, ts, -1), cache_k, cache_v,
                         page_table, slopes)
        return o.reshape(bs * ts, -1)

    (y_sample, k_sample, v_sample, wkv_sample, shift_sample, chunk_v_sample,
     ffn_conv_sample) = _trunk(x_sample, attend_sample, state_shift, state_wkv, state_ffn_conv, prm, w,
                               tile=bs * ts, long_seq=False)

    return (y_prompt, y_sample, k_prompt, v_prompt, k_sample, v_sample, wkv_prompt, wkv_sample,
            shift_prompt, shift_sample, chunk_v_sample, ffn_conv_prompt, ffn_conv_sample)
```

```python
import functools

import jax
import jax.numpy as jnp
from jax import lax
from jax.experimental import pallas as pl
from jax.experimental.pallas import tpu as pltpu

F32 = jnp.float32
BF16 = jnp.bfloat16
HI = lax.Precision.HIGHEST

HEAD_DIM = 64
N_HEADS = 8
D_HEADS = N_HEADS * HEAD_DIM
MOBA_BLOCK = 256
MOBA_TOPK = 3
MOBA_GROUP = 4
SCALE = HEAD_DIM ** -0.5
D_DECAY_LORA = 64
D_AAA_LORA = 64
GMLP_GROUPS = 8
GMLP_CHUNK = 128
CONV_W = 3
NORM_EPS = 1e-6
LN_EPS = 1e-5
GN_EPS = 64e-5
LOG2E = 1.4426950408889634
MOBA_KAUG = 128
MOBA_VAUG = 80
NEG = -1e30
WKV_CHUNK = 64
V7X_VMEM_LIMIT = 56 * 1024 * 1024


def _cparams(sem, vmem=V7X_VMEM_LIMIT):
    return pltpu.CompilerParams(dimension_semantics=sem, vmem_limit_bytes=vmem)


def _mm(a, b, prec=None):
    return jnp.dot(a, b, precision=prec, preferred_element_type=F32)


def _mm_nt(a, b, prec=None):
    return lax.dot_general(a, b, (((1,), (1,)), ((), ())), precision=prec, preferred_element_type=F32)


def _split3(x):
    p1 = x.astype(BF16)
    r1 = x - p1.astype(F32)
    p2 = r1.astype(BF16)
    p3 = (r1 - p2.astype(F32)).astype(BF16)
    return p1, p2, p3


def _head_sums(x, gmat):
    return sum(_mm(p, gmat) for p in _split3(x))


def _rms(x, g):
    return x * lax.rsqrt(jnp.mean(x * x, axis=-1, keepdims=True) + NORM_EPS) * g


def _norm_matmul_kernel(x_ref, g_ref, w_ref, o_ref, xn_ref):
    @pl.when(pl.program_id(1) == 0)
    def _():
        xn_ref[...] = _rms(x_ref[...], g_ref[...]).astype(BF16)

    o_ref[...] = _mm(xn_ref[...], w_ref[...])


def _norm_matmul(x, g, w, tm, tn):
    m, kd = x.shape
    n = w.shape[1]
    return pl.pallas_call(
        _norm_matmul_kernel,
        grid=(m // tm, n // tn),
        in_specs=[pl.BlockSpec((tm, kd), lambda i, j: (i, 0)),
                  pl.BlockSpec((1, kd), lambda i, j: (0, 0)),
                  pl.BlockSpec((kd, tn), lambda i, j: (0, j))],
        out_specs=pl.BlockSpec((tm, tn), lambda i, j: (i, j)),
        out_shape=jax.ShapeDtypeStruct((m, n), F32),
        scratch_shapes=[pltpu.VMEM((tm, kd), BF16)],
        compiler_params=_cparams(("parallel", "arbitrary")),
        name="norm_matmul",
    )(x, g.reshape(1, kd), w)


def _norm_matmul_split_kernel(x_ref, g_ref, w_ref, *refs):
    outs, xn_ref = refs[:-1], refs[-1]
    j = pl.program_id(1)

    @pl.when(j == 0)
    def _():
        xn_ref[...] = _rms(x_ref[...], g_ref[...]).astype(BF16)

    y = _mm(xn_ref[...], w_ref[...])
    for n, o_ref in enumerate(outs):
        @pl.when(j == n)
        def _(o_ref=o_ref):
            o_ref[...] = y


def _norm_matmul_split(x, g, w, tm, n_out):
    m, kd = x.shape
    tn = w.shape[1] // n_out
    return pl.pallas_call(
        _norm_matmul_split_kernel,
        grid=(m // tm, n_out),
        in_specs=[pl.BlockSpec((tm, kd), lambda i, j: (i, 0)),
                  pl.BlockSpec((1, kd), lambda i, j: (0, 0)),
                  pl.BlockSpec((kd, tn), lambda i, j: (0, j))],
        out_specs=[pl.BlockSpec((tm, tn), lambda i, j: (i, 0))] * n_out,
        out_shape=[jax.ShapeDtypeStruct((m, tn), F32)] * n_out,
        scratch_shapes=[pltpu.VMEM((tm, kd), BF16)],
        compiler_params=_cparams(("parallel", "arbitrary")),
        name="norm_matmul_split",
    )(x, g.reshape(1, kd), w)


def _shift_rows_kernel(x_ref, g_ref, prev_ref, w_ref, xn_ref, pp_ref):
    xn_ref[...] = _rms(x_ref[...], g_ref[...])
    pp_ref[...] = _mm(prev_ref[...].astype(BF16), w_ref[...])


def _shift_rows(x_last, g, prev_xn, w_b):
    b, kd = x_last.shape
    n = w_b.shape[1]
    bp = -(-b // 8) * 8
    pad = ((0, bp - b), (0, 0))
    xn, pp = pl.pallas_call(
        _shift_rows_kernel,
        out_shape=(jax.ShapeDtypeStruct((bp, kd), F32), jax.ShapeDtypeStruct((bp, n), F32)),
        compiler_params=_cparams(None),
        name="shift_rows",
    )(jnp.pad(x_last, pad), g.reshape(1, kd), jnp.pad(prev_xn, pad), w_b)
    return xn[:b], pp[:b]


def _kmeans_kernel(k_ref, o_ref):
    k = k_ref[0]
    nb = k.shape[0] // MOBA_BLOCK
    o_ref[0] = jnp.mean(k.reshape(nb, MOBA_BLOCK, HEAD_DIM), axis=1)


def _moba_prompt_kernel(slope_ref, qT_ref, km_ref, k_ref, vT_ref, oT_ref, sel_scr):
    i = pl.program_id(1)
    nb = km_ref.shape[1]
    blk = MOBA_BLOCK
    qf = qT_ref[0]
    gate = _mm(km_ref[0], qf, HI)
    nidx = lax.broadcasted_iota(jnp.int32, gate.shape, 0)
    g = jnp.where(nidx < i, gate, -jnp.inf)
    sel = jnp.zeros(gate.shape, F32)
    for _ in range(MOBA_TOPK):
        top = jnp.max(g, axis=0, keepdims=True)
        idx = jnp.min(jnp.where(g == top, nidx, nb), axis=0, keepdims=True)
        pick = (nidx == idx) & (top > -jnp.inf)
        sel = jnp.where(pick, 1.0, sel)
        g = jnp.where(pick, -jnp.inf, g)
    sel_scr[...] = sel

    slope2 = slope_ref[0] * LOG2E
    pieces = _split3(jnp.broadcast_to(slope2, (1, blk)))
    r16 = lax.broadcasted_iota(jnp.int32, (16, blk), 0)
    aug = jnp.zeros((16, blk), F32)
    for j, piece in enumerate(pieces):
        aug = jnp.where(r16 == j, piece.astype(F32), aug)
    qs = jnp.concatenate([(qf * (SCALE * LOG2E)).astype(BF16), aug.astype(BF16),
                          jnp.zeros((MOBA_KAUG - HEAD_DIM - 16, blk), BF16)], axis=0)
    ki = lax.broadcasted_iota(jnp.int32, (blk, blk), 0)
    qi = lax.broadcasted_iota(jnp.int32, (blk, blk), 1)

    own = pl.multiple_of(i * blk, blk)
    s = jnp.where(ki <= qi, _mm(k_ref[0, pl.ds(own, blk), :], qs), NEG)
    m0 = jnp.max(s, axis=0, keepdims=True)
    acc0 = _mm(vT_ref[0, :, pl.ds(own, blk)], jnp.exp2(s - m0).astype(BF16))

    grp = MOBA_GROUP
    gk = grp * blk

    def qk(g):
        start = pl.multiple_of(jnp.minimum(g, nb // grp - 1) * gk, gk)
        return _mm(k_ref[0, pl.ds(start, gk), :], qs)

    def block_max(s):
        return tuple(jnp.max(s[u * blk:(u + 1) * blk], axis=0, keepdims=True) for u in range(grp))

    def body(g, carry):
        m, acc, s, mbs = carry
        s_next = qk(g + 1)
        n0 = pl.multiple_of(g * grp, grp)
        picked = [sel_scr[pl.ds(n0 + u, 1), :] > 0.0 for u in range(grp)]
        shift = [slope2 * ((n0 + u - i) * blk).astype(F32) for u in range(grp)]
        m_new = m
        for u in range(grp):
            m_new = jnp.where(picked[u], jnp.maximum(m_new, mbs[u] + shift[u]), m_new)
        acc = jnp.exp2(m - m_new) * acc
        for u in range(grp):
            off = jnp.where(picked[u], m_new - shift[u], -NEG)
            p = jnp.exp2(s[u * blk:(u + 1) * blk] - off).astype(BF16)
            start = pl.multiple_of((n0 + u) * blk, blk)
            acc = acc + _mm(vT_ref[0, :, pl.ds(start, blk)], p)
        return m_new, acc, s_next, block_max(s_next)

    s0 = qk(0)
    _, acc, _, _ = lax.fori_loop(0, (i + grp - 1) // grp, body, (m0, acc0, s0, block_max(s0)))
    oT_ref[0] = acc[:HEAD_DIM] / acc[HEAD_DIM:HEAD_DIM + 1]


def _moba_prompt(q, k, v, slopes):
    t = q.shape[0]
    assert t % (MOBA_BLOCK * MOBA_GROUP) == 0
    nb = t // MOBA_BLOCK

    def heads_t(z):
        return z.reshape(t, N_HEADS, HEAD_DIM).transpose(1, 2, 0)

    k_hm = k.reshape(t, N_HEADS, HEAD_DIM).transpose(1, 0, 2)
    offs = jnp.broadcast_to((jnp.arange(t) % MOBA_BLOCK).astype(BF16)[None, :, None], (N_HEADS, t, 3))
    k_aug = jnp.concatenate([k_hm.astype(BF16), offs,
                             jnp.zeros((N_HEADS, t, MOBA_KAUG - HEAD_DIM - 3), BF16)], axis=-1)
    v_aug = jnp.concatenate([heads_t(v).astype(BF16), jnp.ones((N_HEADS, 1, t), BF16),
                             jnp.zeros((N_HEADS, MOBA_VAUG - HEAD_DIM - 1, t), BF16)], axis=1)
    kmean = pl.pallas_call(
        _kmeans_kernel,
        grid=(N_HEADS,),
        in_specs=[pl.BlockSpec((1, t, HEAD_DIM), lambda h: (h, 0, 0))],
        out_specs=pl.BlockSpec((1, nb, HEAD_DIM), lambda h: (h, 0, 0)),
        out_shape=jax.ShapeDtypeStruct((N_HEADS, nb, HEAD_DIM), F32),
        compiler_params=_cparams(("parallel",)),
        name="moba_kmeans",
    )(k_hm)
    oT = pl.pallas_call(
        _moba_prompt_kernel,
        grid=(N_HEADS, nb),
        in_specs=[pl.BlockSpec((1, 1, 1), lambda h, i: (h, 0, 0)),
                  pl.BlockSpec((1, HEAD_DIM, MOBA_BLOCK), lambda h, i: (h, 0, i)),
                  pl.BlockSpec((1, nb, HEAD_DIM), lambda h, i: (h, 0, 0)),
                  pl.BlockSpec((1, t, MOBA_KAUG), lambda h, i: (h, 0, 0)),
                  pl.BlockSpec((1, MOBA_VAUG, t), lambda h, i: (h, 0, 0))],
        out_specs=pl.BlockSpec((1, HEAD_DIM, MOBA_BLOCK), lambda h, i: (h, 0, i)),
        out_shape=jax.ShapeDtypeStruct((N_HEADS, HEAD_DIM, t), F32),
        scratch_shapes=[pltpu.VMEM((nb, MOBA_BLOCK), F32)],
        compiler_params=_cparams(("parallel", "arbitrary")),
        name="moba_prompt",
    )(slopes.reshape(N_HEADS, 1, 1), heads_t(q), kmean, k_aug, v_aug)
    return oT.transpose(2, 0, 1).reshape(t, D_HEADS)


GATE_PAGES = 16


def _moba_gate_kernel(pt_ref, q_ref, ck_ref, sel_ref, kbuf, km_scr, sem, *, n_pages):
    i = pl.program_id(0)
    nsteps = pl.num_programs(0)
    steps_per_seq = n_pages // GATE_PAGES
    g = i % steps_per_seq
    blk_pages = MOBA_BLOCK // ck_ref.shape[3]
    nb = n_pages // blk_pages

    def copies(step, slot):
        return [pltpu.make_async_copy(ck_ref.at[pt_ref[step * GATE_PAGES + c]], kbuf.at[slot, c], sem.at[slot])
                for c in range(GATE_PAGES)]

    @pl.when(i == 0)
    def _():
        for cp in copies(0, 0):
            cp.start()

    @pl.when(i + 1 < nsteps)
    def _():
        for cp in copies(i + 1, (i + 1) % 2):
            cp.start()

    slot = i % 2
    for cp in copies(i, slot):
        cp.wait()

    @pl.when(g == 0)
    def _():
        km_scr[...] = jnp.zeros(km_scr.shape, F32)

    lane = lax.broadcasted_iota(jnp.int32, km_scr.shape, 2)
    acc = km_scr[...]
    for b in range(GATE_PAGES // blk_pages):
        x = kbuf[slot, blk_pages * b]
        for j in range(1, blk_pages):
            x = x + kbuf[slot, blk_pages * b + j]
        acc = jnp.where(lane == g * (GATE_PAGES // blk_pages) + b, jnp.sum(x, axis=-1, keepdims=True), acc)
    km_scr[...] = acc

    @pl.when(g == steps_per_seq - 1)
    def _():
        km = acc * (1.0 / MOBA_BLOCK)
        gate = jnp.concatenate([_mm(q_ref[0, h], km[h], HI) for h in range(N_HEADS)], axis=0)
        bidx = lax.broadcasted_iota(jnp.int32, gate.shape, 1)
        gt = jnp.where(bidx < nb, gate, -jnp.inf)
        sel = jnp.zeros(gate.shape, jnp.int32)
        for r in range(MOBA_TOPK):
            top = jnp.max(gt, axis=1, keepdims=True)
            idx = jnp.min(jnp.where(gt == top, bidx, nb), axis=1, keepdims=True)
            sel = jnp.where(bidx == r, idx, sel)
            gt = jnp.where(bidx == idx, -jnp.inf, gt)
        sel_ref[0] = sel


def _moba_attend_kernel(pt_ref, blk_ref, slope_ref, q_ref, kn_ref, vn_ref, ck_ref, cv_ref, o_ref,
                        kbuf, vbuf, sem, *, n_visits, n_pages):
    i = pl.program_id(0)
    nsteps = pl.num_programs(0)
    page = ck_ref.shape[3]
    blk_pages = MOBA_BLOCK // page
    n_copies = n_visits * blk_pages
    past = n_pages * page

    def copies(step, slot):
        h = step % N_HEADS
        out = []
        for c in range(n_copies):
            blk_id = blk_ref[step * n_visits + c // blk_pages]
            pg = pt_ref[(step // N_HEADS) * n_pages + blk_id * blk_pages + c % blk_pages]
            cols = pl.ds(c * page, page)
            out.append(pltpu.make_async_copy(ck_ref.at[pg, h], kbuf.at[slot, :, cols], sem.at[0, slot]))
            out.append(pltpu.make_async_copy(cv_ref.at[pg, h], vbuf.at[slot, :, cols], sem.at[1, slot]))
        return out

    @pl.when(i == 0)
    def _():
        for cp in copies(0, 0):
            cp.start()

    @pl.when(i + 1 < nsteps)
    def _():
        for cp in copies(i + 1, (i + 1) % 2):
            cp.start()

    slot = i % 2
    for cp in copies(i, slot):
        cp.wait()

    nq = q_ref.shape[2]
    slope = slope_ref[0]
    qs = (q_ref[0, 0] * SCALE).astype(BF16)
    ncol = n_visits * MOBA_BLOCK
    row = lax.broadcasted_iota(jnp.int32, (nq, ncol), 0)
    col = lax.broadcasted_iota(jnp.int32, (nq, ncol), 1)
    vis = lax.broadcasted_iota(jnp.int32, (1, ncol), 1) // MOBA_BLOCK
    kstart = jnp.zeros((1, ncol), jnp.int32)
    for v in range(n_visits):
        kstart = jnp.where(vis == v, blk_ref[i * n_visits + v] * MOBA_BLOCK, kstart)
    kpos = kstart + lax.broadcasted_iota(jnp.int32, (1, ncol), 1) % MOBA_BLOCK
    dist = (past + vis // (n_visits // nq) - kpos).astype(F32)
    s = _mm(qs, kbuf[slot].astype(BF16)) - slope * dist
    s = jnp.where(col // (ncol // nq) == row, s, NEG)
    nk = kn_ref.shape[2]
    dist_o = (lax.broadcasted_iota(jnp.int32, (nq, nk), 0)
              - lax.broadcasted_iota(jnp.int32, (nq, nk), 1)).astype(F32)
    s_o = jnp.where(dist_o >= 0.0, _mm_nt(qs, kn_ref[0, 0].astype(BF16)) - slope * dist_o, NEG)
    m = jnp.maximum(jnp.max(s, axis=1, keepdims=True), jnp.max(s_o, axis=1, keepdims=True))
    p = jnp.exp(s - m)
    p_o = jnp.exp(s_o - m)
    l = jnp.sum(p, axis=1, keepdims=True) + jnp.sum(p_o, axis=1, keepdims=True)
    acc = _mm_nt(p.astype(BF16), vbuf[slot].astype(BF16)) + _mm(p_o.astype(BF16), vn_ref[0, 0].astype(BF16))
    o_ref[0, 0] = acc / l


def _moba_sample(q, k, v, cache_k, cache_v, page_table, slopes):
    b, s, _ = q.shape
    n_pool, page, _, _ = cache_k.shape
    n_pages = page_table.shape[1]
    past = n_pages * page
    pages_per_blk = MOBA_BLOCK // page
    assert pages_per_blk == 2 and past % MOBA_BLOCK == 0
    nb = past // MOBA_BLOCK
    assert nb >= MOBA_TOPK
    pt = page_table.reshape(-1).astype(jnp.int32)

    ck = cache_k.transpose(0, 2, 3, 1)
    cv = cache_v.transpose(0, 2, 3, 1)

    nk = 16
    lanes = 128
    assert n_pages % GATE_PAGES == 0 and nb <= lanes
    steps_per_seq = n_pages // GATE_PAGES

    def heads(z, rows):
        z = jnp.pad(z, ((0, 0), (0, rows - s), (0, 0)))
        return z.reshape(b, rows, N_HEADS, HEAD_DIM).transpose(0, 2, 1, 3)

    sel = pl.pallas_call(
        functools.partial(_moba_gate_kernel, n_pages=n_pages),
        grid_spec=pltpu.PrefetchScalarGridSpec(
            num_scalar_prefetch=1,
            grid=(b * steps_per_seq,),
            in_specs=[pl.BlockSpec((1, N_HEADS, s, HEAD_DIM), lambda i, pt: (i // steps_per_seq, 0, 0, 0)),
                      pl.BlockSpec(memory_space=pl.ANY)],
            out_specs=pl.BlockSpec((1, N_HEADS * s, lanes), lambda i, pt: (i // steps_per_seq, 0, 0)),
            scratch_shapes=[pltpu.VMEM((2, GATE_PAGES, N_HEADS, HEAD_DIM, page), F32),
                            pltpu.VMEM((N_HEADS, HEAD_DIM, lanes), F32),
                            pltpu.SemaphoreType.DMA((2,))]),
        out_shape=jax.ShapeDtypeStruct((b, N_HEADS * s, lanes), jnp.int32),
        compiler_params=_cparams(("arbitrary",)),
        name="moba_gate",
    )(pt, heads(q, s), ck)

    n_visits = s * MOBA_TOPK
    blk = sel[:, :, :MOBA_TOPK].reshape(-1)

    def bh_spec(r, w):
        return pl.BlockSpec((1, 1, r, w), lambda i, pg, bk: (i // N_HEADS, i % N_HEADS, 0, 0))

    out = pl.pallas_call(
        functools.partial(_moba_attend_kernel, n_visits=n_visits, n_pages=n_pages),
        grid_spec=pltpu.PrefetchScalarGridSpec(
            num_scalar_prefetch=2,
            grid=(b * N_HEADS,),
            in_specs=[pl.BlockSpec((1, 1, 1), lambda i, pg, bk: (i % N_HEADS, 0, 0)),
                      bh_spec(s, HEAD_DIM), bh_spec(nk, HEAD_DIM), bh_spec(nk, HEAD_DIM),
                      pl.BlockSpec(memory_space=pl.ANY), pl.BlockSpec(memory_space=pl.ANY)],
            out_specs=bh_spec(s, HEAD_DIM),
            scratch_shapes=[pltpu.VMEM((2, HEAD_DIM, n_visits * MOBA_BLOCK), F32),
                            pltpu.VMEM((2, HEAD_DIM, n_visits * MOBA_BLOCK), F32),
                            pltpu.SemaphoreType.DMA((2, 2))]),
        out_shape=jax.ShapeDtypeStruct((b, N_HEADS, s, HEAD_DIM), F32),
        compiler_params=_cparams(("arbitrary",)),
        name="moba_attend",
    )(pt, blk, slopes.reshape(N_HEADS, 1, 1), heads(q, s), heads(k, nk), heads(v, nk), ck, cv)
    return out.transpose(0, 2, 1, 3).reshape(b, s, D_HEADS)


def _softplus(z):
    return jnp.maximum(z, 0.0) + jnp.log1p(jnp.exp(-jnp.abs(z)))


def _rwkv_prep_kernel(pb_ref, halo_ref, first_ref, mu_ref, w0_ref, w2_ref, a0_ref, a2_ref, g2_ref, kk_ref, ka_ref,
                      rk_ref, gm_ref, r_o, lw_o, k_o, v_o, kk_o, b_o, g_o, bonus_o, *, seq_len):
    pb = pb_ref[...]
    tm = pb.shape[0]
    row = lax.broadcasted_iota(jnp.int32, (tm, 1), 0)
    if seq_len is None:
        starts, first = row == 0, jnp.where(pl.program_id(0) == 0, first_ref[...], halo_ref[7:8])
    else:
        starts, first = (row % seq_len) == 0, first_ref[...]
    prev = jnp.where(starts, first, pltpu.roll(pb, 1, axis=0))
    xm = pb + mu_ref[...] * (prev - pb)
    d = D_HEADS
    o1, o2, o3 = 3 * d, 3 * d + D_DECAY_LORA, 3 * d + D_DECAY_LORA + D_AAA_LORA
    r, k, v = xm[:, 0:d], xm[:, d:2 * d], xm[:, 2 * d:o1]
    xw, xa, xg = xm[:, o1:o2], xm[:, o2:o3], xm[:, o3:]
    logw = -_softplus(-(w0_ref[...] + _mm(jnp.tanh(xw), w2_ref[...]))) - 0.5
    a = jax.nn.sigmoid(a0_ref[...] + _mm(xa, a2_ref[...]))
    gmat = gm_ref[...]
    kk = k * kk_ref[...]
    kk = kk / jnp.maximum(jnp.sqrt(_head_sums(kk * kk, gmat)), 1e-12)
    k = k * (1.0 + (a - 1.0) * ka_ref[...])
    r_o[...] = r
    lw_o[...] = -jnp.exp(logw)
    k_o[...] = k
    v_o[...] = v
    kk_o[...] = kk
    b_o[...] = kk * a
    g_o[...] = _mm(jax.nn.sigmoid(xg), g2_ref[...])
    bonus_o[...] = _head_sums(r * k * rk_ref[...], gmat) * v


def _rwkv_prep(pb, first, prm, gmat, tm, seq_len):
    m, n = pb.shape
    d = D_HEADS
    if seq_len is None:
        first_spec = pl.BlockSpec((1, n), lambda i: (0, 0))
        halo_spec = pl.BlockSpec((8, n), lambda i: (jnp.maximum(i * (tm // 8) - 1, 0), 0))
    else:
        first_spec = pl.BlockSpec((tm, n), lambda i: (i, 0))
        halo_spec = pl.BlockSpec((8, n), lambda i: (0, 0))

    def full(shape):
        return pl.BlockSpec(shape, lambda i: (0,) * len(shape))

    def vec(z):
        return z.reshape(1, -1)

    outs = pl.pallas_call(
        functools.partial(_rwkv_prep_kernel, seq_len=seq_len),
        grid=(m // tm,),
        in_specs=[pl.BlockSpec((tm, n), lambda i: (i, 0)), halo_spec, first_spec,
                  full((1, n)), full((1, d)), full((D_DECAY_LORA, d)), full((1, d)), full((D_AAA_LORA, d)),
                  full((n - 3 * d - D_DECAY_LORA - D_AAA_LORA, d)), full((1, d)), full((1, d)), full((1, d)),
                  full((d, d))],
        out_specs=[pl.BlockSpec((tm, d), lambda i: (i, 0))] * 8,
        out_shape=[jax.ShapeDtypeStruct((m, d), F32)] * 8,
        compiler_params=_cparams(("parallel",)),
        name="rwkv_prep",
    )(pb, pb, first, vec(prm['rw_mu']), vec(prm['rw_w0']), prm['rw_w2'], vec(prm['rw_a0']), prm['rw_a2'],
      prm['rw_g2'], vec(prm['rw_k_k']), vec(prm['rw_k_a']), vec(prm['rw_r_k']), gmat)
    return outs


def _rwkv_scan_kernel(r_ref, lw_ref, k_ref, v_ref, kk_ref, b_ref, kkT_ref, lwT_ref, vT_ref, s0_ref,
                      y_ref, sfin_ref, s_scr, *, chunk, chunks_per_step):
    j = pl.program_id(1)

    @pl.when(j == 0)
    def _():
        s_scr[...] = s0_ref[0]

    step = chunk * chunks_per_step
    d = D_HEADS
    si = lax.broadcasted_iota(jnp.int32, (step, step), 0)
    sj = lax.broadcasted_iota(jnp.int32, (step, step), 1)
    same = (si // chunk) == (sj // chunk)
    lbd = jnp.where(same & (si >= sj), 1.0, 0.0).astype(BF16)
    ubd = jnp.where(same & (si <= sj), 1.0, 0.0).astype(BF16)
    lw = lw_ref[...]
    lwT = lwT_ref[0]
    cl = sum(_mm(lbd, p) for p in _split3(lw))
    clT = sum(_mm(p, ubd) for p in _split3(lwT))
    tots = [cl[(c + 1) * chunk - 1:(c + 1) * chunk, :] for c in range(chunks_per_step)]
    tot_full = jnp.concatenate([jnp.broadcast_to(t, (chunk, d)) for t in tots], axis=0)
    e_neg = jnp.exp(-cl)
    e_end = jnp.exp(tot_full - cl)
    b = b_ref[...]
    k = k_ref[...]
    rho = r_ref[...] * jnp.exp(cl)
    beta, kap = b * e_neg, k * e_neg
    b_end, k_end = (b * e_end).astype(BF16), (k * e_end).astype(BF16)
    alphaT = kkT_ref[0] * jnp.exp(clT - lwT)
    vT = vT_ref[0].astype(BF16)
    v = v_ref[...].astype(BF16)

    ii = lax.broadcasted_iota(jnp.int32, (chunk, chunk), 0)
    jj = lax.broadcasted_iota(jnp.int32, (chunk, chunk), 1)
    low, upper = ii >= jj, ii < jj
    hd = HEAD_DIM
    units = [(slice(c * chunk, (c + 1) * chunk), slice(h * hd, (h + 1) * hd))
             for c in range(chunks_per_step) for h in range(N_HEADS)]
    bk = [jnp.concatenate([beta[rs, cs], kap[rs, cs]], axis=0).astype(BF16) for rs, cs in units]
    aT = [alphaT[cs, rs] for rs, cs in units]
    g1 = [_mm(x, a.astype(BF16)) for x, a in zip(bk, aT)]
    lt = [jnp.where(upper, g[:chunk], 0.0).astype(BF16) for g in g1]
    akT = [jnp.where(upper, g[chunk:], 0.0).astype(BF16) for g in g1]
    g2 = [_mm_nt(rho[rs, cs].astype(BF16), x) for (rs, cs), x in zip(units, bk)]
    arb = [jnp.where(low, g[:, :chunk], 0.0).astype(BF16) for g in g2]
    ark = [jnp.where(low, g[:, chunk:], 0.0).astype(BF16) for g in g2]
    z = [jnp.concatenate([a, _mm(vT[cs, rs], x)], axis=0) for (rs, cs), a, x in zip(units, aT, akT)]
    z = [x - _mm(x.astype(BF16), l) for x, l in zip(z, lt)]
    pw = lt
    n = 2
    while n < chunk:
        pw = [_mm(p, p).astype(BF16) for p in pw]
        z = [x + _mm(x.astype(BF16), p) for x, p in zip(z, pw)]
        n *= 2
    pz = [(-x).astype(BF16) for x in z]
    g3 = [_mm_nt(a, p) for a, p in zip(arb, pz)]
    g4 = [_mm(p, b_end[rs, cs]) for (rs, cs), p in zip(units, pz)]
    yin = [g[:, hd:] + _mm(a, v[rs, cs]) for (rs, cs), g, a in zip(units, g3, ark)]
    vk = [_mm(vT[cs, rs], k_end[rs, cs]) for rs, cs in units]
    state = [s_scr[h] for h in range(N_HEADS)]
    for c in range(chunks_per_step):
        etot = jnp.exp(tots[c])
        for h in range(N_HEADS):
            u = c * N_HEADS + h
            rs, cs = units[u]
            s = state[h]
            sb = s.astype(BF16)
            y_ref[rs, cs] = _mm_nt((rho[rs, cs] + g3[u][:, :hd]).astype(BF16), sb) + yin[u]
            state[h] = s * etot[:, cs] + _mm(sb, g4[u][:hd].astype(BF16)) + g4[u][hd:] + vk[u]
    for h in range(N_HEADS):
        s_scr[h] = state[h]

    @pl.when(j == pl.num_programs(1) - 1)
    def _():
        sfin_ref[0] = s_scr[...]


def _rwkv_scan(r, lw, k, v, kk, b, s0, seq_len, chunks_per_step):
    m, d = r.shape
    nseq = m // seq_len
    step = WKV_CHUNK * chunks_per_step
    nsteps = seq_len // step

    def tr(z):
        return z.reshape(nseq, seq_len, d).transpose(0, 2, 1)

    tm_spec = pl.BlockSpec((step, d), lambda i, j: (i * nsteps + j, 0))
    fm_spec = pl.BlockSpec((1, d, step), lambda i, j: (i, 0, j))
    st_spec = pl.BlockSpec((1, N_HEADS, HEAD_DIM, HEAD_DIM), lambda i, j: (i, 0, 0, 0))
    return pl.pallas_call(
        functools.partial(_rwkv_scan_kernel, chunk=WKV_CHUNK, chunks_per_step=chunks_per_step),
        grid=(nseq, nsteps),
        in_specs=[tm_spec] * 6 + [fm_spec] * 3 + [st_spec],
        out_specs=[tm_spec, st_spec],
        out_shape=[jax.ShapeDtypeStruct((m, d), F32),
                   jax.ShapeDtypeStruct((nseq, N_HEADS, HEAD_DIM, HEAD_DIM), F32)],
        scratch_shapes=[pltpu.VMEM((N_HEADS, HEAD_DIM, HEAD_DIM), F32)],
        compiler_params=_cparams(("parallel", "arbitrary")),
        name="rwkv_scan",
    )(r, lw, k, v, kk, b, tr(kk), tr(lw), tr(v), s0)


def _l0_out_kernel(y_ref, bonus_ref, g_ref, oa_ref, h_ref, gnw_ref, gnb_ref, gm_ref, wa_ref, wb_ref,
                   gpost_ref, o_ref):
    y = y_ref[...]
    gmat = gm_ref[...]
    inv = 1.0 / HEAD_DIM
    yc = y - _head_sums(y, gmat) * inv
    yn = yc * lax.rsqrt(_head_sums(yc * yc, gmat) * inv + GN_EPS) * gnw_ref[...] + gnb_ref[...]
    ob = (yn + bonus_ref[...]) * g_ref[...]
    mix = _mm(oa_ref[...].astype(BF16), wa_ref[...]) + _mm(ob.astype(BF16), wb_ref[...])
    o_ref[...] = h_ref[...] + _rms(mix, gpost_ref[...])


def _l0_out(y, bonus, g, oa, h, prm, gmat, w_out, tm):
    m, dm = h.shape
    d = D_HEADS

    def rowblk(w):
        return pl.BlockSpec((tm, w), lambda i: (i, 0))

    def full(shape):
        return pl.BlockSpec(shape, lambda i: (0,) * len(shape))

    return pl.pallas_call(
        _l0_out_kernel,
        grid=(m // tm,),
        in_specs=[rowblk(d), rowblk(d), rowblk(d), rowblk(d), rowblk(dm), full((1, d)), full((1, d)),
                  full((d, d)), full((d, dm)), full((d, dm)), full((1, dm))],
        out_specs=rowblk(dm),
        out_shape=jax.ShapeDtypeStruct((m, dm), F32),
        compiler_params=_cparams(("parallel",)),
        name="l0_out",
    )(y, bonus, g, oa, h, prm['rw_gn_w'].reshape(1, d), prm['rw_gn_b'].reshape(1, d), gmat,
      w_out[:d], w_out[d:], prm['l0_norm_post'].reshape(1, dm))


FFN_COLS = 256
FFN_HALO = 16


def _ffn_kernel(h_ref, halo_ref, gpre_ref, wg_ref, wv_ref, cwg_ref, cwv_ref, cbg_ref, cbv_ref, pg_ref, pv_ref,
                wd_ref, gpost_ref, o_ref, xn_scr, acc_scr, *, seq_len):
    i = pl.program_id(0)
    c = pl.program_id(1)
    tm = h_ref.shape[0]

    @pl.when(c == 0)
    def _():
        xn_scr[0:tm] = _rms(h_ref[...], gpre_ref[...]).astype(BF16)
        if seq_len is None:
            xn_scr[tm:tm + FFN_HALO] = _rms(halo_ref[...], gpre_ref[...]).astype(BF16)
        acc_scr[...] = jnp.zeros(acc_scr.shape, F32)

    row = lax.broadcasted_iota(jnp.int32, (tm, 1), 0)
    pos = row if seq_len is None else row % seq_len

    def conv(w_ref, cw_ref, cb_ref, p_ref):
        up = _mm(xn_scr[...], w_ref[...])
        u = up[:tm]
        m1 = pltpu.roll(u, 1, axis=0)
        m2 = pltpu.roll(u, 2, axis=0)
        if seq_len is None:
            last = tm + FFN_HALO
            p0 = jnp.where(i == 0, p_ref[6:7], up[last - 2:last - 1])
            p1 = jnp.where(i == 0, p_ref[7:8], up[last - 1:last])
            m1 = jnp.concatenate([jnp.where(pos[:8] == 0, p1, m1[:8]), m1[8:]], axis=0)
            m2 = jnp.concatenate([jnp.where(pos[:8] == 0, p0, jnp.where(pos[:8] == 1, p1, m2[:8])), m2[8:]], axis=0)
        else:
            m1 = jnp.where(pos == 0, p_ref[1], m1)
            m2 = jnp.where(pos == 0, p_ref[0], jnp.where(pos == 1, p_ref[1], m2))
        return cb_ref[...] + cw_ref[0:1] * m2 + cw_ref[1:2] * m1 + cw_ref[2:3] * u

    gate = conv(wg_ref, cwg_ref, cbg_ref, pg_ref)
    val = conv(wv_ref, cwv_ref, cbv_ref, pv_ref)
    acc_scr[...] += _mm((jax.nn.gelu(gate) * val).astype(BF16), wd_ref[...])

    @pl.when(c == pl.num_programs(1) - 1)
    def _():
        o_ref[...] = h_ref[...] + _rms(acc_scr[...], gpost_ref[...])


def _ffn(h, prev_rows, g_pre, w_up, conv_w, conv_b, w_down, g_post, tm, seq_len):
    m, dm = h.shape
    dff = w_down.shape[0]
    n2 = 2 * dff
    assert dff % FFN_COLS == 0
    nc = dff // FFN_COLS
    if seq_len is None:
        prev = jnp.pad(prev_rows[0], ((6, 0), (0, 0)))

        def p_spec(off):
            return pl.BlockSpec((8, FFN_COLS), lambda i, c: (0, off + c))

        halo_spec = pl.BlockSpec((FFN_HALO, dm), lambda i, c: (jnp.maximum(i * (tm // FFN_HALO) - 1, 0), 0))
        xn_rows = tm + FFN_HALO
    else:
        prev = jnp.stack([jnp.repeat(prev_rows[:, 0], seq_len, axis=0), jnp.repeat(prev_rows[:, 1], seq_len, axis=0)])

        def p_spec(off):
            return pl.BlockSpec((2, tm, FFN_COLS), lambda i, c: (0, i, off + c))

        halo_spec = pl.BlockSpec((FFN_HALO, dm), lambda i, c: (0, 0))
        xn_rows = tm

    def col_spec(rows, off):
        return pl.BlockSpec((rows, FFN_COLS), lambda i, c: (0, off + c))

    def full(shape):
        return pl.BlockSpec(shape, lambda i, c: (0,) * len(shape))

    cb = conv_b.reshape(1, n2)
    return pl.pallas_call(
        functools.partial(_ffn_kernel, seq_len=seq_len),
        grid=(m // tm, nc),
        in_specs=[pl.BlockSpec((tm, dm), lambda i, c: (i, 0)), halo_spec, full((1, dm)),
                  col_spec(dm, 0), col_spec(dm, nc), col_spec(CONV_W, 0), col_spec(CONV_W, nc),
                  col_spec(1, 0), col_spec(1, nc), p_spec(0), p_spec(nc),
                  pl.BlockSpec((FFN_COLS, dm), lambda i, c: (c, 0)), full((1, dm))],
        out_specs=pl.BlockSpec((tm, dm), lambda i, c: (i, 0)),
        out_shape=jax.ShapeDtypeStruct((m, dm), F32),
        scratch_shapes=[pltpu.VMEM((xn_rows, dm), BF16), pltpu.VMEM((tm, dm), F32)],
        compiler_params=_cparams(("parallel", "arbitrary")),
        name="ffn",
    )(h, h, g_pre.reshape(1, dm), w_up, w_up, conv_w, conv_w, cb, cb, prev, prev, w_down, g_post.reshape(1, dm))


def _gmlp_kernel(h_ref, gpre_ref, win_ref, lnw_ref, lnb_ref, wmix_ref, bias_ref, wout_ref, gpost_ref,
                 o_ref, cv_ref, gated_scr):
    h = h_ref[...]
    tm = h.shape[0]
    dc = wout_ref.shape[0]
    rm = wmix_ref.shape[1]
    gw = dc // GMLP_GROUPS
    z = jax.nn.gelu(_mm(_rms(h, gpre_ref[...]).astype(BF16), win_ref[...]))
    v = z[:, dc:]
    vc = v - jnp.mean(v, axis=-1, keepdims=True)
    vn = vc * lax.rsqrt(jnp.mean(vc * vc, axis=-1, keepdims=True) + LN_EPS) * lnw_ref[...] + lnb_ref[...]
    cv_ref[...] = vn
    for s in range(tm // rm):
        rows = slice(s * rm, (s + 1) * rm)
        for g in range(GMLP_GROUPS):
            cols = slice(g * gw, (g + 1) * gw)
            mixed = _mm(wmix_ref[g], vn[rows, cols].astype(BF16)) + bias_ref[:, cols]
            gated_scr[rows, cols] = (z[rows, cols] * mixed).astype(BF16)
    o_ref[...] = h + _rms(_mm(gated_scr[...], wout_ref[...]), gpost_ref[...])


def _gmlp(h, g_pre, w_in, ln_w, ln_b, wmix, bias, w_out, g_post, tm):
    m, dm = h.shape
    dc = w_out.shape[0]
    rm = wmix.shape[1]

    def full(shape):
        return pl.BlockSpec(shape, lambda i: (0,) * len(shape))

    return pl.pallas_call(
        _gmlp_kernel,
        grid=(m // tm,),
        in_specs=[pl.BlockSpec((tm, dm), lambda i: (i, 0)), full((1, dm)), full((dm, 2 * dc)), full((1, dc)),
                  full((1, dc)), full((GMLP_GROUPS, rm, rm)), full((rm, dc)), full((dc, dm)), full((1, dm))],
        out_specs=[pl.BlockSpec((tm, dm), lambda i: (i, 0)), pl.BlockSpec((tm, dc), lambda i: (i, 0))],
        out_shape=[jax.ShapeDtypeStruct((m, dm), F32), jax.ShapeDtypeStruct((m, dc), F32)],
        scratch_shapes=[pltpu.VMEM((tm, dc), BF16)],
        compiler_params=_cparams(("parallel",)),
        name="gmlp",
    )(h, g_pre.reshape(1, dm), w_in, ln_w.reshape(1, dc), ln_b.reshape(1, dc), wmix, bias, w_out,
      g_post.reshape(1, dm))


def _trunk(x, attend, prev_xn, wkv0, conv0, prm, w, *, tile, long_seq):
    nseq, t, dm = x.shape
    m = nseq * t
    h = x.reshape(m, dm)
    seq_len = None if long_seq else t

    g0 = prm['l0_norm_pre']
    q, k_new, v_new = _norm_matmul_split(h, g0, w['w_qkv'], tile, 3)
    pb = _norm_matmul(h, g0, w['w_b'], tile, w['w_b'].shape[1] // 2)
    shift_new, prev_proj = _shift_rows(x[:, -1], g0, prev_xn, w['w_b'])
    first = prev_proj if long_seq else jnp.repeat(prev_proj, t, axis=0)
    r, lw, kmod, vv, kk, bb, gate, bonus = _rwkv_prep(pb, first, prm, w['gmat'], tile, seq_len)
    o_a = attend(q, k_new, v_new)

    t_pad = -(-t // WKV_CHUNK) * WKV_CHUNK

    def padt(z):
        if t_pad == t:
            return z
        return jnp.pad(z.reshape(nseq, t, -1), ((0, 0), (0, t_pad - t), (0, 0))).reshape(nseq * t_pad, -1)

    y, wkv_new = _rwkv_scan(padt(r), padt(lw), padt(kmod), padt(vv), padt(kk), padt(bb), wkv0, t_pad,
                            4 if t_pad % (4 * WKV_CHUNK) == 0 else 1)
    if t_pad != t:
        y = y.reshape(nseq, t_pad, -1)[:, :t].reshape(m, -1)
    h = _l0_out(y, bonus, gate, o_a, h, prm, w['gmat'], w['w_out0'], tile)

    conv_rows = []
    ffn_tile = min(m, 1024)

    def ffn(h, layer):
        g_pre = prm['ffn_norm_pre'][layer]
        tail = h.reshape(nseq, t, dm)[:, t - min(t, 8):].reshape(-1, dm)
        up_tail = _norm_matmul(tail, g_pre, w['w_up'][layer], tail.shape[0], 512)
        conv_rows.append(up_tail.reshape(nseq, min(t, 8), -1)[:, -(CONV_W - 1):])
        return _ffn(h, conv0[layer], g_pre, w['w_up'][layer], prm['ffn_conv_w'][layer], prm['ffn_conv_b'][layer],
                    w['w_down'][layer], prm['ffn_norm_post'][layer], ffn_tile, seq_len)

    h = ffn(h, 0)

    if long_seq:
        wmix, bias = w['wmix'], w['gbias']
    else:
        eye = jnp.eye(tile // t, dtype=F32)
        wmix = jnp.einsum('ab,gij->gaibj', eye, w['wmix'][:, :t, :t].astype(F32))
        wmix = wmix.reshape(GMLP_GROUPS, tile, tile).astype(BF16)
        bias = jnp.tile(w['gbias'][:t], (tile // t, 1))
    h, chunk_v = _gmlp(h, prm['l1_norm_pre'], w['w_in1'], prm['gm_ln_w'], prm['gm_ln_b'], wmix, bias,
                       w['w_out1'], prm['l1_norm_post'], tile)
    h = ffn(h, 1)
    return (h.reshape(nseq, t, dm), k_new.reshape(nseq, t, N_HEADS, HEAD_DIM),
            v_new.reshape(nseq, t, N_HEADS, HEAD_DIM), wkv_new, shift_new, chunk_v.reshape(nseq, t, -1),
            jnp.stack(conv_rows))


def kernel(x_prompt, x_sample, cache_k, cache_v, state_wkv, state_shift, state_ffn_conv, page_table,
           l0_norm_pre, l0_norm_post, l0_w_in, rw_mu, rw_w0, rw_w2, rw_a0, rw_a2, rw_g2, rw_k_k, rw_k_a,
           rw_r_k, rw_gn_w, rw_gn_b, l0_w_out, l1_norm_pre, l1_norm_post, l1_w_in, gm_ln_w, gm_ln_b,
           gm_ws, gm_bs, l1_w_out, ffn_norm_pre, ffn_norm_post, ffn_w_up, ffn_conv_w, ffn_conv_b,
           ffn_w_down):
    prm = dict(l0_norm_pre=l0_norm_pre, l0_norm_post=l0_norm_post, rw_mu=rw_mu, rw_w0=rw_w0, rw_w2=rw_w2,
               rw_a0=rw_a0, rw_a2=rw_a2, rw_g2=rw_g2, rw_k_k=rw_k_k, rw_k_a=rw_k_a, rw_r_k=rw_r_k,
               rw_gn_w=rw_gn_w, rw_gn_b=rw_gn_b, l1_norm_pre=l1_norm_pre, l1_norm_post=l1_norm_post,
               gm_ln_w=gm_ln_w, gm_ln_b=gm_ln_b, ffn_norm_pre=ffn_norm_pre, ffn_norm_post=ffn_norm_post,
               ffn_conv_w=ffn_conv_w, ffn_conv_b=ffn_conv_b)
    d_a = D_HEADS
    dc = l1_w_out.shape[0]
    tri = jnp.tril(jnp.ones((GMLP_CHUNK, GMLP_CHUNK), dtype=bool))
    hd = jnp.arange(D_HEADS) // HEAD_DIM
    w = dict(
        w_qkv=l0_w_in[:, :3 * d_a].astype(BF16),
        w_b=l0_w_in[:, 3 * d_a:].astype(BF16),
        w_out0=l0_w_out.astype(BF16),
        w_up=ffn_w_up.astype(BF16),
        w_down=ffn_w_down.astype(BF16),
        w_in1=l1_w_in.astype(BF16),
        w_out1=l1_w_out.astype(BF16),
        wmix=jnp.where(tri[None], gm_ws, 0.0).astype(BF16),
        gbias=jnp.repeat(gm_bs.T, dc // GMLP_GROUPS, axis=1),
        gmat=(hd[:, None] == hd[None, :]).astype(BF16),
    )
    slopes = jnp.exp2(-8.0 * jnp.arange(1, N_HEADS + 1, dtype=F32) / N_HEADS)

    bp, tp, dm = x_prompt.shape
    assert bp == 1
    dt = x_prompt.dtype
    depth = ffn_w_up.shape[0]
    wkv0 = jnp.zeros((bp, N_HEADS, HEAD_DIM, HEAD_DIM), dt)
    shift0 = jnp.zeros((bp, dm), dt)
    conv0 = jnp.zeros((depth, bp, CONV_W - 1, ffn_w_up.shape[2]), dt)

    def attend_prompt(q, k, v):
        return _moba_prompt(q, k, v, slopes)

    (y_prompt, k_prompt, v_prompt, wkv_prompt, shift_prompt, _unused_chunk_v,
     ffn_conv_prompt) = _trunk(x_prompt, attend_prompt, shift0, wkv0, conv0, prm, w, tile=512, long_seq=True)

    bs, ts, _ = x_sample.shape

    def attend_sample(q, k, v):
        o = _moba_sample(q.reshape(bs, ts, -1), k.reshape(bs, ts, -1), v.reshape(bs, ts, -1), cache_k, cache_v,
                         page_table, slopes)
        return o.reshape(bs * ts, -1)

    (y_sample, k_sample, v_sample, wkv_sample, shift_sample, chunk_v_sample,
     ffn_conv_sample) = _trunk(x_sample, attend_sample, state_shift, state_wkv, state_ffn_conv, prm, w,
                               tile=bs * ts, long_seq=False)

    return (y_prompt, y_sample, k_prompt, v_prompt, k_sample, v_sample, wkv_prompt, wkv_sample,
            shift_prompt, shift_sample, chunk_v_sample, ffn_conv_prompt, ffn_conv_sample)
```

```python
import functools

import jax
import jax.numpy as jnp
from jax import lax
from jax.experimental import pallas as pl
from jax.experimental.pallas import tpu as pltpu

F32 = jnp.float32
BF16 = jnp.bfloat16
HI = lax.Precision.HIGHEST

HEAD_DIM = 64
N_HEADS = 8
D_HEADS = N_HEADS * HEAD_DIM
MOBA_BLOCK = 256
MOBA_TOPK = 3
MOBA_GROUP = 8
SCALE = HEAD_DIM ** -0.5
D_DECAY_LORA = 64
D_AAA_LORA = 64
GMLP_GROUPS = 8
GMLP_CHUNK = 128
CONV_W = 3
NORM_EPS = 1e-6
LN_EPS = 1e-5
GN_EPS = 64e-5
LOG2E = 1.4426950408889634
MOBA_KAUG = 128
MOBA_VAUG = 80
NEG = -1e30
WKV_CHUNK = 64
V7X_VMEM_LIMIT = 56 * 1024 * 1024


def _cparams(sem, vmem=V7X_VMEM_LIMIT):
    return pltpu.CompilerParams(dimension_semantics=sem, vmem_limit_bytes=vmem)


def _mm(a, b, prec=None):
    return jnp.dot(a, b, precision=prec, preferred_element_type=F32)


def _mm_nt(a, b, prec=None):
    return lax.dot_general(a, b, (((1,), (1,)), ((), ())), precision=prec, preferred_element_type=F32)


def _split3(x):
    p1 = x.astype(BF16)
    r1 = x - p1.astype(F32)
    p2 = r1.astype(BF16)
    p3 = (r1 - p2.astype(F32)).astype(BF16)
    return p1, p2, p3


def _head_sums(x, gmat):
    return sum(_mm(p, gmat) for p in _split3(x))


def _rms(x, g):
    return x * lax.rsqrt(jnp.mean(x * x, axis=-1, keepdims=True) + NORM_EPS) * g


def _norm_matmul_kernel(x_ref, g_ref, w_ref, o_ref, xn_ref):
    @pl.when(pl.program_id(1) == 0)
    def _():
        xn_ref[...] = _rms(x_ref[...], g_ref[...]).astype(BF16)

    o_ref[...] = _mm(xn_ref[...], w_ref[...])


def _norm_matmul(x, g, w, tm, tn):
    m, kd = x.shape
    n = w.shape[1]
    return pl.pallas_call(
        _norm_matmul_kernel,
        grid=(m // tm, n // tn),
        in_specs=[pl.BlockSpec((tm, kd), lambda i, j: (i, 0)),
                  pl.BlockSpec((1, kd), lambda i, j: (0, 0)),
                  pl.BlockSpec((kd, tn), lambda i, j: (0, j))],
        out_specs=pl.BlockSpec((tm, tn), lambda i, j: (i, j)),
        out_shape=jax.ShapeDtypeStruct((m, n), F32),
        scratch_shapes=[pltpu.VMEM((tm, kd), BF16)],
        compiler_params=_cparams(("parallel", "arbitrary")),
        name="norm_matmul",
    )(x, g.reshape(1, kd), w)


def _norm_matmul_split_kernel(x_ref, g_ref, w_ref, *refs):
    outs, xn_ref = refs[:-1], refs[-1]
    j = pl.program_id(1)

    @pl.when(j == 0)
    def _():
        xn_ref[...] = _rms(x_ref[...], g_ref[...]).astype(BF16)

    y = _mm(xn_ref[...], w_ref[...])
    for n, o_ref in enumerate(outs):
        @pl.when(j == n)
        def _(o_ref=o_ref):
            o_ref[...] = y


def _norm_matmul_split(x, g, w, tm, n_out):
    m, kd = x.shape
    tn = w.shape[1] // n_out
    return pl.pallas_call(
        _norm_matmul_split_kernel,
        grid=(m // tm, n_out),
        in_specs=[pl.BlockSpec((tm, kd), lambda i, j: (i, 0)),
                  pl.BlockSpec((1, kd), lambda i, j: (0, 0)),
                  pl.BlockSpec((kd, tn), lambda i, j: (0, j))],
        out_specs=[pl.BlockSpec((tm, tn), lambda i, j: (i, 0))] * n_out,
        out_shape=[jax.ShapeDtypeStruct((m, tn), F32)] * n_out,
        scratch_shapes=[pltpu.VMEM((tm, kd), BF16)],
        compiler_params=_cparams(("parallel", "arbitrary")),
        name="norm_matmul_split",
    )(x, g.reshape(1, kd), w)


def _shift_rows_kernel(x_ref, g_ref, prev_ref, w_ref, xn_ref, pp_ref):
    xn_ref[...] = _rms(x_ref[...], g_ref[...])
    pp_ref[...] = _mm(prev_ref[...].astype(BF16), w_ref[...])


def _shift_rows(x_last, g, prev_xn, w_b):
    b, kd = x_last.shape
    n = w_b.shape[1]
    bp = -(-b // 8) * 8
    pad = ((0, bp - b), (0, 0))
    xn, pp = pl.pallas_call(
        _shift_rows_kernel,
        out_shape=(jax.ShapeDtypeStruct((bp, kd), F32), jax.ShapeDtypeStruct((bp, n), F32)),
        compiler_params=_cparams(None),
        name="shift_rows",
    )(jnp.pad(x_last, pad), g.reshape(1, kd), jnp.pad(prev_xn, pad), w_b)
    return xn[:b], pp[:b]


def _kmeans_kernel(k_ref, o_ref):
    k = k_ref[0]
    nb = k.shape[0] // MOBA_BLOCK
    o_ref[0] = jnp.mean(k.reshape(nb, MOBA_BLOCK, HEAD_DIM), axis=1)


def _moba_prompt_kernel(slope_ref, qT_ref, km_ref, k_ref, vT_ref, oT_ref, sel_scr):
    i = pl.program_id(1)
    nb = km_ref.shape[1]
    blk = MOBA_BLOCK
    qf = qT_ref[0]
    gate = _mm(km_ref[0], qf, HI)
    nidx = lax.broadcasted_iota(jnp.int32, gate.shape, 0)
    g = jnp.where(nidx < i, gate, -jnp.inf)
    sel = jnp.zeros(gate.shape, F32)
    for _ in range(MOBA_TOPK):
        top = jnp.max(g, axis=0, keepdims=True)
        idx = jnp.min(jnp.where(g == top, nidx, nb), axis=0, keepdims=True)
        pick = (nidx == idx) & (top > -jnp.inf)
        sel = jnp.where(pick, 1.0, sel)
        g = jnp.where(pick, -jnp.inf, g)
    sel_scr[...] = sel

    slope2 = slope_ref[0] * LOG2E
    pieces = _split3(jnp.broadcast_to(slope2, (1, blk)))
    r16 = lax.broadcasted_iota(jnp.int32, (16, blk), 0)
    aug = jnp.zeros((16, blk), F32)
    for j, piece in enumerate(pieces):
        aug = jnp.where(r16 == j, piece.astype(F32), aug)
    qs = jnp.concatenate([(qf * (SCALE * LOG2E)).astype(BF16), aug.astype(BF16),
                          jnp.zeros((MOBA_KAUG - HEAD_DIM - 16, blk), BF16)], axis=0)
    ki = lax.broadcasted_iota(jnp.int32, (blk, blk), 0)
    qi = lax.broadcasted_iota(jnp.int32, (blk, blk), 1)

    own = pl.multiple_of(i * blk, blk)
    s = jnp.where(ki <= qi, _mm(k_ref[0, pl.ds(own, blk), :], qs), NEG)
    m0 = jnp.max(s, axis=0, keepdims=True)
    acc0 = _mm(vT_ref[0, :, pl.ds(own, blk)], jnp.exp2(s - m0).astype(BF16))

    grp = MOBA_GROUP
    gk = grp * blk

    def qk(g):
        start = pl.multiple_of(jnp.minimum(g, nb // grp - 1) * gk, gk)
        return _mm(k_ref[0, pl.ds(start, gk), :], qs)

    def block_max(s):
        return tuple(jnp.max(s[u * blk:(u + 1) * blk], axis=0, keepdims=True) for u in range(grp))

    def body(g, carry):
        m, acc, s, mbs = carry
        s_next = qk(g + 1)
        n0 = pl.multiple_of(g * grp, grp)
        picked = [sel_scr[pl.ds(n0 + u, 1), :] > 0.0 for u in range(grp)]
        shift = [slope2 * ((n0 + u - i) * blk).astype(F32) for u in range(grp)]
        m_new = m
        for u in range(grp):
            m_new = jnp.where(picked[u], jnp.maximum(m_new, mbs[u] + shift[u]), m_new)
        acc = jnp.exp2(m - m_new) * acc
        for u in range(grp):
            off = jnp.where(picked[u], m_new - shift[u], -NEG)
            p = jnp.exp2(s[u * blk:(u + 1) * blk] - off).astype(BF16)
            start = pl.multiple_of((n0 + u) * blk, blk)
            acc = acc + _mm(vT_ref[0, :, pl.ds(start, blk)], p)
        return m_new, acc, s_next, block_max(s_next)

    s0 = qk(0)
    _, acc, _, _ = lax.fori_loop(0, (i + grp - 1) // grp, body, (m0, acc0, s0, block_max(s0)))
    oT_ref[0] = acc[:HEAD_DIM] / acc[HEAD_DIM:HEAD_DIM + 1]


def _moba_prompt(q, k, v, slopes):
    t = q.shape[0]
    assert t % (MOBA_BLOCK * MOBA_GROUP) == 0
    nb = t // MOBA_BLOCK

    def heads_t(z):
        return z.reshape(t, N_HEADS, HEAD_DIM).transpose(1, 2, 0)

    k_hm = k.reshape(t, N_HEADS, HEAD_DIM).transpose(1, 0, 2)
    offs = jnp.broadcast_to((jnp.arange(t) % MOBA_BLOCK).astype(BF16)[None, :, None], (N_HEADS, t, 3))
    k_aug = jnp.concatenate([k_hm.astype(BF16), offs,
                             jnp.zeros((N_HEADS, t, MOBA_KAUG - HEAD_DIM - 3), BF16)], axis=-1)
    v_aug = jnp.concatenate([heads_t(v).astype(BF16), jnp.ones((N_HEADS, 1, t), BF16),
                             jnp.zeros((N_HEADS, MOBA_VAUG - HEAD_DIM - 1, t), BF16)], axis=1)
    kmean = pl.pallas_call(
        _kmeans_kernel,
        grid=(N_HEADS,),
        in_specs=[pl.BlockSpec((1, t, HEAD_DIM), lambda h: (h, 0, 0))],
        out_specs=pl.BlockSpec((1, nb, HEAD_DIM), lambda h: (h, 0, 0)),
        out_shape=jax.ShapeDtypeStruct((N_HEADS, nb, HEAD_DIM), F32),
        compiler_params=_cparams(("parallel",)),
        name="moba_kmeans",
    )(k_hm)
    oT = pl.pallas_call(
        _moba_prompt_kernel,
        grid=(N_HEADS, nb),
        in_specs=[pl.BlockSpec((1, 1, 1), lambda h, i: (h, 0, 0)),
                  pl.BlockSpec((1, HEAD_DIM, MOBA_BLOCK), lambda h, i: (h, 0, i)),
                  pl.BlockSpec((1, nb, HEAD_DIM), lambda h, i: (h, 0, 0)),
                  pl.BlockSpec((1, t, MOBA_KAUG), lambda h, i: (h, 0, 0)),
                  pl.BlockSpec((1, MOBA_VAUG, t), lambda h, i: (h, 0, 0))],
        out_specs=pl.BlockSpec((1, HEAD_DIM, MOBA_BLOCK), lambda h, i: (h, 0, i)),
        out_shape=jax.ShapeDtypeStruct((N_HEADS, HEAD_DIM, t), F32),
        scratch_shapes=[pltpu.VMEM((nb, MOBA_BLOCK), F32)],
        compiler_params=_cparams(("parallel", "arbitrary")),
        name="moba_prompt",
    )(slopes.reshape(N_HEADS, 1, 1), heads_t(q), kmean, k_aug, v_aug)
    return oT.transpose(2, 0, 1).reshape(t, D_HEADS)


GATE_PAGES = 16


def _moba_gate_kernel(pt_ref, q_ref, ck_ref, sel_ref, kbuf, km_scr, sem, *, n_pages):
    i = pl.program_id(0)
    nsteps = pl.num_programs(0)
    steps_per_seq = n_pages // GATE_PAGES
    g = i % steps_per_seq
    blk_pages = MOBA_BLOCK // ck_ref.shape[3]
    nb = n_pages // blk_pages

    def copies(step, slot):
        return [pltpu.make_async_copy(ck_ref.at[pt_ref[step * GATE_PAGES + c]], kbuf.at[slot, c], sem.at[slot])
                for c in range(GATE_PAGES)]

    @pl.when(i == 0)
    def _():
        for cp in copies(0, 0):
            cp.start()

    @pl.when(i + 1 < nsteps)
    def _():
        for cp in copies(i + 1, (i + 1) % 2):
            cp.start()

    slot = i % 2
    for cp in copies(i, slot):
        cp.wait()

    @pl.when(g == 0)
    def _():
        km_scr[...] = jnp.zeros(km_scr.shape, F32)

    lane = lax.broadcasted_iota(jnp.int32, km_scr.shape, 2)
    acc = km_scr[...]
    for b in range(GATE_PAGES // blk_pages):
        x = kbuf[slot, blk_pages * b]
        for j in range(1, blk_pages):
            x = x + kbuf[slot, blk_pages * b + j]
        acc = jnp.where(lane == g * (GATE_PAGES // blk_pages) + b, jnp.sum(x, axis=-1, keepdims=True), acc)
    km_scr[...] = acc

    @pl.when(g == steps_per_seq - 1)
    def _():
        km = acc * (1.0 / MOBA_BLOCK)
        gate = jnp.concatenate([_mm(q_ref[0, h], km[h], HI) for h in range(N_HEADS)], axis=0)
        bidx = lax.broadcasted_iota(jnp.int32, gate.shape, 1)
        gt = jnp.where(bidx < nb, gate, -jnp.inf)
        sel = jnp.zeros(gate.shape, jnp.int32)
        for r in range(MOBA_TOPK):
            top = jnp.max(gt, axis=1, keepdims=True)
            idx = jnp.min(jnp.where(gt == top, bidx, nb), axis=1, keepdims=True)
            sel = jnp.where(bidx == r, idx, sel)
            gt = jnp.where(bidx == idx, -jnp.inf, gt)
        sel_ref[0] = sel


def _moba_attend_kernel(pt_ref, blk_ref, slope_ref, q_ref, kn_ref, vn_ref, ck_ref, cv_ref, o_ref,
                        kbuf, vbuf, sem, *, n_visits, n_pages):
    i = pl.program_id(0)
    nsteps = pl.num_programs(0)
    page = ck_ref.shape[3]
    blk_pages = MOBA_BLOCK // page
    n_copies = n_visits * blk_pages
    past = n_pages * page

    def copies(step, slot):
        h = step % N_HEADS
        out = []
        for c in range(n_copies):
            blk_id = blk_ref[step * n_visits + c // blk_pages]
            pg = pt_ref[(step // N_HEADS) * n_pages + blk_id * blk_pages + c % blk_pages]
            cols = pl.ds(c * page, page)
            out.append(pltpu.make_async_copy(ck_ref.at[pg, h], kbuf.at[slot, :, cols], sem.at[0, slot]))
            out.append(pltpu.make_async_copy(cv_ref.at[pg, h], vbuf.at[slot, :, cols], sem.at[1, slot]))
        return out

    @pl.when(i == 0)
    def _():
        for cp in copies(0, 0):
            cp.start()

    @pl.when(i + 1 < nsteps)
    def _():
        for cp in copies(i + 1, (i + 1) % 2):
            cp.start()

    slot = i % 2
    for cp in copies(i, slot):
        cp.wait()

    nq = q_ref.shape[2]
    slope = slope_ref[0]
    qs = (q_ref[0, 0] * SCALE).astype(BF16)
    ncol = n_visits * MOBA_BLOCK
    row = lax.broadcasted_iota(jnp.int32, (nq, ncol), 0)
    col = lax.broadcasted_iota(jnp.int32, (nq, ncol), 1)
    vis = lax.broadcasted_iota(jnp.int32, (1, ncol), 1) // MOBA_BLOCK
    kstart = jnp.zeros((1, ncol), jnp.int32)
    for v in range(n_visits):
        kstart = jnp.where(vis == v, blk_ref[i * n_visits + v] * MOBA_BLOCK, kstart)
    kpos = kstart + lax.broadcasted_iota(jnp.int32, (1, ncol), 1) % MOBA_BLOCK
    dist = (past + vis // (n_visits // nq) - kpos).astype(F32)
    s = _mm(qs, kbuf[slot].astype(BF16)) - slope * dist
    s = jnp.where(col // (ncol // nq) == row, s, NEG)
    nk = kn_ref.shape[2]
    dist_o = (lax.broadcasted_iota(jnp.int32, (nq, nk), 0)
              - lax.broadcasted_iota(jnp.int32, (nq, nk), 1)).astype(F32)
    s_o = jnp.where(dist_o >= 0.0, _mm_nt(qs, kn_ref[0, 0].astype(BF16)) - slope * dist_o, NEG)
    m = jnp.maximum(jnp.max(s, axis=1, keepdims=True), jnp.max(s_o, axis=1, keepdims=True))
    p = jnp.exp(s - m)
    p_o = jnp.exp(s_o - m)
    l = jnp.sum(p, axis=1, keepdims=True) + jnp.sum(p_o, axis=1, keepdims=True)
    acc = _mm_nt(p.astype(BF16), vbuf[slot].astype(BF16)) + _mm(p_o.astype(BF16), vn_ref[0, 0].astype(BF16))
    o_ref[0, 0] = acc / l


def _moba_sample(q, k, v, cache_k, cache_v, page_table, slopes):
    b, s, _ = q.shape
    n_pool, page, _, _ = cache_k.shape
    n_pages = page_table.shape[1]
    past = n_pages * page
    pages_per_blk = MOBA_BLOCK // page
    assert pages_per_blk == 2 and past % MOBA_BLOCK == 0
    nb = past // MOBA_BLOCK
    assert nb >= MOBA_TOPK
    pt = page_table.reshape(-1).astype(jnp.int32)

    ck = cache_k.transpose(0, 2, 3, 1)
    cv = cache_v.transpose(0, 2, 3, 1)

    nk = 16
    lanes = 128
    assert n_pages % GATE_PAGES == 0 and nb <= lanes
    steps_per_seq = n_pages // GATE_PAGES

    def heads(z, rows):
        z = jnp.pad(z, ((0, 0), (0, rows - s), (0, 0)))
        return z.reshape(b, rows, N_HEADS, HEAD_DIM).transpose(0, 2, 1, 3)

    sel = pl.pallas_call(
        functools.partial(_moba_gate_kernel, n_pages=n_pages),
        grid_spec=pltpu.PrefetchScalarGridSpec(
            num_scalar_prefetch=1,
            grid=(b * steps_per_seq,),
            in_specs=[pl.BlockSpec((1, N_HEADS, s, HEAD_DIM), lambda i, pt: (i // steps_per_seq, 0, 0, 0)),
                      pl.BlockSpec(memory_space=pl.ANY)],
            out_specs=pl.BlockSpec((1, N_HEADS * s, lanes), lambda i, pt: (i // steps_per_seq, 0, 0)),
            scratch_shapes=[pltpu.VMEM((2, GATE_PAGES, N_HEADS, HEAD_DIM, page), F32),
                            pltpu.VMEM((N_HEADS, HEAD_DIM, lanes), F32),
                            pltpu.SemaphoreType.DMA((2,))]),
        out_shape=jax.ShapeDtypeStruct((b, N_HEADS * s, lanes), jnp.int32),
        compiler_params=_cparams(("arbitrary",)),
        name="moba_gate",
    )(pt, heads(q, s), ck)

    n_visits = s * MOBA_TOPK
    blk = sel[:, :, :MOBA_TOPK].reshape(-1)

    def bh_spec(r, w):
        return pl.BlockSpec((1, 1, r, w), lambda i, pg, bk: (i // N_HEADS, i % N_HEADS, 0, 0))

    out = pl.pallas_call(
        functools.partial(_moba_attend_kernel, n_visits=n_visits, n_pages=n_pages),
        grid_spec=pltpu.PrefetchScalarGridSpec(
            num_scalar_prefetch=2,
            grid=(b * N_HEADS,),
            in_specs=[pl.BlockSpec((1, 1, 1), lambda i, pg, bk: (i % N_HEADS, 0, 0)),
                      bh_spec(s, HEAD_DIM), bh_spec(nk, HEAD_DIM), bh_spec(nk, HEAD_DIM),
                      pl.BlockSpec(memory_space=pl.ANY), pl.BlockSpec(memory_space=pl.ANY)],
            out_specs=bh_spec(s, HEAD_DIM),
            scratch_shapes=[pltpu.VMEM((2, HEAD_DIM, n_visits * MOBA_BLOCK), F32),
                            pltpu.VMEM((2, HEAD_DIM, n_visits * MOBA_BLOCK), F32),
                            pltpu.SemaphoreType.DMA((2, 2))]),
        out_shape=jax.ShapeDtypeStruct((b, N_HEADS, s, HEAD_DIM), F32),
        compiler_params=_cparams(("arbitrary",)),
        name="moba_attend",
    )(pt, blk, slopes.reshape(N_HEADS, 1, 1), heads(q, s), heads(k, nk), heads(v, nk), ck, cv)
    return out.transpose(0, 2, 1, 3).reshape(b, s, D_HEADS)


def _softplus(z):
    return jnp.maximum(z, 0.0) + jnp.log1p(jnp.exp(-jnp.abs(z)))


def _rwkv_prep_kernel(pb_ref, halo_ref, first_ref, mu_ref, w0_ref, w2_ref, a0_ref, a2_ref, g2_ref, kk_ref, ka_ref,
                      rk_ref, gm_ref, r_o, lw_o, k_o, v_o, kk_o, b_o, g_o, bonus_o, *, seq_len):
    pb = pb_ref[...]
    tm = pb.shape[0]
    row = lax.broadcasted_iota(jnp.int32, (tm, 1), 0)
    if seq_len is None:
        starts, first = row == 0, jnp.where(pl.program_id(0) == 0, first_ref[...], halo_ref[7:8])
    else:
        starts, first = (row % seq_len) == 0, first_ref[...]
    prev = jnp.where(starts, first, pltpu.roll(pb, 1, axis=0))
    xm = pb + mu_ref[...] * (prev - pb)
    d = D_HEADS
    o1, o2, o3 = 3 * d, 3 * d + D_DECAY_LORA, 3 * d + D_DECAY_LORA + D_AAA_LORA
    r, k, v = xm[:, 0:d], xm[:, d:2 * d], xm[:, 2 * d:o1]
    xw, xa, xg = xm[:, o1:o2], xm[:, o2:o3], xm[:, o3:]
    logw = -_softplus(-(w0_ref[...] + _mm(jnp.tanh(xw), w2_ref[...]))) - 0.5
    a = jax.nn.sigmoid(a0_ref[...] + _mm(xa, a2_ref[...]))
    gmat = gm_ref[...]
    kk = k * kk_ref[...]
    kk = kk / jnp.maximum(jnp.sqrt(_head_sums(kk * kk, gmat)), 1e-12)
    k = k * (1.0 + (a - 1.0) * ka_ref[...])
    r_o[...] = r
    lw_o[...] = -jnp.exp(logw)
    k_o[...] = k
    v_o[...] = v
    kk_o[...] = kk
    b_o[...] = kk * a
    g_o[...] = _mm(jax.nn.sigmoid(xg), g2_ref[...])
    bonus_o[...] = _head_sums(r * k * rk_ref[...], gmat) * v


def _rwkv_prep(pb, first, prm, gmat, tm, seq_len):
    m, n = pb.shape
    d = D_HEADS
    if seq_len is None:
        first_spec = pl.BlockSpec((1, n), lambda i: (0, 0))
        halo_spec = pl.BlockSpec((8, n), lambda i: (jnp.maximum(i * (tm // 8) - 1, 0), 0))
    else:
        first_spec = pl.BlockSpec((tm, n), lambda i: (i, 0))
        halo_spec = pl.BlockSpec((8, n), lambda i: (0, 0))

    def full(shape):
        return pl.BlockSpec(shape, lambda i: (0,) * len(shape))

    def vec(z):
        return z.reshape(1, -1)

    outs = pl.pallas_call(
        functools.partial(_rwkv_prep_kernel, seq_len=seq_len),
        grid=(m // tm,),
        in_specs=[pl.BlockSpec((tm, n), lambda i: (i, 0)), halo_spec, first_spec,
                  full((1, n)), full((1, d)), full((D_DECAY_LORA, d)), full((1, d)), full((D_AAA_LORA, d)),
                  full((n - 3 * d - D_DECAY_LORA - D_AAA_LORA, d)), full((1, d)), full((1, d)), full((1, d)),
                  full((d, d))],
        out_specs=[pl.BlockSpec((tm, d), lambda i: (i, 0))] * 8,
        out_shape=[jax.ShapeDtypeStruct((m, d), F32)] * 8,
        compiler_params=_cparams(("parallel",)),
        name="rwkv_prep",
    )(pb, pb, first, vec(prm['rw_mu']), vec(prm['rw_w0']), prm['rw_w2'], vec(prm['rw_a0']), prm['rw_a2'],
      prm['rw_g2'], vec(prm['rw_k_k']), vec(prm['rw_k_a']), vec(prm['rw_r_k']), gmat)
    return outs


def _rwkv_scan_kernel(r_ref, lw_ref, k_ref, v_ref, kk_ref, b_ref, kkT_ref, lwT_ref, vT_ref, s0_ref,
                      y_ref, sfin_ref, s_scr, *, chunk, chunks_per_step):
    j = pl.program_id(1)

    @pl.when(j == 0)
    def _():
        s_scr[...] = s0_ref[0]

    step = chunk * chunks_per_step
    d = D_HEADS
    si = lax.broadcasted_iota(jnp.int32, (step, step), 0)
    sj = lax.broadcasted_iota(jnp.int32, (step, step), 1)
    same = (si // chunk) == (sj // chunk)
    lbd = jnp.where(same & (si >= sj), 1.0, 0.0).astype(BF16)
    ubd = jnp.where(same & (si <= sj), 1.0, 0.0).astype(BF16)
    lw = lw_ref[...]
    lwT = lwT_ref[0]
    cl = sum(_mm(lbd, p) for p in _split3(lw))
    clT = sum(_mm(p, ubd) for p in _split3(lwT))
    tots = [cl[(c + 1) * chunk - 1:(c + 1) * chunk, :] for c in range(chunks_per_step)]
    tot_full = jnp.concatenate([jnp.broadcast_to(t, (chunk, d)) for t in tots], axis=0)
    e_neg = jnp.exp(-cl)
    e_end = jnp.exp(tot_full - cl)
    b = b_ref[...]
    k = k_ref[...]
    rho = r_ref[...] * jnp.exp(cl)
    beta, kap = b * e_neg, k * e_neg
    b_end, k_end = (b * e_end).astype(BF16), (k * e_end).astype(BF16)
    alphaT = kkT_ref[0] * jnp.exp(clT - lwT)
    vT = vT_ref[0].astype(BF16)
    v = v_ref[...].astype(BF16)

    ii = lax.broadcasted_iota(jnp.int32, (chunk, chunk), 0)
    jj = lax.broadcasted_iota(jnp.int32, (chunk, chunk), 1)
    low, upper = ii >= jj, ii < jj
    hd = HEAD_DIM
    units = [(slice(c * chunk, (c + 1) * chunk), slice(h * hd, (h + 1) * hd))
             for c in range(chunks_per_step) for h in range(N_HEADS)]
    bk = [jnp.concatenate([beta[rs, cs], kap[rs, cs]], axis=0).astype(BF16) for rs, cs in units]
    aT = [alphaT[cs, rs] for rs, cs in units]
    g1 = [_mm(x, a.astype(BF16)) for x, a in zip(bk, aT)]
    lt = [jnp.where(upper, g[:chunk], 0.0).astype(BF16) for g in g1]
    akT = [jnp.where(upper, g[chunk:], 0.0).astype(BF16) for g in g1]
    g2 = [_mm_nt(rho[rs, cs].astype(BF16), x) for (rs, cs), x in zip(units, bk)]
    arb = [jnp.where(low, g[:, :chunk], 0.0).astype(BF16) for g in g2]
    ark = [jnp.where(low, g[:, chunk:], 0.0).astype(BF16) for g in g2]
    z = [jnp.concatenate([a, _mm(vT[cs, rs], x)], axis=0) for (rs, cs), a, x in zip(units, aT, akT)]
    z = [x - _mm(x.astype(BF16), l) for x, l in zip(z, lt)]
    pw = lt
    n = 2
    while n < chunk:
        pw = [_mm(p, p).astype(BF16) for p in pw]
        z = [x + _mm(x.astype(BF16), p) for x, p in zip(z, pw)]
        n *= 2
    pz = [(-x).astype(BF16) for x in z]
    g3 = [_mm_nt(a, p) for a, p in zip(arb, pz)]
    g4 = [_mm(p, b_end[rs, cs]) for (rs, cs), p in zip(units, pz)]
    yin = [g[:, hd:] + _mm(a, v[rs, cs]) for (rs, cs), g, a in zip(units, g3, ark)]
    vk = [_mm(vT[cs, rs], k_end[rs, cs]) for rs, cs in units]
    state = [s_scr[h] for h in range(N_HEADS)]
    for c in range(chunks_per_step):
        etot = jnp.exp(tots[c])
        for h in range(N_HEADS):
            u = c * N_HEADS + h
            rs, cs = units[u]
            s = state[h]
            sb = s.astype(BF16)
            y_ref[rs, cs] = _mm_nt((rho[rs, cs] + g3[u][:, :hd]).astype(BF16), sb) + yin[u]
            state[h] = s * etot[:, cs] + _mm(sb, g4[u][:hd].astype(BF16)) + g4[u][hd:] + vk[u]
    for h in range(N_HEADS):
        s_scr[h] = state[h]

    @pl.when(j == pl.num_programs(1) - 1)
    def _():
        sfin_ref[0] = s_scr[...]


def _rwkv_scan(r, lw, k, v, kk, b, s0, seq_len, chunks_per_step):
    m, d = r.shape
    nseq = m // seq_len
    step = WKV_CHUNK * chunks_per_step
    nsteps = seq_len // step

    def tr(z):
        return z.reshape(nseq, seq_len, d).transpose(0, 2, 1)

    tm_spec = pl.BlockSpec((step, d), lambda i, j: (i * nsteps + j, 0))
    fm_spec = pl.BlockSpec((1, d, step), lambda i, j: (i, 0, j))
    st_spec = pl.BlockSpec((1, N_HEADS, HEAD_DIM, HEAD_DIM), lambda i, j: (i, 0, 0, 0))
    return pl.pallas_call(
        functools.partial(_rwkv_scan_kernel, chunk=WKV_CHUNK, chunks_per_step=chunks_per_step),
        grid=(nseq, nsteps),
        in_specs=[tm_spec] * 6 + [fm_spec] * 3 + [st_spec],
        out_specs=[tm_spec, st_spec],
        out_shape=[jax.ShapeDtypeStruct((m, d), F32),
                   jax.ShapeDtypeStruct((nseq, N_HEADS, HEAD_DIM, HEAD_DIM), F32)],
        scratch_shapes=[pltpu.VMEM((N_HEADS, HEAD_DIM, HEAD_DIM), F32)],
        compiler_params=_cparams(("parallel", "arbitrary")),
        name="rwkv_scan",
    )(r, lw, k, v, kk, b, tr(kk), tr(lw), tr(v), s0)


def _l0_out_kernel(y_ref, bonus_ref, g_ref, oa_ref, h_ref, gnw_ref, gnb_ref, gm_ref, wa_ref, wb_ref,
                   gpost_ref, o_ref):
    y = y_ref[...]
    gmat = gm_ref[...]
    inv = 1.0 / HEAD_DIM
    yc = y - _head_sums(y, gmat) * inv
    yn = yc * lax.rsqrt(_head_sums(yc * yc, gmat) * inv + GN_EPS) * gnw_ref[...] + gnb_ref[...]
    ob = (yn + bonus_ref[...]) * g_ref[...]
    mix = _mm(oa_ref[...].astype(BF16), wa_ref[...]) + _mm(ob.astype(BF16), wb_ref[...])
    o_ref[...] = h_ref[...] + _rms(mix, gpost_ref[...])


def _l0_out(y, bonus, g, oa, h, prm, gmat, w_out, tm):
    m, dm = h.shape
    d = D_HEADS

    def rowblk(w):
        return pl.BlockSpec((tm, w), lambda i: (i, 0))

    def full(shape):
        return pl.BlockSpec(shape, lambda i: (0,) * len(shape))

    return pl.pallas_call(
        _l0_out_kernel,
        grid=(m // tm,),
        in_specs=[rowblk(d), rowblk(d), rowblk(d), rowblk(d), rowblk(dm), full((1, d)), full((1, d)),
                  full((d, d)), full((d, dm)), full((d, dm)), full((1, dm))],
        out_specs=rowblk(dm),
        out_shape=jax.ShapeDtypeStruct((m, dm), F32),
        compiler_params=_cparams(("parallel",)),
        name="l0_out",
    )(y, bonus, g, oa, h, prm['rw_gn_w'].reshape(1, d), prm['rw_gn_b'].reshape(1, d), gmat,
      w_out[:d], w_out[d:], prm['l0_norm_post'].reshape(1, dm))


FFN_COLS = 256
FFN_HALO = 16


def _ffn_kernel(h_ref, halo_ref, gpre_ref, wg_ref, wv_ref, cwg_ref, cwv_ref, cbg_ref, cbv_ref, pg_ref, pv_ref,
                wd_ref, gpost_ref, o_ref, xn_scr, acc_scr, *, seq_len):
    i = pl.program_id(0)
    c = pl.program_id(1)
    tm = h_ref.shape[0]

    @pl.when(c == 0)
    def _():
        xn_scr[0:tm] = _rms(h_ref[...], gpre_ref[...]).astype(BF16)
        if seq_len is None:
            xn_scr[tm:tm + FFN_HALO] = _rms(halo_ref[...], gpre_ref[...]).astype(BF16)
        acc_scr[...] = jnp.zeros(acc_scr.shape, F32)

    row = lax.broadcasted_iota(jnp.int32, (tm, 1), 0)
    pos = row if seq_len is None else row % seq_len

    def conv(w_ref, cw_ref, cb_ref, p_ref):
        up = _mm(xn_scr[...], w_ref[...])
        u = up[:tm]
        m1 = pltpu.roll(u, 1, axis=0)
        m2 = pltpu.roll(u, 2, axis=0)
        if seq_len is None:
            last = tm + FFN_HALO
            p0 = jnp.where(i == 0, p_ref[6:7], up[last - 2:last - 1])
            p1 = jnp.where(i == 0, p_ref[7:8], up[last - 1:last])
            m1 = jnp.concatenate([jnp.where(pos[:8] == 0, p1, m1[:8]), m1[8:]], axis=0)
            m2 = jnp.concatenate([jnp.where(pos[:8] == 0, p0, jnp.where(pos[:8] == 1, p1, m2[:8])), m2[8:]], axis=0)
        else:
            m1 = jnp.where(pos == 0, p_ref[1], m1)
            m2 = jnp.where(pos == 0, p_ref[0], jnp.where(pos == 1, p_ref[1], m2))
        return cb_ref[...] + cw_ref[0:1] * m2 + cw_ref[1:2] * m1 + cw_ref[2:3] * u

    gate = conv(wg_ref, cwg_ref, cbg_ref, pg_ref)
    val = conv(wv_ref, cwv_ref, cbv_ref, pv_ref)
    acc_scr[...] += _mm((jax.nn.gelu(gate) * val).astype(BF16), wd_ref[...])

    @pl.when(c == pl.num_programs(1) - 1)
    def _():
        o_ref[...] = h_ref[...] + _rms(acc_scr[...], gpost_ref[...])


def _ffn(h, prev_rows, g_pre, w_up, conv_w, conv_b, w_down, g_post, tm, seq_len):
    m, dm = h.shape
    dff = w_down.shape[0]
    n2 = 2 * dff
    assert dff % FFN_COLS == 0
    nc = dff // FFN_COLS
    if seq_len is None:
        prev = jnp.pad(prev_rows[0], ((6, 0), (0, 0)))

        def p_spec(off):
            return pl.BlockSpec((8, FFN_COLS), lambda i, c: (0, off + c))

        halo_spec = pl.BlockSpec((FFN_HALO, dm), lambda i, c: (jnp.maximum(i * (tm // FFN_HALO) - 1, 0), 0))
        xn_rows = tm + FFN_HALO
    else:
        prev = jnp.stack([jnp.repeat(prev_rows[:, 0], seq_len, axis=0), jnp.repeat(prev_rows[:, 1], seq_len, axis=0)])

        def p_spec(off):
            return pl.BlockSpec((2, tm, FFN_COLS), lambda i, c: (0, i, off + c))

        halo_spec = pl.BlockSpec((FFN_HALO, dm), lambda i, c: (0, 0))
        xn_rows = tm

    def col_spec(rows, off):
        return pl.BlockSpec((rows, FFN_COLS), lambda i, c: (0, off + c))

    def full(shape):
        return pl.BlockSpec(shape, lambda i, c: (0,) * len(shape))

    cb = conv_b.reshape(1, n2)
    return pl.pallas_call(
        functools.partial(_ffn_kernel, seq_len=seq_len),
        grid=(m // tm, nc),
        in_specs=[pl.BlockSpec((tm, dm), lambda i, c: (i, 0)), halo_spec, full((1, dm)),
                  col_spec(dm, 0), col_spec(dm, nc), col_spec(CONV_W, 0), col_spec(CONV_W, nc),
                  col_spec(1, 0), col_spec(1, nc), p_spec(0), p_spec(nc),
                  pl.BlockSpec((FFN_COLS, dm), lambda i, c: (c, 0)), full((1, dm))],
        out_specs=pl.BlockSpec((tm, dm), lambda i, c: (i, 0)),
        out_shape=jax.ShapeDtypeStruct((m, dm), F32),
        scratch_shapes=[pltpu.VMEM((xn_rows, dm), BF16), pltpu.VMEM((tm, dm), F32)],
        compiler_params=_cparams(("parallel", "arbitrary")),
        name="ffn",
    )(h, h, g_pre.reshape(1, dm), w_up, w_up, conv_w, conv_w, cb, cb, prev, prev, w_down, g_post.reshape(1, dm))


def _gmlp_kernel(h_ref, gpre_ref, win_ref, lnw_ref, lnb_ref, wmix_ref, bias_ref, wout_ref, gpost_ref,
                 o_ref, cv_ref, gated_scr):
    h = h_ref[...]
    tm = h.shape[0]
    dc = wout_ref.shape[0]
    rm = wmix_ref.shape[1]
    gw = dc // GMLP_GROUPS
    z = jax.nn.gelu(_mm(_rms(h, gpre_ref[...]).astype(BF16), win_ref[...]))
    v = z[:, dc:]
    vc = v - jnp.mean(v, axis=-1, keepdims=True)
    vn = vc * lax.rsqrt(jnp.mean(vc * vc, axis=-1, keepdims=True) + LN_EPS) * lnw_ref[...] + lnb_ref[...]
    cv_ref[...] = vn
    for s in range(tm // rm):
        rows = slice(s * rm, (s + 1) * rm)
        for g in range(GMLP_GROUPS):
            cols = slice(g * gw, (g + 1) * gw)
            mixed = _mm(wmix_ref[g], vn[rows, cols].astype(BF16)) + bias_ref[:, cols]
            gated_scr[rows, cols] = (z[rows, cols] * mixed).astype(BF16)
    o_ref[...] = h + _rms(_mm(gated_scr[...], wout_ref[...]), gpost_ref[...])


def _gmlp(h, g_pre, w_in, ln_w, ln_b, wmix, bias, w_out, g_post, tm):
    m, dm = h.shape
    dc = w_out.shape[0]
    rm = wmix.shape[1]

    def full(shape):
        return pl.BlockSpec(shape, lambda i: (0,) * len(shape))

    return pl.pallas_call(
        _gmlp_kernel,
        grid=(m // tm,),
        in_specs=[pl.BlockSpec((tm, dm), lambda i: (i, 0)), full((1, dm)), full((dm, 2 * dc)), full((1, dc)),
                  full((1, dc)), full((GMLP_GROUPS, rm, rm)), full((rm, dc)), full((dc, dm)), full((1, dm))],
        out_specs=[pl.BlockSpec((tm, dm), lambda i: (i, 0)), pl.BlockSpec((tm, dc), lambda i: (i, 0))],
        out_shape=[jax.ShapeDtypeStruct((m, dm), F32), jax.ShapeDtypeStruct((m, dc), F32)],
        scratch_shapes=[pltpu.VMEM((tm, dc), BF16)],
        compiler_params=_cparams(("parallel",)),
        name="gmlp",
    )(h, g_pre.reshape(1, dm), w_in, ln_w.reshape(1, dc), ln_b.reshape(1, dc), wmix, bias, w_out,
      g_post.reshape(1, dm))


def _trunk(x, attend, prev_xn, wkv0, conv0, prm, w, *, tile, long_seq):
    nseq, t, dm = x.shape
    m = nseq * t
    h = x.reshape(m, dm)
    seq_len = None if long_seq else t

    g0 = prm['l0_norm_pre']
    q, k_new, v_new = _norm_matmul_split(h, g0, w['w_qkv'], tile, 3)
    pb = _norm_matmul(h, g0, w['w_b'], tile, w['w_b'].shape[1] // 2)
    shift_new, prev_proj = _shift_rows(x[:, -1], g0, prev_xn, w['w_b'])
    first = prev_proj if long_seq else jnp.repeat(prev_proj, t, axis=0)
    r, lw, kmod, vv, kk, bb, gate, bonus = _rwkv_prep(pb, first, prm, w['gmat'], tile, seq_len)
    o_a = attend(q, k_new, v_new)

    t_pad = -(-t // WKV_CHUNK) * WKV_CHUNK

    def padt(z):
        if t_pad == t:
            return z
        return jnp.pad(z.reshape(nseq, t, -1), ((0, 0), (0, t_pad - t), (0, 0))).reshape(nseq * t_pad, -1)

    y, wkv_new = _rwkv_scan(padt(r), padt(lw), padt(kmod), padt(vv), padt(kk), padt(bb), wkv0, t_pad,
                            4 if t_pad % (4 * WKV_CHUNK) == 0 else 1)
    if t_pad != t:
        y = y.reshape(nseq, t_pad, -1)[:, :t].reshape(m, -1)
    h = _l0_out(y, bonus, gate, o_a, h, prm, w['gmat'], w['w_out0'], tile)

    conv_rows = []
    ffn_tile = min(m, 1024)

    def ffn(h, layer):
        g_pre = prm['ffn_norm_pre'][layer]
        tail = h.reshape(nseq, t, dm)[:, t - min(t, 8):].reshape(-1, dm)
        up_tail = _norm_matmul(tail, g_pre, w['w_up'][layer], tail.shape[0], 512)
        conv_rows.append(up_tail.reshape(nseq, min(t, 8), -1)[:, -(CONV_W - 1):])
        return _ffn(h, conv0[layer], g_pre, w['w_up'][layer], prm['ffn_conv_w'][layer], prm['ffn_conv_b'][layer],
                    w['w_down'][layer], prm['ffn_norm_post'][layer], ffn_tile, seq_len)

    h = ffn(h, 0)

    if long_seq:
        wmix, bias = w['wmix'], w['gbias']
    else:
        eye = jnp.eye(tile // t, dtype=F32)
        wmix = jnp.einsum('ab,gij->gaibj', eye, w['wmix'][:, :t, :t].astype(F32))
        wmix = wmix.reshape(GMLP_GROUPS, tile, tile).astype(BF16)
        bias = jnp.tile(w['gbias'][:t], (tile // t, 1))
    h, chunk_v = _gmlp(h, prm['l1_norm_pre'], w['w_in1'], prm['gm_ln_w'], prm['gm_ln_b'], wmix, bias,
                       w['w_out1'], prm['l1_norm_post'], tile)
    h = ffn(h, 1)
    return (h.reshape(nseq, t, dm), k_new.reshape(nseq, t, N_HEADS, HEAD_DIM),
            v_new.reshape(nseq, t, N_HEADS, HEAD_DIM), wkv_new, shift_new, chunk_v.reshape(nseq, t, -1),
            jnp.stack(conv_rows))


def kernel(x_prompt, x_sample, cache_k, cache_v, state_wkv, state_shift, state_ffn_conv, page_table,
           l0_norm_pre, l0_norm_post, l0_w_in, rw_mu, rw_w0, rw_w2, rw_a0, rw_a2, rw_g2, rw_k_k, rw_k_a,
           rw_r_k, rw_gn_w, rw_gn_b, l0_w_out, l1_norm_pre, l1_norm_post, l1_w_in, gm_ln_w, gm_ln_b,
           gm_ws, gm_bs, l1_w_out, ffn_norm_pre, ffn_norm_post, ffn_w_up, ffn_conv_w, ffn_conv_b,
           ffn_w_down):
    prm = dict(l0_norm_pre=l0_norm_pre, l0_norm_post=l0_norm_post, rw_mu=rw_mu, rw_w0=rw_w0, rw_w2=rw_w2,
               rw_a0=rw_a0, rw_a2=rw_a2, rw_g2=rw_g2, rw_k_k=rw_k_k, rw_k_a=rw_k_a, rw_r_k=rw_r_k,
               rw_gn_w=rw_gn_w, rw_gn_b=rw_gn_b, l1_norm_pre=l1_norm_pre, l1_norm_post=l1_norm_post,
               gm_ln_w=gm_ln_w, gm_ln_b=gm_ln_b, ffn_norm_pre=ffn_norm_pre, ffn_norm_post=ffn_norm_post,
               ffn_conv_w=ffn_conv_w, ffn_conv_b=ffn_conv_b)
    d_a = D_HEADS
    dc = l1_w_out.shape[0]
    tri = jnp.tril(jnp.ones((GMLP_CHUNK, GMLP_CHUNK), dtype=bool))
    hd = jnp.arange(D_HEADS) // HEAD_DIM
    w = dict(
        w_qkv=l0_w_in[:, :3 * d_a].astype(BF16),
        w_b=l0_w_in[:, 3 * d_a:].astype(BF16),
        w_out0=l0_w_out.astype(BF16),
        w_up=ffn_w_up.astype(BF16),
        w_down=ffn_w_down.astype(BF16),
        w_in1=l1_w_in.astype(BF16),
        w_out1=l1_w_out.astype(BF16),
        wmix=jnp.where(tri[None], gm_ws, 0.0).astype(BF16),
        gbias=jnp.repeat(gm_bs.T, dc // GMLP_GROUPS, axis=1),
        gmat=(hd[:, None] == hd[None, :]).astype(BF16),
    )
    slopes = jnp.exp2(-8.0 * jnp.arange(1, N_HEADS + 1, dtype=F32) / N_HEADS)

    bp, tp, dm = x_prompt.shape
    assert bp == 1
    dt = x_prompt.dtype
    depth = ffn_w_up.shape[0]
    wkv0 = jnp.zeros((bp, N_HEADS, HEAD_DIM, HEAD_DIM), dt)
    shift0 = jnp.zeros((bp, dm), dt)
    conv0 = jnp.zeros((depth, bp, CONV_W - 1, ffn_w_up.shape[2]), dt)

    def attend_prompt(q, k, v):
        return _moba_prompt(q, k, v, slopes)

    (y_prompt, k_prompt, v_prompt, wkv_prompt, shift_prompt, _unused_chunk_v,
     ffn_conv_prompt) = _trunk(x_prompt, attend_prompt, shift0, wkv0, conv0, prm, w, tile=512, long_seq=True)

    bs, ts, _ = x_sample.shape

    def attend_sample(q, k, v):
        o = _moba_sample(q.reshape(bs, ts, -1), k.reshape(bs, ts, -1), v.reshape(bs, ts, -1), cache_k, cache_v,
                         page_table, slopes)
        return o.reshape(bs * ts, -1)

    (y_sample, k_sample, v_sample, wkv_sample, shift_sample, chunk_v_sample,
     ffn_conv_sample) = _trunk(x_sample, attend_sample, state_shift, state_wkv, state_ffn_conv, prm, w,
                               tile=bs * ts, long_seq=False)

    return (y_prompt, y_sample, k_prompt, v_prompt, k_sample, v_sample, wkv_prompt, wkv_sample,
            shift_prompt, shift_sample, chunk_v_sample, ffn_conv_prompt, ffn_conv_sample)
```
